```python
import jax, jax.numpy as jnp
from jax import lax
import numpy as np

D_MODEL = 1024
BATCH = 8
SEQ = 2048
DEPTH = 2
DEC_BATCH = 32
DEC_SEQ = 1
PAST_LEN = 8192
PAGE_SIZE = 128

N_EVEN = (DEPTH + 1) // 2
N_ODD = DEPTH // 2
SB_HEADS = 8
SB_HEAD_DIM = 64
SB_WIDTH = SB_HEADS * SB_HEAD_DIM
SB_BLOCK = 128
SB_BIAS_INIT = -6.0
SSD_HEADS = 8
SSD_HEAD_DIM = 64
SSD_WIDTH = SSD_HEADS * SSD_HEAD_DIM
SSD_STATE = 128
SSD_GROUPS = 2
SSD_CONV = 4
SSD_CONV_CH = SSD_WIDTH + 2 * SSD_GROUPS * SSD_STATE
SSD_CHUNK = 128
EVEN_IN = 3 * SB_WIDTH + SSD_WIDTH + SSD_CONV_CH + SSD_HEADS
EVEN_MIX = SB_WIDTH + SSD_WIDTH
HG_EXPAND = 128
HG_HEADS = D_MODEL // HG_EXPAND
HG_KEY = HG_EXPAND
HG_VAL = D_MODEL // HG_HEADS
HG_WIDTH = HG_HEADS * HG_KEY
HG_VWIDTH = HG_HEADS * HG_VAL
HG_CHUNK = 32
ODD_IN = 2 * HG_WIDTH + 2 * HG_VWIDTH
D_FF = 4 * D_MODEL
EPS = 1e-6

kernel_name = 'hybrid_stickbreak_ssd_hgrn2_step'


def _split(a, sizes):
    idx, acc = [], 0
    for s in sizes[:-1]:
        acc += s
        idx.append(acc)
    return jnp.split(a, idx, axis=-1)


def rmsnorm(x, w):
    xf = x.astype(jnp.float32)
    y = xf * lax.rsqrt(jnp.mean(xf * xf, axis=-1, keepdims=True) + EPS)
    return (y * w.astype(jnp.float32)).astype(x.dtype)


def sq_relu_mlp(h, w1, w2):
    return jnp.square(jax.nn.relu(h @ w1)) @ w2


def _pad_time(a, pad):
    return jnp.pad(a, [(0, 0), (0, pad)] + [(0, 0)] * (a.ndim - 2))


def stick_breaking(q, k, v, bias):
    bsz, tq, nh, dh = q.shape
    tk = k.shape[1]
    q_off = tk - tq
    qb = min(SB_BLOCK, tq)
    nb = -(-tq // qb)
    pad = nb * qb - tq
    qblocks = jnp.moveaxis(_pad_time(q, pad).reshape(bsz, nb, qb, nh, dh), 1, 0)
    kpos = jnp.arange(tk)
    v32 = v.astype(jnp.float32)
    scale = dh ** -0.5
    b32 = bias.astype(jnp.float32)[None, :, None, None]

    def block(args):
        qblk, bi = args
        qpos = q_off + bi * qb + jnp.arange(qb)
        mask = kpos[None, :] < qpos[:, None]
        z = jnp.einsum('bqhd,bkhd->bhqk', qblk, k).astype(jnp.float32) * scale + b32
        log_1m = jnp.where(mask, jax.nn.log_sigmoid(-z), 0.0)
        rem = lax.cumsum(log_1m, axis=3, reverse=True) - log_1m
        w = jnp.where(mask, jnp.exp(jax.nn.log_sigmoid(z) + rem), 0.0)
        return jnp.einsum('bhqk,bkhd->bqhd', w, v32)

    out = lax.map(block, (qblocks, jnp.arange(nb)))
    return jnp.moveaxis(out, 0, 1).reshape(bsz, nb * qb, nh, dh)[:, :tq]


def causal_conv(xbc, conv_state, w, b):
    full = jnp.concatenate([conv_state.astype(xbc.dtype), xbc], axis=1)
    out = lax.conv_general_dilated(full, w[:, None, :].astype(full.dtype), window_strides=(1,),
                                   padding='VALID', dimension_numbers=('NWC', 'WIO', 'NWC'),
                                   feature_group_count=full.shape[-1])
    return out + b.astype(out.dtype), full[:, full.shape[1] - (SSD_CONV - 1):]


def ssd_chunked(x, dt, a, bm, cm, h0):
    bsz, t, nh, hp = x.shape
    rep = nh // bm.shape[2]
    bh = jnp.repeat(bm, rep, axis=2).astype(jnp.float32)
    ch = jnp.repeat(cm, rep, axis=2).astype(jnp.float32)
    L = min(SSD_CHUNK, t)
    nc = -(-t // L)
    pad = nc * L - t
    xc = _pad_time(x.astype(jnp.float32), pad).reshape(bsz, nc, L, nh, hp)
    dtc = _pad_time(dt, pad).reshape(bsz, nc, L, nh)
    bh = _pad_time(bh, pad).reshape(bsz, nc, L, nh, -1)
    ch = _pad_time(ch, pad).reshape(bsz, nc, L, nh, -1)
    cum = jnp.cumsum(dtc * a, axis=2)
    tri = jnp.arange(L)[:, None] >= jnp.arange(L)[None, :]
    seg = cum[:, :, :, None, :] - cum[:, :, None, :, :]
    decay = jnp.exp(jnp.where(tri[None, None, :, :, None], seg, -jnp.inf))
    scores = jnp.einsum('bcthn,bcshn->bctsh', ch, bh) * decay * dtc[:, :, None]
    y_intra = jnp.einsum('bctsh,bcshp->bcthp', scores, xc)
    last = cum[:, :, -1]
    wst = jnp.exp(last[:, :, None] - cum) * dtc
    chunk_states = jnp.einsum('bclhn,bclh,bclhp->bchpn', bh, wst, xc)

    def step(h, inp):
        st, dl = inp
        return jnp.exp(dl)[:, :, None, None] * h + st, h

    h_t, h_in = lax.scan(step, h0.astype(jnp.float32),
                         (jnp.moveaxis(chunk_states, 1, 0), jnp.moveaxis(last, 1, 0)))
    h_in = jnp.moveaxis(h_in, 0, 1)
    y_inter = jnp.einsum('bclhn,bchpn,bclh->bclhp', ch, h_in, jnp.exp(cum))
    y = (y_intra + y_inter).reshape(bsz, nc * L, nh, hp)[:, :t]
    return y, h_t


def hgrn2_chunked(q, log_f, k_in, v, s0):
    bsz, t, nh, _ = q.shape
    L = min(HG_CHUNK, t)
    nc = -(-t // L)
    pad = nc * L - t

    def chunks(a):
        a = _pad_time(a.astype(jnp.float32), pad)
        return jnp.moveaxis(a.reshape(bsz, nc, L, nh, a.shape[-1]), 1, 0)

    tri = jnp.arange(L)[:, None] >= jnp.arange(L)[None, :]

    def step(s, inp):
        qc, gc, kc, vc = inp
        b = jnp.cumsum(gc, axis=1)
        o_inter = jnp.einsum('blhk,bhkv->blhv', qc * jnp.exp(b), s)
        diff = b[:, :, None] - b[:, None, :]
        decay = jnp.exp(jnp.where(tri[None, :, :, None, None], diff, -jnp.inf))
        att = jnp.einsum('bthk,btshk,bshk->bths', qc, decay, kc)
        o_intra = jnp.einsum('bths,bshv->bthv', att, vc)
        b_last = b[:, -1]
        s_new = jnp.exp(b_last)[..., None] * s + jnp.einsum(
            'bshk,bshv->bhkv', jnp.exp(b_last[:, None] - b) * kc, vc)
        return s_new, o_inter + o_intra

    s_t, o = lax.scan(step, s0.astype(jnp.float32), (chunks(q), chunks(log_f), chunks(k_in), chunks(v)))
    o = jnp.moveaxis(o, 0, 1).reshape(bsz, nc * L, nh, v.shape[-1])[:, :t]
    return o, s_t


def even_mixer(h, past_k, past_v, conv_state, ssm_state, w_in, sb_bias, conv_w, conv_b, dt_bias,
               a_log, d_skip, ssd_norm, w_out):
    bsz, t, _ = h.shape
    q, k, v, z, xbc, dt_raw = _split(h @ w_in, [SB_WIDTH] * 3 + [SSD_WIDTH, SSD_CONV_CH, SSD_HEADS])
    q = q.reshape(bsz, t, SB_HEADS, SB_HEAD_DIM)
    k = k.reshape(bsz, t, SB_HEADS, SB_HEAD_DIM)
    v = v.reshape(bsz, t, SB_HEADS, SB_HEAD_DIM)
    if past_k is None:
        keys, vals = k, v
    else:
        keys = jnp.concatenate([past_k.astype(k.dtype), k], axis=1)
        vals = jnp.concatenate([past_v.astype(v.dtype), v], axis=1)
    o_sb = stick_breaking(q, keys, vals, sb_bias).reshape(bsz, t, SB_WIDTH)
    xbc, new_conv = causal_conv(xbc, conv_state, conv_w, conv_b)
    xbc = jax.nn.silu(xbc)
    xs, bm, cm = _split(xbc, [SSD_WIDTH, SSD_GROUPS * SSD_STATE, SSD_GROUPS * SSD_STATE])
    xs = xs.reshape(bsz, t, SSD_HEADS, SSD_HEAD_DIM)
    bm = bm.reshape(bsz, t, SSD_GROUPS, SSD_STATE)
    cm = cm.reshape(bsz, t, SSD_GROUPS, SSD_STATE)
    dt = jax.nn.softplus(dt_raw.astype(jnp.float32) + dt_bias.astype(jnp.float32))
    a = -jnp.exp(a_log.astype(jnp.float32))
    y, new_ssm = ssd_chunked(xs, dt, a, bm, cm, ssm_state)
    y = y + d_skip.astype(jnp.float32)[:, None] * xs.astype(jnp.float32)
    y = rmsnorm(y.reshape(bsz, t, SSD_WIDTH) * jax.nn.silu(z.astype(jnp.float32)), ssd_norm)
    out = jnp.concatenate([o_sb.astype(y.dtype), y], axis=-1).astype(h.dtype) @ w_out
    return out, k, v, new_conv, new_ssm


def odd_mixer(h, state, lb, w_in, hg_norm, w_out):
    bsz, t, _ = h.shape
    q, fpre, inp, g = _split(h @ w_in, [HG_WIDTH, HG_WIDTH, HG_VWIDTH, HG_VWIDTH])
    lb = lb.reshape(HG_HEADS, HG_KEY)
    fpre = fpre.astype(jnp.float32).reshape(bsz, t, HG_HEADS, HG_KEY)
    log_f = jnp.logaddexp(jnp.log(lb), jnp.log1p(-lb) + jax.nn.log_sigmoid(fpre))
    k_in = (1.0 - lb) * jax.nn.sigmoid(-fpre)
    o, new_state = hgrn2_chunked(q.reshape(bsz, t, HG_HEADS, HG_KEY), log_f, k_in,
                                 inp.reshape(bsz, t, HG_HEADS, HG_VAL), state)
    o = rmsnorm(o.reshape(bsz, t, HG_VWIDTH), hg_norm) * jax.nn.silu(g.astype(jnp.float32))
    return o.astype(h.dtype) @ w_out, new_state


def setup_inputs(seed: int = 0) -> dict:
    key = jax.random.key(seed)
    ks = jax.random.split(key, 32)
    f32 = jnp.float32
    n_pages = PAST_LEN // PAGE_SIZE
    n_used = DEC_BATCH * n_pages
    n_phys = n_used + (n_used + 3) // 4

    def nrm(k, shape, scale):
        return jax.random.normal(k, shape, f32) * scale

    dt0 = jnp.exp(jax.random.uniform(ks[13], (N_EVEN, SSD_HEADS), f32, np.log(1e-3), np.log(1e-1)))
    return {
        'x_prompt': nrm(ks[0], (BATCH, SEQ, D_MODEL), 1.0),
        'x_sample': nrm(ks[1], (DEC_BATCH, DEC_SEQ, D_MODEL), 1.0),
        'cache_k': nrm(ks[2], (N_EVEN, n_phys, PAGE_SIZE, SB_HEADS, SB_HEAD_DIM), 1.0),
        'cache_v': nrm(ks[3], (N_EVEN, n_phys, PAGE_SIZE, SB_HEADS, SB_HEAD_DIM), 1.0),
        'page_table': jax.random.permutation(ks[4], n_phys)[:n_used].reshape(DEC_BATCH, n_pages).astype(jnp.int32),
        'state_conv': nrm(ks[5], (N_EVEN, DEC_BATCH, SSD_CONV - 1, SSD_CONV_CH), 1.0),
        'state_ssm': nrm(ks[6], (N_EVEN, DEC_BATCH, SSD_HEADS, SSD_HEAD_DIM, SSD_STATE), 0.1),
        'state_hgrn': nrm(ks[7], (N_ODD, DEC_BATCH, HG_HEADS, HG_KEY, HG_VAL), 0.5),
        'norm_mix': 1.0 + nrm(ks[8], (DEPTH, D_MODEL), 0.02),
        'norm_ffn': 1.0 + nrm(ks[9], (DEPTH, D_MODEL), 0.02),
        'norm_final': 1.0 + nrm(ks[10], (D_MODEL,), 0.02),
        'w_in_even': nrm(ks[11], (N_EVEN, D_MODEL, EVEN_IN), D_MODEL ** -0.5),
        'sb_bias': SB_BIAS_INIT + nrm(ks[25], (N_EVEN, SB_HEADS), 0.5),
        'conv_w': nrm(ks[12], (N_EVEN, SSD_CONV, SSD_CONV_CH), SSD_CONV ** -0.5),
        'conv_b': nrm(ks[14], (N_EVEN, SSD_CONV_CH), 0.01),
        'dt_bias': dt0 + jnp.log(-jnp.expm1(-dt0)),
        'a_log': jnp.log(jax.random.uniform(ks[15], (N_EVEN, SSD_HEADS), f32, 1.0, 16.0)),
        'd_skip': 1.0 + nrm(ks[16], (N_EVEN, SSD_HEADS), 0.1),
        'ssd_norm': 1.0 + nrm(ks[17], (N_EVEN, SSD_WIDTH), 0.02),
        'w_out_even': nrm(ks[18], (N_EVEN, EVEN_MIX, D_MODEL), EVEN_MIX ** -0.5),
        'w_in_odd': nrm(ks[19], (N_ODD, D_MODEL, ODD_IN), D_MODEL ** -0.5),
        'hg_lb_raw': nrm(ks[20], (DEPTH, HG_WIDTH), 0.1),
        'hg_norm': 1.0 + nrm(ks[21], (N_ODD, HG_VWIDTH), 0.02),
        'w_out_odd': nrm(ks[22], (N_ODD, HG_VWIDTH, D_MODEL), HG_VWIDTH ** -0.5),
        'w_ff1': nrm(ks[23], (DEPTH, D_MODEL, D_FF), D_MODEL ** -0.5),
        'w_ff2': nrm(ks[24], (DEPTH, D_FF, D_MODEL), D_FF ** -0.5),
    }


def reference(x_prompt, x_sample, cache_k, cache_v, page_table, state_conv, state_ssm, state_hgrn,
              norm_mix, norm_ffn, norm_final, w_in_even, sb_bias, conv_w, conv_b, dt_bias, a_log,
              d_skip, ssd_norm, w_out_even, w_in_odd, hg_lb_raw, hg_norm, w_out_odd, w_ff1, w_ff2):
    p = jax.nn.softmax(hg_lb_raw.astype(jnp.float32), axis=0)
    lb_all = jnp.cumsum(p, axis=0) - p[0]
    n_seq, n_pages = page_table.shape
    hp, hs = x_prompt, x_sample
    bp = hp.shape[0]
    kp_l, vp_l, ks_l, vs_l, cp_l, cs_l, sp_l, ss_l, gp_l, gs_l = [], [], [], [], [], [], [], [], [], []
    for layer in range(DEPTH):
        li = layer // 2
        if layer % 2 == 0:
            wts = (w_in_even[li], sb_bias[li], conv_w[li], conv_b[li], dt_bias[li], a_log[li],
                   d_skip[li], ssd_norm[li], w_out_even[li])
            past_k = cache_k[li][page_table].reshape(n_seq, n_pages * PAGE_SIZE, SB_HEADS, SB_HEAD_DIM)
            past_v = cache_v[li][page_table].reshape(n_seq, n_pages * PAGE_SIZE, SB_HEADS, SB_HEAD_DIM)
            mp, kp, vp, cp, sp = even_mixer(
                rmsnorm(hp, norm_mix[layer]), None, None,
                jnp.zeros((bp, SSD_CONV - 1, SSD_CONV_CH), hp.dtype),
                jnp.zeros((bp, SSD_HEADS, SSD_HEAD_DIM, SSD_STATE), jnp.float32), *wts)
            ms, ks_, vs_, cs, ss = even_mixer(
                rmsnorm(hs, norm_mix[layer]), past_k, past_v, state_conv[li], state_ssm[li], *wts)
            kp_l.append(kp); vp_l.append(vp); ks_l.append(ks_); vs_l.append(vs_)
            cp_l.append(cp); cs_l.append(cs); sp_l.append(sp); ss_l.append(ss)
        else:
            wts = (lb_all[layer], w_in_odd[li], hg_norm[li], w_out_odd[li])
            mp, gp = odd_mixer(rmsnorm(hp, norm_mix[layer]),
                               jnp.zeros((bp, HG_HEADS, HG_KEY, HG_VAL), jnp.float32), *wts)
            ms, gs = odd_mixer(rmsnorm(hs, norm_mix[layer]), state_hgrn[li], *wts)
            gp_l.append(gp); gs_l.append(gs)
        hp = hp + mp.astype(hp.dtype)
        hs = hs + ms.astype(hs.dtype)
        hp = hp + sq_relu_mlp(rmsnorm(hp, norm_ffn[layer]), w_ff1[layer], w_ff2[layer]).astype(hp.dtype)
        hs = hs + sq_relu_mlp(rmsnorm(hs, norm_ffn[layer]), w_ff1[layer], w_ff2[layer]).astype(hs.dtype)
    y_prompt = rmsnorm(hp, norm_final)
    y_sample = rmsnorm(hs, norm_final)
    return (y_prompt, y_sample, jnp.stack(kp_l), jnp.stack(vp_l), jnp.stack(ks_l), jnp.stack(vs_l),
            jnp.stack(cp_l), jnp.stack(cs_l), jnp.stack(sp_l), jnp.stack(ss_l),
            jnp.stack(gp_l), jnp.stack(gs_l))
```

```python
import functools

import jax
import jax.numpy as jnp
from jax import lax
from jax.experimental import pallas as pl
from jax.experimental.pallas import tpu as pltpu

F32 = jnp.float32
BF16 = jnp.bfloat16

EPS = 1e-6
SB_HEADS = 8
SB_HEAD_DIM = 64
SB_WIDTH = SB_HEADS * SB_HEAD_DIM
SSD_HEADS = 8
SSD_HEAD_DIM = 64
SSD_WIDTH = SSD_HEADS * SSD_HEAD_DIM
SSD_STATE = 128
SSD_GROUPS = 2
SSD_CONV = 4
SSD_CONV_CH = SSD_WIDTH + 2 * SSD_GROUPS * SSD_STATE
HG_HEADS = 8
HG_KEY = 128
HG_VAL = 128

LANES = 128
SUBLANES = 8
VMEM_LIMIT_BYTES = 52 * 1024 * 1024

SB_TILE = 256
SSD_CHUNK = 128
HG_CHUNK = 128
HG_DIAG = 16
PROJ_COLS = 512


def _params(sem):
    return pltpu.CompilerParams(dimension_semantics=sem, vmem_limit_bytes=VMEM_LIMIT_BYTES)


def _dot(a, b):
    return jnp.dot(a, b, preferred_element_type=F32)


def _dot_nt(a, b):
    return lax.dot_general(a, b, (((1,), (1,)), ((), ())), preferred_element_type=F32)


def _dot_tn(a, b):
    return lax.dot_general(a, b, (((0,), (0,)), ((), ())), preferred_element_type=F32)


def _split3(x):
    hi = x.astype(BF16)
    r = x - hi.astype(F32)
    mid = r.astype(BF16)
    lo = (r - mid.astype(F32)).astype(BF16)
    return hi, mid, lo


def _rmsnorm(x, w):
    return x * lax.rsqrt(jnp.mean(x * x, axis=-1, keepdims=True) + EPS) * w


def _softplus(x):
    return jnp.maximum(x, 0.0) + jnp.log1p(jnp.exp(-jnp.abs(x)))


def _sigmoid_pair(x):
    e = jnp.exp(-jnp.abs(x))
    r = 1.0 / (1.0 + e)
    big, small = r, e * r
    pos = x >= 0
    return jnp.where(pos, big, small), jnp.where(pos, small, big)


def _silu(x):
    return x * _sigmoid_pair(x)[0]


def _row_to_col(row):
    n = row.shape[1]
    eye = lax.broadcasted_iota(jnp.int32, (n, n), 0) == lax.broadcasted_iota(jnp.int32, (n, n), 1)
    return jnp.sum(jnp.where(eye, jnp.broadcast_to(row, (n, n)), 0.0), axis=1, keepdims=True)


def _col_to_row(col):
    n = col.shape[0]
    eye = lax.broadcasted_iota(jnp.int32, (n, n), 0) == lax.broadcasted_iota(jnp.int32, (n, n), 1)
    return jnp.sum(jnp.where(eye, jnp.broadcast_to(col, (n, n)), 0.0), axis=0, keepdims=True)


def _even_proj_kernel(x_ref, nw_ref, w_ref, wdt_ref, wdtt_ref,
                      q_ref, k_ref, v_ref, kf_ref, vf_ref, z_ref, xbc_ref, dt_ref, dtt_ref):
    hn = _rmsnorm(x_ref[...], nw_ref[...]).astype(BF16)
    scale = SB_HEAD_DIM ** -0.5
    npair = SB_WIDTH // LANES
    q = _dot(hn, w_ref[:, 0:SB_WIDTH]) * scale
    for p in range(npair):
        q_ref[p] = q[:, p * LANES:(p + 1) * LANES].astype(BF16)
    k = _dot(hn, w_ref[:, SB_WIDTH:2 * SB_WIDTH])
    kf_ref[...] = k
    for p in range(npair):
        k_ref[p] = k[:, p * LANES:(p + 1) * LANES].astype(BF16)
    v = _dot(hn, w_ref[:, 2 * SB_WIDTH:3 * SB_WIDTH])
    vf_ref[...] = v
    for p in range(npair):
        v_ref[p] = v[:, p * LANES:(p + 1) * LANES].astype(BF16)
    z0 = 3 * SB_WIDTH
    z_ref[...] = _dot(hn, w_ref[:, z0:z0 + SSD_WIDTH])
    x0 = z0 + SSD_WIDTH
    for c in range(SSD_CONV_CH // PROJ_COLS):
        xbc_ref[:, c * PROJ_COLS:(c + 1) * PROJ_COLS] = _dot(
            hn, w_ref[:, x0 + c * PROJ_COLS:x0 + (c + 1) * PROJ_COLS])
    dt_ref[...] = _dot(hn, wdt_ref[...])[:, 0:SSD_HEADS]
    dtt_ref[...] = _dot_nt(wdtt_ref[...], hn)


def _even_proj(x, norm_w, w_in, tm):
    m, d = x.shape
    npair = SB_WIDTH // LANES
    wb = w_in.astype(BF16)
    n_main = 3 * SB_WIDTH + SSD_WIDTH + SSD_CONV_CH
    w_main = wb[:, :n_main]
    w_dt = jnp.pad(wb[:, n_main:], ((0, 0), (0, LANES - SSD_HEADS)))
    w_dtt = wb[:, n_main:].T
    full = lambda i: (0, 0)
    rows = lambda i: (i, 0)
    hp = lambda i: (0, i, 0)
    return pl.pallas_call(
        _even_proj_kernel,
        grid=(m // tm,),
        in_specs=[
            pl.BlockSpec((tm, d), rows),
            pl.BlockSpec((1, d), full),
            pl.BlockSpec((d, n_main), full),
            pl.BlockSpec((d, LANES), full),
            pl.BlockSpec((SSD_HEADS, d), full),
        ],
        out_specs=[
            pl.BlockSpec((npair, tm, LANES), hp),
            pl.BlockSpec((npair, tm, LANES), hp),
            pl.BlockSpec((npair, tm, LANES), hp),
            pl.BlockSpec((tm, SB_WIDTH), rows),
            pl.BlockSpec((tm, SB_WIDTH), rows),
            pl.BlockSpec((tm, SSD_WIDTH), rows),
            pl.BlockSpec((tm, SSD_CONV_CH), rows),
            pl.BlockSpec((tm, SSD_HEADS), rows),
            pl.BlockSpec((SSD_HEADS, tm), lambda i: (0, i)),
        ],
        out_shape=[
            jax.ShapeDtypeStruct((npair, m, LANES), BF16),
            jax.ShapeDtypeStruct((npair, m, LANES), BF16),
            jax.ShapeDtypeStruct((npair, m, LANES), BF16),
            jax.ShapeDtypeStruct((m, SB_WIDTH), F32),
            jax.ShapeDtypeStruct((m, SB_WIDTH), F32),
            jax.ShapeDtypeStruct((m, SSD_WIDTH), F32),
            jax.ShapeDtypeStruct((m, SSD_CONV_CH), F32),
            jax.ShapeDtypeStruct((m, SSD_HEADS), F32),
            jax.ShapeDtypeStruct((SSD_HEADS, m), F32),
        ],
        compiler_params=_params(("parallel",)),
        name="even_proj",
    )(x, norm_w.reshape(1, d), w_main, w_dt, w_dtt)


def _sb_tri(n):
    j = jnp.arange(n)[:, None]
    s = jnp.arange(n)[None, :]
    return jnp.concatenate([(j > s).astype(BF16), jnp.ones((n, LANES), BF16)], axis=1)


def _sb_block(qh, kb, vb, tri, bias, carry, valid):
    t = kb.shape[0]
    z = _dot_nt(qh, kb) + bias
    sp = _softplus(z)
    l1m = -sp if valid is None else jnp.where(valid, -sp, 0.0)
    hi = l1m.astype(BF16)
    lo = (l1m - hi.astype(F32)).astype(BF16)
    r = _dot(hi, tri) + _dot(lo, tri)
    rem = r[:, 0:t] + jnp.concatenate([carry] * (t // LANES), axis=1)
    w = jnp.exp(z - sp + rem)
    if valid is not None:
        w = jnp.where(valid, w, 0.0)
    return _dot(w.astype(BF16), vb), r[:, t:t + LANES]


def _sb_prompt_kernel(bias_ref, q_ref, k_ref, v_ref, tri_ref, o_ref, c_scr, a_scr):
    p = pl.program_id(1)
    i = pl.program_id(2)
    tq = q_ref.shape[1]
    q = q_ref[0]
    lane = lax.broadcasted_iota(jnp.int32, (tq, LANES), 1)
    first = lane < SB_HEAD_DIM
    qs = (jnp.where(first, q, jnp.zeros_like(q)), jnp.where(first, jnp.zeros_like(q), q))
    biases = (bias_ref[2 * p], bias_ref[2 * p + 1])
    tri = tri_ref[...]
    c_scr[...] = jnp.zeros_like(c_scr)
    a_scr[...] = jnp.zeros_like(a_scr)

    def step(j, valid):
        kb = k_ref[0, pl.ds(pl.multiple_of(j * tq, tq), tq), :]
        vb = v_ref[0, pl.ds(pl.multiple_of(j * tq, tq), tq), :]
        for h in range(2):
            pv, tot = _sb_block(qs[h], kb, vb, tri, biases[h], c_scr[h], valid)
            a_scr[h] += pv
            c_scr[h] += tot

    row = lax.broadcasted_iota(jnp.int32, (tq, tq), 0)
    col = lax.broadcasted_iota(jnp.int32, (tq, tq), 1)
    step(i, col < row)

    def body(n, _):
        step(i - 1 - n, None)
        return 0

    lax.fori_loop(0, i, body, 0)
    o_ref[0] = jnp.where(first, a_scr[0], a_scr[1])


def _sb_prompt(q, k, v, sb_bias, bsz, t):
    npair, m, _ = q.shape
    tq = SB_TILE
    nq = t // tq
    return pl.pallas_call(
        _sb_prompt_kernel,
        grid_spec=pltpu.PrefetchScalarGridSpec(
            num_scalar_prefetch=1,
            grid=(bsz, npair, nq),
            in_specs=[
                pl.BlockSpec((1, tq, LANES), lambda b, p, i, s: (p, b * nq + i, 0)),
                pl.BlockSpec((1, t, LANES), lambda b, p, i, s: (p, b, 0)),
                pl.BlockSpec((1, t, LANES), lambda b, p, i, s: (p, b, 0)),
                pl.BlockSpec((tq, tq + LANES), lambda b, p, i, s: (0, 0)),
            ],
            out_specs=pl.BlockSpec((1, tq, LANES), lambda b, p, i, s: (p, b * nq + i, 0)),
            scratch_shapes=[pltpu.VMEM((2, tq, LANES), F32), pltpu.VMEM((2, tq, LANES), F32)],
        ),
        out_shape=jax.ShapeDtypeStruct((npair, m, LANES), F32),
        compiler_params=_params(("parallel", "parallel", "arbitrary")),
        name="sb_prompt",
    )(sb_bias.astype(F32), q, k, v, _sb_tri(tq))


def _sb_decode_kernel(pt_ref, q_ref, bias_ref, k_ref, v_ref, tri_ref, o_ref, c_scr, a_scr):
    s = pl.program_id(0)
    g = pl.program_id(1)
    npages = pl.num_programs(1)
    width = q_ref.shape[2]

    @pl.when(g == 0)
    def _():
        c_scr[...] = jnp.zeros_like(c_scr)
        a_scr[...] = jnp.zeros_like(a_scr)

    head = lax.broadcasted_iota(jnp.int32, (SB_HEADS, width), 0)
    lane = lax.broadcasted_iota(jnp.int32, (SB_HEADS, width), 1)
    own = (lane // SB_HEAD_DIM) == head
    qm = jnp.where(own, jnp.broadcast_to(q_ref[0], (SB_HEADS, width)), 0.0).astype(BF16)
    kb = k_ref[0].astype(BF16)
    vb = v_ref[0].astype(BF16)
    z = _dot_nt(qm, kb) + bias_ref[...]
    sp = _softplus(z)
    l1m = -sp
    hi = l1m.astype(BF16)
    lo = (l1m - hi.astype(F32)).astype(BF16)
    tri = tri_ref[...]
    rem = _dot(hi, tri) + _dot(lo, tri) + c_scr[...]
    w = jnp.exp(z - sp + rem)
    a_scr[...] += _dot(w.astype(BF16), vb)
    c_scr[...] += jnp.sum(l1m, axis=1, keepdims=True)

    @pl.when(g == npages - 1)
    def _():
        o = jnp.sum(jnp.where(own, a_scr[...], 0.0), axis=0, keepdims=True)
        for p in range(width // LANES):
            o_ref[p, pl.ds(s, 1), :] = o[:, p * LANES:(p + 1) * LANES]


def _sb_decode(q, cache_k, cache_v, page_table, sb_bias):
    n_seq, _, width = q.shape
    page = cache_k.shape[1]
    npages = page_table.shape[1]
    j = jnp.arange(page)[:, None]
    s = jnp.arange(page)[None, :]
    tri = (j > s).astype(BF16)
    kv_map = lambda b, g, pt: (pt[b, npages - 1 - g], 0, 0)
    return pl.pallas_call(
        _sb_decode_kernel,
        grid_spec=pltpu.PrefetchScalarGridSpec(
            num_scalar_prefetch=1,
            grid=(n_seq, npages),
            in_specs=[
                pl.BlockSpec((1, 1, width), lambda b, g, pt: (b, 0, 0)),
                pl.BlockSpec((SB_HEADS, 1), lambda b, g, pt: (0, 0)),
                pl.BlockSpec((1, page, width), kv_map),
                pl.BlockSpec((1, page, width), kv_map),
                pl.BlockSpec((page, page), lambda b, g, pt: (0, 0)),
            ],
            out_specs=pl.BlockSpec((width // LANES, n_seq, LANES), lambda b, g, pt: (0, 0, 0)),
            scratch_shapes=[pltpu.VMEM((SB_HEADS, 1), F32), pltpu.VMEM((SB_HEADS, width), F32)],
        ),
        out_shape=jax.ShapeDtypeStruct((width // LANES, n_seq, LANES), F32),
        compiler_params=_params(("arbitrary", "arbitrary")),
        name="sb_decode",
    )(page_table, q, sb_bias.astype(F32).reshape(SB_HEADS, 1), cache_k, cache_v, tri)


def _ssd_prompt_kernel(xbc_ref, z_ref, dt_ref, dtt_ref, cw_ref, cb_ref, dtb_ref, dtbt_ref, alog_ref, alogt_ref,
                       dskip_ref, nw_ref, y_ref, conv_ref, ssm_ref, buf, st):
    c = pl.program_id(1)
    nc = pl.num_programs(1)
    L = xbc_ref.shape[0]
    pad = SUBLANES

    @pl.when(c == 0)
    def _():
        buf[0:pad, :] = jnp.zeros((pad, SSD_CONV_CH), F32)
        st[...] = jnp.zeros_like(st)

    buf[pad:pad + L, :] = xbc_ref[...]
    conv = cb_ref[...]
    for j in range(SSD_CONV):
        off = pad - (SSD_CONV - 1) + j
        conv = conv + cw_ref[j:j + 1, :] * buf[off:off + L, :]
    tail = buf[pad + L - (SSD_CONV - 1):pad + L, :]
    buf[pad - (SSD_CONV - 1):pad, :] = tail
    xa = _silu(conv)
    xs = xa[:, 0:SSD_WIDTH]
    gw = SSD_STATE
    bm = [xa[:, SSD_WIDTH + g * gw:SSD_WIDTH + (g + 1) * gw].astype(BF16) for g in range(SSD_GROUPS)]
    cm = [xa[:, SSD_WIDTH + (SSD_GROUPS + g) * gw:SSD_WIDTH + (SSD_GROUPS + g + 1) * gw].astype(BF16)
          for g in range(SSD_GROUPS)]

    dt = _softplus(dt_ref[...] + dtb_ref[...])
    dtt = _softplus(dtt_ref[...] + dtbt_ref[...])
    a = -jnp.exp(alog_ref[...])
    at = -jnp.exp(alogt_ref[...])
    row = lax.broadcasted_iota(jnp.int32, (L, L), 0)
    col = lax.broadcasted_iota(jnp.int32, (L, L), 1)
    lower = col <= row
    lower_b = lower.astype(BF16)
    upper_b = (row <= col).astype(BF16)
    cum = sum(_dot(lower_b, part) for part in _split3(dt * a))
    cumt = sum(_dot(part, upper_b) for part in _split3(dtt * at))
    last = cum[L - 1:L, :]
    wst = jnp.exp(last - cum) * dt
    ecum = jnp.exp(cum)
    elast = jnp.exp(last)

    lane = lax.broadcasted_iota(jnp.int32, (L, LANES), 1)
    first = lane < SSD_HEAD_DIM
    first_row = first[0:1, :]
    heads_per_group = SSD_HEADS // SSD_GROUPS
    cb = [_dot_nt(cm[g], bm[g]) for g in range(SSD_GROUPS)]
    ys = []
    for p in range(SSD_WIDTH // LANES):
        g = (2 * p) // heads_per_group
        xp = xs[:, p * LANES:(p + 1) * LANES]
        xpb = xp.astype(BF16)
        yi = []
        for h in (2 * p, 2 * p + 1):
            seg = cum[:, h:h + 1] - cumt[h:h + 1, :]
            dec = jnp.where(lower, jnp.exp(jnp.where(lower, seg, 0.0)), 0.0) * dtt[h:h + 1, :]
            yi.append(_dot((cb[g] * dec).astype(BF16), xpb))
        y_intra = jnp.where(first, yi[0], yi[1])
        stp = st[:, p * LANES:(p + 1) * LANES]
        ec = jnp.where(first, ecum[:, 2 * p:2 * p + 1], ecum[:, 2 * p + 1:2 * p + 2])
        y_inter = _dot(cm[g], stp.astype(BF16)) * ec
        wp = jnp.where(first, wst[:, 2 * p:2 * p + 1], wst[:, 2 * p + 1:2 * p + 2])
        el = jnp.where(first_row, elast[:, 2 * p:2 * p + 1], elast[:, 2 * p + 1:2 * p + 2])
        st[:, p * LANES:(p + 1) * LANES] = el * stp + _dot_tn(bm[g], (xp * wp).astype(BF16))
        ys.append(y_intra + y_inter + dskip_ref[:, p * LANES:(p + 1) * LANES] * xp)
    y = jnp.concatenate(ys, axis=1)
    y_ref[...] = _rmsnorm(y * _silu(z_ref[...]), nw_ref[...])

    @pl.when(c == nc - 1)
    def _():
        conv_ref[0] = tail
        ssm_ref[0] = st[...].T


def _ssd_prompt(xbc, z, dt, dtt, conv_w, conv_b, dt_bias, a_log, d_skip, ssd_norm, bsz, t):
    m = xbc.shape[0]
    L = SSD_CHUNK
    nc = t // L
    rows = lambda b, c: (b * nc + c, 0)
    full = lambda b, c: (0, 0)
    vec = lambda v: v.astype(F32).reshape(1, -1)
    colv = lambda v: v.astype(F32).reshape(-1, 1)
    y, conv, ssm = pl.pallas_call(
        _ssd_prompt_kernel,
        grid=(bsz, nc),
        in_specs=[
            pl.BlockSpec((L, SSD_CONV_CH), rows),
            pl.BlockSpec((L, SSD_WIDTH), rows),
            pl.BlockSpec((L, SSD_HEADS), rows),
            pl.BlockSpec((SSD_HEADS, L), lambda b, c: (0, b * nc + c)),
            pl.BlockSpec((SSD_CONV, SSD_CONV_CH), full),
            pl.BlockSpec((1, SSD_CONV_CH), full),
            pl.BlockSpec((1, SSD_HEADS), full),
            pl.BlockSpec((SSD_HEADS, 1), full),
            pl.BlockSpec((1, SSD_HEADS), full),
            pl.BlockSpec((SSD_HEADS, 1), full),
            pl.BlockSpec((1, SSD_WIDTH), full),
            pl.BlockSpec((1, SSD_WIDTH), full),
        ],
        out_specs=[
            pl.BlockSpec((L, SSD_WIDTH), rows),
            pl.BlockSpec((1, SSD_CONV - 1, SSD_CONV_CH), lambda b, c: (b, 0, 0)),
            pl.BlockSpec((1, SSD_WIDTH, SSD_STATE), lambda b, c: (b, 0, 0)),
        ],
        out_shape=[
            jax.ShapeDtypeStruct((m, SSD_WIDTH), F32),
            jax.ShapeDtypeStruct((bsz, SSD_CONV - 1, SSD_CONV_CH), F32),
            jax.ShapeDtypeStruct((bsz, SSD_WIDTH, SSD_STATE), F32),
        ],
        scratch_shapes=[pltpu.VMEM((L + SUBLANES, SSD_CONV_CH), F32), pltpu.VMEM((SSD_STATE, SSD_WIDTH), F32)],
        compiler_params=_params(("parallel", "arbitrary")),
        name="ssd_prompt",
    )(xbc, z, dt, dtt, conv_w.astype(F32), vec(conv_b), vec(dt_bias), colv(dt_bias), vec(a_log), colv(a_log),
      vec(jnp.repeat(d_skip, SSD_HEAD_DIM)), vec(ssd_norm))
    return y, conv, ssm.reshape(bsz, SSD_HEADS, SSD_HEAD_DIM, SSD_STATE)


def _ssd_step_kernel(xbc_ref, z_ref, dt_ref, sconv_ref, sssm_ref, cw_ref, cb_ref, dtb_ref, alog_ref, dskip_ref,
                     nw_ref, y_ref, conv_ref, ssm_ref):
    xr = xbc_ref[0]
    cs = sconv_ref[0]
    conv = cb_ref[...] + cw_ref[SSD_CONV - 1:SSD_CONV, :] * xr
    for j in range(SSD_CONV - 1):
        conv = conv + cw_ref[j:j + 1, :] * cs[j:j + 1, :]
    conv_ref[0] = jnp.concatenate([cs[1:SSD_CONV - 1, :], xr], axis=0)
    xa = _silu(conv)
    xs = xa[:, 0:SSD_WIDTH]
    dt = _softplus(dt_ref[0] + dtb_ref[...])
    da = jnp.exp(dt * -jnp.exp(alog_ref[...]))
    lane_head = lax.broadcasted_iota(jnp.int32, (1, SSD_WIDTH), 1) // SSD_HEAD_DIM
    dt_w = jnp.zeros((1, SSD_WIDTH), F32)
    da_w = jnp.zeros((1, SSD_WIDTH), F32)
    for h in range(SSD_HEADS):
        dt_w = jnp.where(lane_head == h, dt[:, h:h + 1], dt_w)
        da_w = jnp.where(lane_head == h, da[:, h:h + 1], da_w)
    dtx_col = _row_to_col(dt_w * xs)
    da_col = _row_to_col(da_w)
    rows_per_group = SSD_WIDTH // SSD_GROUPS
    row = lax.broadcasted_iota(jnp.int32, (SSD_WIDTH, SSD_STATE), 0)
    b_rows = jnp.zeros((SSD_WIDTH, SSD_STATE), F32)
    c_rows = jnp.zeros((SSD_WIDTH, SSD_STATE), F32)
    for g in range(SSD_GROUPS):
        sel = (row // rows_per_group) == g
        b0 = SSD_WIDTH + g * SSD_STATE
        c0 = SSD_WIDTH + (SSD_GROUPS + g) * SSD_STATE
        b_rows = jnp.where(sel, xa[:, b0:b0 + SSD_STATE], b_rows)
        c_rows = jnp.where(sel, xa[:, c0:c0 + SSD_STATE], c_rows)
    new = da_col * sssm_ref[0] + dtx_col * b_rows
    ssm_ref[0] = new
    y = _col_to_row(jnp.sum(new * c_rows, axis=1, keepdims=True)) + dskip_ref[...] * xs
    y_ref[0] = _rmsnorm(y * _silu(z_ref[0]), nw_ref[...])


def _ssd_step(xbc, z, dt, state_conv, state_ssm, conv_w, conv_b, dt_bias, a_log, d_skip, ssd_norm):
    n = xbc.shape[0]
    vec = lambda v: v.astype(F32).reshape(1, -1)
    per = lambda i: (i, 0, 0)
    full = lambda i: (0, 0)
    y, conv, ssm = pl.pallas_call(
        _ssd_step_kernel,
        grid=(n,),
        in_specs=[
            pl.BlockSpec((1, 1, SSD_CONV_CH), per),
            pl.BlockSpec((1, 1, SSD_WIDTH), per),
            pl.BlockSpec((1, 1, SSD_HEADS), per),
            pl.BlockSpec((1, SSD_CONV - 1, SSD_CONV_CH), per),
            pl.BlockSpec((1, SSD_WIDTH, SSD_STATE), per),
            pl.BlockSpec((SSD_CONV, SSD_CONV_CH), full),
            pl.BlockSpec((1, SSD_CONV_CH), full),
            pl.BlockSpec((1, SSD_HEADS), full),
            pl.BlockSpec((1, SSD_HEADS), full),
            pl.BlockSpec((1, SSD_WIDTH), full),
            pl.BlockSpec((1, SSD_WIDTH), full),
        ],
        out_specs=[
            pl.BlockSpec((1, 1, SSD_WIDTH), per),
            pl.BlockSpec((1, SSD_CONV - 1, SSD_CONV_CH), per),
            pl.BlockSpec((1, SSD_WIDTH, SSD_STATE), per),
        ],
        out_shape=[
            jax.ShapeDtypeStruct((n, 1, SSD_WIDTH), F32),
            jax.ShapeDtypeStruct((n, SSD_CONV - 1, SSD_CONV_CH), F32),
            jax.ShapeDtypeStruct((n, SSD_WIDTH, SSD_STATE), F32),
        ],
        compiler_params=_params(("parallel",)),
        name="ssd_step",
    )(xbc.reshape(n, 1, SSD_CONV_CH), z.reshape(n, 1, SSD_WIDTH), dt.reshape(n, 1, SSD_HEADS),
      state_conv.astype(F32), state_ssm.astype(F32).reshape(n, SSD_WIDTH, SSD_STATE), conv_w.astype(F32),
      vec(conv_b), vec(dt_bias), vec(a_log), vec(jnp.repeat(d_skip, SSD_HEAD_DIM)), vec(ssd_norm))
    return y.reshape(n, SSD_WIDTH), conv, ssm.reshape(n, SSD_HEADS, SSD_HEAD_DIM, SSD_STATE)


def _even_out_kernel(osb_ref, y_ref, x_ref, w_ref, o_ref):
    mix = jnp.concatenate([osb_ref[p] for p in range(osb_ref.shape[0])] + [y_ref[...]], axis=1)
    o_ref[...] = x_ref[...] + _dot(mix.astype(BF16), w_ref[...])


def _even_out(osb, y, x, w_out, tm):
    m, d = x.shape
    npair = osb.shape[0]
    rows = lambda i: (i, 0)
    return pl.pallas_call(
        _even_out_kernel,
        grid=(m // tm,),
        in_specs=[
            pl.BlockSpec((npair, tm, LANES), lambda i: (0, i, 0)),
            pl.BlockSpec((tm, SSD_WIDTH), rows),
            pl.BlockSpec((tm, d), rows),
            pl.BlockSpec(w_out.shape, lambda i: (0, 0)),
        ],
        out_specs=pl.BlockSpec((tm, d), rows),
        out_shape=jax.ShapeDtypeStruct((m, d), F32),
        compiler_params=_params(("parallel",)),
        name="even_out",
    )(osb, y, x, w_out.astype(BF16))


def _odd_out_kernel(o_ref, g_ref, x_ref, nw_ref, w_ref, out_ref):
    o = jnp.concatenate([o_ref[h] for h in range(o_ref.shape[0])], axis=1)
    gated = _rmsnorm(o, nw_ref[...]) * _silu(g_ref[...])
    out_ref[...] = x_ref[...] + _dot(gated.astype(BF16), w_ref[...])


def _odd_out(o, g, x, hg_norm, w_out, tm):
    m, d = x.shape
    nh = o.shape[0]
    rows = lambda i: (i, 0)
    return pl.pallas_call(
        _odd_out_kernel,
        grid=(m // tm,),
        in_specs=[
            pl.BlockSpec((nh, tm, HG_VAL), lambda i: (0, i, 0)),
            pl.BlockSpec((tm, nh * HG_VAL), rows),
            pl.BlockSpec((tm, d), rows),
            pl.BlockSpec((1, nh * HG_VAL), lambda i: (0, 0)),
            pl.BlockSpec(w_out.shape, lambda i: (0, 0)),
        ],
        out_specs=pl.BlockSpec((tm, d), rows),
        out_shape=jax.ShapeDtypeStruct((m, d), F32),
        compiler_params=_params(("parallel",)),
        name="odd_out",
    )(o, g, x, hg_norm.astype(F32).reshape(1, -1), w_out.astype(BF16))


def _mlp_kernel(x_ref, nw_ref, w1_ref, w2_ref, fw_ref, o_ref, hn_scr, acc_scr, *, final_norm):
    j = pl.program_id(1)

    @pl.when(j == 0)
    def _():
        hn_scr[...] = _rmsnorm(x_ref[...], nw_ref[...]).astype(BF16)
        acc_scr[...] = jnp.zeros_like(acc_scr)

    h = jnp.maximum(_dot(hn_scr[...], w1_ref[...]), 0.0)
    acc_scr[...] += _dot((h * h).astype(BF16), w2_ref[...])

    @pl.when(j == pl.num_programs(1) - 1)
    def _():
        out = x_ref[...] + acc_scr[...]
        o_ref[...] = _rmsnorm(out, fw_ref[...]) if final_norm else out


def _mlp(x, norm_w, w1, w2, final_w, tm, tf):
    m, d = x.shape
    f = w1.shape[1]
    final_norm = final_w is not None
    fw = (final_w if final_norm else jnp.ones((d,), F32)).astype(F32).reshape(1, d)
    return pl.pallas_call(
        functools.partial(_mlp_kernel, final_norm=final_norm),
        grid=(m // tm, f // tf),
        in_specs=[
            pl.BlockSpec((tm, d), lambda i, j: (i, 0)),
            pl.BlockSpec((1, d), lambda i, j: (0, 0)),
            pl.BlockSpec((d, tf), lambda i, j: (0, j)),
            pl.BlockSpec((tf, d), lambda i, j: (j, 0)),
            pl.BlockSpec((1, d), lambda i, j: (0, 0)),
        ],
        out_specs=pl.BlockSpec((tm, d), lambda i, j: (i, 0)),
        out_shape=jax.ShapeDtypeStruct((m, d), F32),
        scratch_shapes=[pltpu.VMEM((tm, d), BF16), pltpu.VMEM((tm, d), F32)],
        compiler_params=_params(("parallel", "arbitrary")),
        name="mlp",
    )(x, norm_w.astype(F32).reshape(1, d), w1.astype(BF16), w2.astype(BF16), fw)


def _odd_proj_kernel(x_ref, nw_ref, w_ref, q_ref, f_ref, v_ref, g_ref):
    hn = _rmsnorm(x_ref[...], nw_ref[...]).astype(BF16)
    width = HG_HEADS * HG_KEY
    per_chunk = PROJ_COLS // HG_KEY
    for i, ref in enumerate((q_ref, f_ref, v_ref)):
        for c in range(width // PROJ_COLS):
            y = _dot(hn, w_ref[:, i * width + c * PROJ_COLS:i * width + (c + 1) * PROJ_COLS])
            for h in range(per_chunk):
                ref[c * per_chunk + h] = y[:, h * HG_KEY:(h + 1) * HG_KEY]
    for c in range(width // PROJ_COLS):
        g_ref[:, c * PROJ_COLS:(c + 1) * PROJ_COLS] = _dot(
            hn, w_ref[:, 3 * width + c * PROJ_COLS:3 * width + (c + 1) * PROJ_COLS])


def _odd_proj(x, norm_w, w_in, tm):
    m, d = x.shape
    width = HG_HEADS * HG_KEY
    heads = lambda i: (0, i, 0)
    per_head = jax.ShapeDtypeStruct((HG_HEADS, m, HG_KEY), F32)
    return pl.pallas_call(
        _odd_proj_kernel,
        grid=(m // tm,),
        in_specs=[
            pl.BlockSpec((tm, d), lambda i: (i, 0)),
            pl.BlockSpec((1, d), lambda i: (0, 0)),
            pl.BlockSpec(w_in.shape, lambda i: (0, 0)),
        ],
        out_specs=[
            pl.BlockSpec((HG_HEADS, tm, HG_KEY), heads),
            pl.BlockSpec((HG_HEADS, tm, HG_KEY), heads),
            pl.BlockSpec((HG_HEADS, tm, HG_KEY), heads),
            pl.BlockSpec((tm, width), lambda i: (i, 0)),
        ],
        out_shape=[per_head, per_head, per_head, jax.ShapeDtypeStruct((m, width), F32)],
        compiler_params=_params(("parallel",)),
        name="odd_proj",
    )(x, norm_w.astype(F32).reshape(1, d), w_in.astype(BF16))


def _hgrn_lower_bound(raw, layer):
    e = jnp.exp(raw - jnp.max(raw, axis=0, keepdims=True))
    p = e / jnp.sum(e, axis=0, keepdims=True)
    lb = jnp.zeros_like(p[0])
    for l in range(1, layer + 1):
        lb = lb + p[l]
    return lb


def _hgrn_gates(fpre, lb):
    sig, nsig = _sigmoid_pair(fpre)
    return lb + (1.0 - lb) * sig, (1.0 - lb) * nsig


def _hgrn_prompt_kernel(q_ref, f_ref, v_ref, lb_ref, o_ref, s_ref, st, b_scr, k_scr, od_scr, *, layer):
    c = pl.program_id(1)
    nc = pl.num_programs(1)
    L = q_ref.shape[1]
    nblk = L // HG_DIAG

    @pl.when(c == 0)
    def _():
        st[...] = jnp.zeros_like(st)

    row = lax.broadcasted_iota(jnp.int32, (L, L), 0)
    col = lax.broadcasted_iota(jnp.int32, (L, L), 1)
    lower_b = (col <= row).astype(BF16)
    pos = lax.broadcasted_iota(jnp.int32, (L, HG_KEY), 0)

    def head(h, _):
        q = q_ref[h]
        v = v_ref[h]
        fg, kin = _hgrn_gates(f_ref[h], _hgrn_lower_bound(lb_ref[:, h], layer))
        b = sum(_dot(lower_b, part) for part in _split3(jnp.log(fg)))
        b_scr[...] = b
        k_scr[...] = kin
        vb = v.astype(BF16)
        stt = st[h]
        o = _dot_nt((q * jnp.exp(b)).astype(BF16), stt.astype(BF16))
        b_last = b[L - 1:L, :]
        st[h] = stt * jnp.exp(b_last) + _dot_tn(vb, (kin * jnp.exp(b_last - b)).astype(BF16))

        att = jnp.zeros((L, L), F32)
        g = L
        while g > HG_DIAG:
            half = g // 2
            b3 = b.reshape(L // g, g, HG_KEY)
            mid = jnp.broadcast_to(b3[:, half - 1:half, :], b3.shape).reshape(L, HG_KEY)
            late = (pos & (g - 1)) >= half
            qd = jnp.where(late, q * jnp.exp(jnp.minimum(b - mid, 0.0)), 0.0)
            kd = jnp.where(late, 0.0, kin * jnp.exp(jnp.minimum(mid - b, 0.0)))
            same = (row ^ col) < g
            att = att + jnp.where(same, _dot_nt(qd.astype(BF16), kd.astype(BF16)), 0.0)
            g = half
        o = o + _dot(att.astype(BF16), vb)

        for i in range(HG_DIAG):
            qi = q_ref[h, pl.ds(i, nblk, stride=HG_DIAG), :]
            bi = b_scr[pl.ds(i, nblk, stride=HG_DIAG), :]
            acc = jnp.zeros((nblk, HG_VAL), F32)
            for j in range(i + 1):
                kj = k_scr[pl.ds(j, nblk, stride=HG_DIAG), :]
                bj = b_scr[pl.ds(j, nblk, stride=HG_DIAG), :]
                vj = v_ref[h, pl.ds(j, nblk, stride=HG_DIAG), :]
                a = jnp.sum(qi * kj * jnp.exp(bi - bj), axis=1, keepdims=True)
                acc = acc + a * vj
            od_scr[pl.ds(i, nblk, stride=HG_DIAG), :] = acc
        o_ref[h] = o + od_scr[...]
        return 0

    lax.fori_loop(0, q_ref.shape[0], head, 0)

    @pl.when(c == nc - 1)
    def _():
        for h in range(s_ref.shape[1]):
            s_ref[0, h] = st[h].T


def _hgrn_prompt(q, f, v, lb_raw, layer, bsz, t):
    nh, m, _ = q.shape
    L = HG_CHUNK
    nc = t // L
    blk = lambda b, c: (0, b * nc + c, 0)
    depth = lb_raw.shape[0]
    return pl.pallas_call(
        functools.partial(_hgrn_prompt_kernel, layer=layer),
        grid=(bsz, nc),
        in_specs=[
            pl.BlockSpec((nh, L, HG_KEY), blk),
            pl.BlockSpec((nh, L, HG_KEY), blk),
            pl.BlockSpec((nh, L, HG_VAL), blk),
            pl.BlockSpec((depth, nh, 1, HG_KEY), lambda b, c: (0, 0, 0, 0)),
        ],
        out_specs=[
            pl.BlockSpec((nh, L, HG_VAL), blk),
            pl.BlockSpec((1, nh, HG_KEY, HG_VAL), lambda b, c: (b, 0, 0, 0)),
        ],
        out_shape=[
            jax.ShapeDtypeStruct((nh, m, HG_VAL), F32),
            jax.ShapeDtypeStruct((bsz, nh, HG_KEY, HG_VAL), F32),
        ],
        scratch_shapes=[
            pltpu.VMEM((nh, HG_VAL, HG_KEY), F32),
            pltpu.VMEM((L, HG_KEY), F32),
            pltpu.VMEM((L, HG_KEY), F32),
            pltpu.VMEM((L, HG_VAL), F32),
        ],
        compiler_params=_params(("parallel", "arbitrary")),
        name="hgrn_prompt",
    )(q, f, v, lb_raw.astype(F32).reshape(depth, nh, 1, HG_KEY))


def _hgrn_step_kernel(q_ref, f_ref, v_ref, lb_ref, s_ref, o_ref, snew_ref, *, layer):
    i = pl.program_id(0)
    for h in range(q_ref.shape[0]):
        q = q_ref[h, pl.ds(i, 1), :]
        v = v_ref[h, pl.ds(i, 1), :]
        fg, kin = _hgrn_gates(f_ref[h, pl.ds(i, 1), :], _hgrn_lower_bound(lb_ref[:, h], layer))
        new = _row_to_col(fg) * s_ref[0, h] + _row_to_col(kin) * v
        snew_ref[0, h] = new
        o_ref[h, pl.ds(i, 1), :] = jnp.sum(_row_to_col(q) * new, axis=0, keepdims=True)


def _hgrn_step(q, f, v, lb_raw, state, layer):
    nh, n, _ = q.shape
    depth = lb_raw.shape[0]
    whole = lambda i: (0, 0, 0)
    return pl.pallas_call(
        functools.partial(_hgrn_step_kernel, layer=layer),
        grid=(n,),
        in_specs=[
            pl.BlockSpec((nh, n, HG_KEY), whole),
            pl.BlockSpec((nh, n, HG_KEY), whole),
            pl.BlockSpec((nh, n, HG_VAL), whole),
            pl.BlockSpec((depth, nh, 1, HG_KEY), lambda i: (0, 0, 0, 0)),
            pl.BlockSpec((1, nh, HG_KEY, HG_VAL), lambda i: (i, 0, 0, 0)),
        ],
        out_specs=[
            pl.BlockSpec((nh, n, HG_VAL), whole),
            pl.BlockSpec((1, nh, HG_KEY, HG_VAL), lambda i: (i, 0, 0, 0)),
        ],
        out_shape=[
            jax.ShapeDtypeStruct((nh, n, HG_VAL), F32),
            jax.ShapeDtypeStruct((n, nh, HG_KEY, HG_VAL), F32),
        ],
        compiler_params=_params(("arbitrary",)),
        name="hgrn_step",
    )(q, f, v, lb_raw.astype(F32).reshape(depth, nh, 1, HG_KEY), state.astype(F32))


def _row_tile(m, want):
    return want if m % want == 0 else m


def kernel(x_prompt, x_sample, cache_k, cache_v, page_table, state_conv, state_ssm, state_hgrn, norm_mix, norm_ffn,
           norm_final, w_in_even, sb_bias, conv_w, conv_b, dt_bias, a_log, d_skip, ssd_norm, w_out_even, w_in_odd,
           hg_lb_raw, hg_norm, w_out_odd, w_ff1, w_ff2):
    bsz, t, d = x_prompt.shape
    n_seq = x_sample.shape[0]
    depth = norm_mix.shape[0]
    mp = bsz * t
    hp = x_prompt.reshape(mp, d)
    hs = x_sample.reshape(n_seq, d)
    tmp = _row_tile(mp, 512)
    tmm = _row_tile(mp, 1024)
    tf = 512
    npair = SB_WIDTH // LANES
    outs = {k: [] for k in ("kp", "vp", "ks", "vs", "cp", "cs", "sp", "ss", "gp", "gs")}

    for layer in range(depth):
        li = layer // 2
        if layer % 2 == 0:
            ssd_w = (conv_w[li], conv_b[li], dt_bias[li], a_log[li], d_skip[li], ssd_norm[li])
            q, k, v, kf, vf, z, xbc, dt, dtt = _even_proj(hp, norm_mix[layer], w_in_even[li], tmp)
            osb = _sb_prompt(q, k, v, sb_bias[li], bsz, t)
            y, cp, sp = _ssd_prompt(xbc, z, dt, dtt, *ssd_w, bsz, t)
            hp = _even_out(osb, y, hp, w_out_even[li], tmp)
            outs["kp"].append(kf.reshape(bsz, t, SB_HEADS, SB_HEAD_DIM))
            outs["vp"].append(vf.reshape(bsz, t, SB_HEADS, SB_HEAD_DIM))
            outs["cp"].append(cp)
            outs["sp"].append(sp)
            q, k, v, kf, vf, z, xbc, dt, dtt = _even_proj(hs, norm_mix[layer], w_in_even[li], n_seq)
            qs = q.astype(F32).transpose(1, 0, 2).reshape(n_seq, 1, SB_WIDTH)
            n_phys, page = cache_k.shape[1], cache_k.shape[2]
            osb = _sb_decode(qs, cache_k[li].reshape(n_phys, page, SB_WIDTH),
                             cache_v[li].reshape(n_phys, page, SB_WIDTH), page_table, sb_bias[li])
            y, cs, ss = _ssd_step(xbc, z, dt, state_conv[li], state_ssm[li], *ssd_w)
            hs = _even_out(osb, y, hs, w_out_even[li], n_seq)
            outs["ks"].append(kf.reshape(n_seq, 1, SB_HEADS, SB_HEAD_DIM))
            outs["vs"].append(vf.reshape(n_seq, 1, SB_HEADS, SB_HEAD_DIM))
            outs["cs"].append(cs)
            outs["ss"].append(ss)
        else:
            q, f, v, g = _odd_proj(hp, norm_mix[layer], w_in_odd[li], tmp)
            o, gp = _hgrn_prompt(q, f, v, hg_lb_raw, layer, bsz, t)
            hp = _odd_out(o, g, hp, hg_norm[li], w_out_odd[li], tmp)
            outs["gp"].append(gp)
            q, f, v, g = _odd_proj(hs, norm_mix[layer], w_in_odd[li], n_seq)
            o, gs = _hgrn_step(q, f, v, hg_lb_raw, state_hgrn[li], layer)
            hs = _odd_out(o, g, hs, hg_norm[li], w_out_odd[li], n_seq)
            outs["gs"].append(gs)
        fw = norm_final if layer == depth - 1 else None
        hp = _mlp(hp, norm_ffn[layer], w_ff1[layer], w_ff2[layer], fw, tmm, tf)
        hs = _mlp(hs, norm_ffn[layer], w_ff1[layer], w_ff2[layer], fw, n_seq, tf)

    y_prompt = hp.reshape(bsz, t, d)
    y_sample = hs.reshape(n_seq, 1, d)
    st = lambda key: jnp.stack(outs[key])
    return (y_prompt, y_sample, st("kp"), st("vp"), st("ks"), st("vs"), st("cp"), st("cs"), st("sp"), st("ss"),
            st("gp"), st("gs"))
```

```python
import functools

import jax
import jax.numpy as jnp
from jax import lax
from jax.experimental import pallas as pl
from jax.experimental.pallas import tpu as pltpu

F32 = jnp.float32
BF16 = jnp.bfloat16

EPS = 1e-6
SB_HEADS = 8
SB_HEAD_DIM = 64
SB_WIDTH = SB_HEADS * SB_HEAD_DIM
SSD_HEADS = 8
SSD_HEAD_DIM = 64
SSD_WIDTH = SSD_HEADS * SSD_HEAD_DIM
SSD_STATE = 128
SSD_GROUPS = 2
SSD_CONV = 4
SSD_CONV_CH = SSD_WIDTH + 2 * SSD_GROUPS * SSD_STATE
HG_HEADS = 8
HG_KEY = 128
HG_VAL = 128

LANES = 128
SUBLANES = 8
VMEM_LIMIT_BYTES = 52 * 1024 * 1024

SB_TILE = 256
SB_DECODE_GROUP = 8
SSD_CHUNK = 128
HG_CHUNK = 128
HG_DIAG = 16
PROJ_COLS = 512


def _params(sem):
    return pltpu.CompilerParams(dimension_semantics=sem, vmem_limit_bytes=VMEM_LIMIT_BYTES)


def _dot(a, b):
    return jnp.dot(a, b, preferred_element_type=F32)


def _dot_nt(a, b):
    return lax.dot_general(a, b, (((1,), (1,)), ((), ())), preferred_element_type=F32)


def _dot_tn(a, b):
    return lax.dot_general(a, b, (((0,), (0,)), ((), ())), preferred_element_type=F32)


def _split3(x):
    hi = x.astype(BF16)
    r = x - hi.astype(F32)
    mid = r.astype(BF16)
    lo = (r - mid.astype(F32)).astype(BF16)
    return hi, mid, lo


def _rmsnorm(x, w):
    return x * lax.rsqrt(jnp.mean(x * x, axis=-1, keepdims=True) + EPS) * w


def _softplus(x):
    return jnp.maximum(x, 0.0) + jnp.log1p(jnp.exp(-jnp.abs(x)))


def _softplus_tile(x):
    return jnp.maximum(x, 0.0) + jnp.log(1.0 + jnp.exp(-jnp.abs(x)))


def _sigmoid_pair(x):
    e = jnp.exp(-jnp.abs(x))
    r = 1.0 / (1.0 + e)
    big, small = r, e * r
    pos = x >= 0
    return jnp.where(pos, big, small), jnp.where(pos, small, big)


def _silu(x):
    return x * _sigmoid_pair(x)[0]


def _row_to_col(row):
    n = row.shape[1]
    eye = lax.broadcasted_iota(jnp.int32, (n, n), 0) == lax.broadcasted_iota(jnp.int32, (n, n), 1)
    return jnp.sum(jnp.where(eye, jnp.broadcast_to(row, (n, n)), 0.0), axis=1, keepdims=True)


def _col_to_row(col):
    n = col.shape[0]
    eye = lax.broadcasted_iota(jnp.int32, (n, n), 0) == lax.broadcasted_iota(jnp.int32, (n, n), 1)
    return jnp.sum(jnp.where(eye, jnp.broadcast_to(col, (n, n)), 0.0), axis=0, keepdims=True)


def _even_proj_kernel(x_ref, nw_ref, w_ref, wdt_ref, wdtt_ref,
                      q_ref, k_ref, v_ref, kf_ref, vf_ref, z_ref, xbc_ref, dt_ref, dtt_ref):
    hn = _rmsnorm(x_ref[...], nw_ref[...]).astype(BF16)
    scale = SB_HEAD_DIM ** -0.5
    npair = SB_WIDTH // LANES
    q = _dot(hn, w_ref[:, 0:SB_WIDTH]) * scale
    for p in range(npair):
        q_ref[p] = q[:, p * LANES:(p + 1) * LANES].astype(BF16)
    k = _dot(hn, w_ref[:, SB_WIDTH:2 * SB_WIDTH])
    kf_ref[...] = k
    for p in range(npair):
        k_ref[p] = k[:, p * LANES:(p + 1) * LANES].astype(BF16)
    v = _dot(hn, w_ref[:, 2 * SB_WIDTH:3 * SB_WIDTH])
    vf_ref[...] = v
    for p in range(npair):
        v_ref[p] = v[:, p * LANES:(p + 1) * LANES].astype(BF16)
    z0 = 3 * SB_WIDTH
    z_ref[...] = _dot(hn, w_ref[:, z0:z0 + SSD_WIDTH])
    x0 = z0 + SSD_WIDTH
    for c in range(SSD_CONV_CH // PROJ_COLS):
        xbc_ref[:, c * PROJ_COLS:(c + 1) * PROJ_COLS] = _dot(
            hn, w_ref[:, x0 + c * PROJ_COLS:x0 + (c + 1) * PROJ_COLS])
    dt_ref[...] = _dot(hn, wdt_ref[...])[:, 0:SSD_HEADS]
    dtt_ref[...] = _dot_nt(wdtt_ref[...], hn)


def _even_proj(x, norm_w, w_in, tm):
    m, d = x.shape
    npair = SB_WIDTH // LANES
    wb = w_in.astype(BF16)
    n_main = 3 * SB_WIDTH + SSD_WIDTH + SSD_CONV_CH
    w_main = wb[:, :n_main]
    w_dt = jnp.pad(wb[:, n_main:], ((0, 0), (0, LANES - SSD_HEADS)))
    w_dtt = wb[:, n_main:].T
    full = lambda i: (0, 0)
    rows = lambda i: (i, 0)
    hp = lambda i: (0, i, 0)
    return pl.pallas_call(
        _even_proj_kernel,
        grid=(m // tm,),
        in_specs=[
            pl.BlockSpec((tm, d), rows),
            pl.BlockSpec((1, d), full),
            pl.BlockSpec((d, n_main), full),
            pl.BlockSpec((d, LANES), full),
            pl.BlockSpec((SSD_HEADS, d), full),
        ],
        out_specs=[
            pl.BlockSpec((npair, tm, LANES), hp),
            pl.BlockSpec((npair, tm, LANES), hp),
            pl.BlockSpec((npair, tm, LANES), hp),
            pl.BlockSpec((tm, SB_WIDTH), rows),
            pl.BlockSpec((tm, SB_WIDTH), rows),
            pl.BlockSpec((tm, SSD_WIDTH), rows),
            pl.BlockSpec((tm, SSD_CONV_CH), rows),
            pl.BlockSpec((tm, SSD_HEADS), rows),
            pl.BlockSpec((SSD_HEADS, tm), lambda i: (0, i)),
        ],
        out_shape=[
            jax.ShapeDtypeStruct((npair, m, LANES), BF16),
            jax.ShapeDtypeStruct((npair, m, LANES), BF16),
            jax.ShapeDtypeStruct((npair, m, LANES), BF16),
            jax.ShapeDtypeStruct((m, SB_WIDTH), F32),
            jax.ShapeDtypeStruct((m, SB_WIDTH), F32),
            jax.ShapeDtypeStruct((m, SSD_WIDTH), F32),
            jax.ShapeDtypeStruct((m, SSD_CONV_CH), F32),
            jax.ShapeDtypeStruct((m, SSD_HEADS), F32),
            jax.ShapeDtypeStruct((SSD_HEADS, m), F32),
        ],
        compiler_params=_params(("parallel",)),
        name="even_proj",
    )(x, norm_w.reshape(1, d), w_main, w_dt, w_dtt)


def _sb_tri(n):
    j = jnp.arange(n)[:, None]
    s = jnp.arange(n)[None, :]
    return jnp.concatenate([(j > s).astype(BF16), jnp.ones((n, LANES), BF16)], axis=1)


def _sb_block(qh, kb, vb, tri, bias, carry, valid):
    t = kb.shape[0]
    z = _dot_nt(qh, kb) + bias
    sp = _softplus_tile(z)
    l1m = -sp if valid is None else jnp.where(valid, -sp, 0.0)
    hi = l1m.astype(BF16)
    lo = (l1m - hi.astype(F32)).astype(BF16)
    r = _dot(hi, tri) + _dot(lo, tri)
    rem = r[:, 0:t] + jnp.concatenate([carry] * (t // LANES), axis=1)
    w = jnp.exp(z - sp + rem)
    if valid is not None:
        w = jnp.where(valid, w, 0.0)
    return _dot(w.astype(BF16), vb), r[:, t:t + LANES]


def _sb_prompt_kernel(bias_ref, q_ref, k_ref, v_ref, tri_ref, o_ref, c_scr, a_scr):
    p = pl.program_id(1)
    i = pl.program_id(2)
    tq = q_ref.shape[1]
    q = q_ref[0]
    lane = lax.broadcasted_iota(jnp.int32, (tq, LANES), 1)
    first = lane < SB_HEAD_DIM
    qs = (jnp.where(first, q, jnp.zeros_like(q)), jnp.where(first, jnp.zeros_like(q), q))
    biases = (bias_ref[2 * p], bias_ref[2 * p + 1])
    tri = tri_ref[...]
    c_scr[...] = jnp.zeros_like(c_scr)
    a_scr[...] = jnp.zeros_like(a_scr)

    def step(j, valid):
        kb = k_ref[0, pl.ds(pl.multiple_of(j * tq, tq), tq), :]
        vb = v_ref[0, pl.ds(pl.multiple_of(j * tq, tq), tq), :]
        for h in range(2):
            pv, tot = _sb_block(qs[h], kb, vb, tri, biases[h], c_scr[h], valid)
            a_scr[h] += pv
            c_scr[h] += tot

    row = lax.broadcasted_iota(jnp.int32, (tq, tq), 0)
    col = lax.broadcasted_iota(jnp.int32, (tq, tq), 1)
    step(i, col < row)

    def body(n, _):
        step(i - 1 - n, None)
        return 0

    lax.fori_loop(0, i, body, 0)
    o_ref[0] = jnp.where(first, a_scr[0], a_scr[1])


def _sb_prompt(q, k, v, sb_bias, bsz, t):
    npair, m, _ = q.shape
    tq = SB_TILE
    nq = t // tq
    return pl.pallas_call(
        _sb_prompt_kernel,
        grid_spec=pltpu.PrefetchScalarGridSpec(
            num_scalar_prefetch=1,
            grid=(bsz, npair, nq),
            in_specs=[
                pl.BlockSpec((1, tq, LANES), lambda b, p, i, s: (p, b * nq + i, 0)),
                pl.BlockSpec((1, t, LANES), lambda b, p, i, s: (p, b, 0)),
                pl.BlockSpec((1, t, LANES), lambda b, p, i, s: (p, b, 0)),
                pl.BlockSpec((tq, tq + LANES), lambda b, p, i, s: (0, 0)),
            ],
            out_specs=pl.BlockSpec((1, tq, LANES), lambda b, p, i, s: (p, b * nq + i, 0)),
            scratch_shapes=[pltpu.VMEM((2, tq, LANES), F32), pltpu.VMEM((2, tq, LANES), F32)],
        ),
        out_shape=jax.ShapeDtypeStruct((npair, m, LANES), F32),
        compiler_params=_params(("parallel", "parallel", "arbitrary")),
        name="sb_prompt",
    )(sb_bias.astype(F32), q, k, v, _sb_tri(tq))


def _sb_decode_kernel(pt_ref, q_ref, bias_ref, *refs, group):
    k_refs, v_refs = refs[0:group], refs[group:2 * group]
    tri_ref, o_ref, c_scr, a_scr = refs[2 * group:]
    s = pl.program_id(0)
    g = pl.program_id(1)
    width = q_ref.shape[2]

    @pl.when(g == 0)
    def _():
        c_scr[...] = jnp.zeros_like(c_scr)
        a_scr[...] = jnp.zeros_like(a_scr)

    head = lax.broadcasted_iota(jnp.int32, (SB_HEADS, width), 0)
    lane = lax.broadcasted_iota(jnp.int32, (SB_HEADS, width), 1)
    own = (lane // SB_HEAD_DIM) == head
    qm = jnp.where(own, jnp.broadcast_to(q_ref[0], (SB_HEADS, width)), 0.0).astype(BF16)
    bias = jnp.concatenate([bias_ref[...]] * group, axis=0)
    z = jnp.concatenate([_dot(qm, k_refs[j][0].astype(BF16)) for j in range(group)], axis=0) + bias
    sp = _softplus_tile(z)
    l1m = -sp
    hi = l1m.astype(BF16)
    lo = (l1m - hi.astype(F32)).astype(BF16)
    tot = jnp.sum(l1m, axis=1, keepdims=True)
    carries = [c_scr[...]]
    for j in range(group):
        carries.append(carries[j] + tot[j * SB_HEADS:(j + 1) * SB_HEADS, :])
    c_scr[...] = carries[group]
    rem = _dot(hi, tri_ref[...]) + _dot(lo, tri_ref[...]) + jnp.concatenate(carries[0:group], axis=0)
    w = jnp.exp(z - sp + rem)
    acc = a_scr[...]
    for j in range(group):
        wj = w[j * SB_HEADS:(j + 1) * SB_HEADS, :].astype(BF16)
        acc = acc + _dot_nt(wj, v_refs[j][0].astype(BF16))
    a_scr[...] = acc

    @pl.when(g == pl.num_programs(1) - 1)
    def _():
        o = jnp.sum(jnp.where(own, acc, 0.0), axis=0, keepdims=True)
        for p in range(width // LANES):
            o_ref[p, pl.ds(s, 1), :] = o[:, p * LANES:(p + 1) * LANES]


def _sb_decode(q, cache_kt, cache_vt, page_table, sb_bias):
    n_seq, _, width = q.shape
    page = cache_kt.shape[2]
    npages = page_table.shape[1]
    group = SB_DECODE_GROUP if npages % SB_DECODE_GROUP == 0 else 1
    j = jnp.arange(page)[:, None]
    s = jnp.arange(page)[None, :]
    tri = (j > s).astype(BF16)

    def kv_spec(jj):
        return pl.BlockSpec((1, width, page), lambda b, g, pt: (pt[b, npages - 1 - (g * group + jj)], 0, 0))

    return pl.pallas_call(
        functools.partial(_sb_decode_kernel, group=group),
        grid_spec=pltpu.PrefetchScalarGridSpec(
            num_scalar_prefetch=1,
            grid=(n_seq, npages // group),
            in_specs=[
                pl.BlockSpec((1, 1, width), lambda b, g, pt: (b, 0, 0)),
                pl.BlockSpec((SB_HEADS, 1), lambda b, g, pt: (0, 0)),
                *[kv_spec(jj) for jj in range(group)],
                *[kv_spec(jj) for jj in range(group)],
                pl.BlockSpec((page, page), lambda b, g, pt: (0, 0)),
            ],
            out_specs=pl.BlockSpec((width // LANES, n_seq, LANES), lambda b, g, pt: (0, 0, 0)),
            scratch_shapes=[pltpu.VMEM((SB_HEADS, 1), F32), pltpu.VMEM((SB_HEADS, width), F32)],
        ),
        out_shape=jax.ShapeDtypeStruct((width // LANES, n_seq, LANES), F32),
        compiler_params=_params(("arbitrary", "arbitrary")),
        name="sb_decode",
    )(page_table, q, sb_bias.astype(F32).reshape(SB_HEADS, 1), *([cache_kt] * group), *([cache_vt] * group), tri)


def _ssd_prompt_kernel(xbc_ref, z_ref, dt_ref, dtt_ref, cw_ref, cb_ref, dtb_ref, dtbt_ref, alog_ref, alogt_ref,
                       dskip_ref, nw_ref, y_ref, conv_ref, ssm_ref, buf, st):
    c = pl.program_id(1)
    nc = pl.num_programs(1)
    L = xbc_ref.shape[0]
    pad = SUBLANES

    @pl.when(c == 0)
    def _():
        buf[0:pad, :] = jnp.zeros((pad, SSD_CONV_CH), F32)
        st[...] = jnp.zeros_like(st)

    buf[pad:pad + L, :] = xbc_ref[...]
    conv = cb_ref[...]
    for j in range(SSD_CONV):
        off = pad - (SSD_CONV - 1) + j
        conv = conv + cw_ref[j:j + 1, :] * buf[off:off + L, :]
    tail = buf[pad + L - (SSD_CONV - 1):pad + L, :]
    buf[pad - (SSD_CONV - 1):pad, :] = tail
    xa = _silu(conv)
    xs = xa[:, 0:SSD_WIDTH]
    gw = SSD_STATE
    bm = [xa[:, SSD_WIDTH + g * gw:SSD_WIDTH + (g + 1) * gw].astype(BF16) for g in range(SSD_GROUPS)]
    cm = [xa[:, SSD_WIDTH + (SSD_GROUPS + g) * gw:SSD_WIDTH + (SSD_GROUPS + g + 1) * gw].astype(BF16)
          for g in range(SSD_GROUPS)]

    dt = _softplus(dt_ref[...] + dtb_ref[...])
    dtt = _softplus(dtt_ref[...] + dtbt_ref[...])
    a = -jnp.exp(alog_ref[...])
    at = -jnp.exp(alogt_ref[...])
    row = lax.broadcasted_iota(jnp.int32, (L, L), 0)
    col = lax.broadcasted_iota(jnp.int32, (L, L), 1)
    lower = col <= row
    lower_b = lower.astype(BF16)
    upper_b = (row <= col).astype(BF16)
    cum = sum(_dot(lower_b, part) for part in _split3(dt * a))
    cumt = sum(_dot(part, upper_b) for part in _split3(dtt * at))
    last = cum[L - 1:L, :]
    wst = jnp.exp(last - cum) * dt
    ecum = jnp.exp(cum)
    elast = jnp.exp(last)

    lane = lax.broadcasted_iota(jnp.int32, (L, LANES), 1)
    first = lane < SSD_HEAD_DIM
    first_row = first[0:1, :]
    heads_per_group = SSD_HEADS // SSD_GROUPS
    cb = [_dot_nt(cm[g], bm[g]) for g in range(SSD_GROUPS)]
    ys = []
    for p in range(SSD_WIDTH // LANES):
        g = (2 * p) // heads_per_group
        xp = xs[:, p * LANES:(p + 1) * LANES]
        xpb = xp.astype(BF16)
        yi = []
        for h in (2 * p, 2 * p + 1):
            seg = cum[:, h:h + 1] - cumt[h:h + 1, :]
            dec = jnp.where(lower, jnp.exp(jnp.where(lower, seg, 0.0)), 0.0) * dtt[h:h + 1, :]
            yi.append(_dot((cb[g] * dec).astype(BF16), xpb))
        y_intra = jnp.where(first, yi[0], yi[1])
        stp = st[:, p * LANES:(p + 1) * LANES]
        ec = jnp.where(first, ecum[:, 2 * p:2 * p + 1], ecum[:, 2 * p + 1:2 * p + 2])
        y_inter = _dot(cm[g], stp.astype(BF16)) * ec
        wp = jnp.where(first, wst[:, 2 * p:2 * p + 1], wst[:, 2 * p + 1:2 * p + 2])
        el = jnp.where(first_row, elast[:, 2 * p:2 * p + 1], elast[:, 2 * p + 1:2 * p + 2])
        st[:, p * LANES:(p + 1) * LANES] = el * stp + _dot_tn(bm[g], (xp * wp).astype(BF16))
        ys.append(y_intra + y_inter + dskip_ref[:, p * LANES:(p + 1) * LANES] * xp)
    y = jnp.concatenate(ys, axis=1)
    y_ref[...] = _rmsnorm(y * _silu(z_ref[...]), nw_ref[...])

    @pl.when(c == nc - 1)
    def _():
        conv_ref[0] = tail
        ssm_ref[0] = st[...].T


def _ssd_prompt(xbc, z, dt, dtt, conv_w, conv_b, dt_bias, a_log, d_skip, ssd_norm, bsz, t):
    m = xbc.shape[0]
    L = SSD_CHUNK
    nc = t // L
    rows = lambda b, c: (b * nc + c, 0)
    full = lambda b, c: (0, 0)
    vec = lambda v: v.astype(F32).reshape(1, -1)
    colv = lambda v: v.astype(F32).reshape(-1, 1)
    y, conv, ssm = pl.pallas_call(
        _ssd_prompt_kernel,
        grid=(bsz, nc),
        in_specs=[
            pl.BlockSpec((L, SSD_CONV_CH), rows),
            pl.BlockSpec((L, SSD_WIDTH), rows),
            pl.BlockSpec((L, SSD_HEADS), rows),
            pl.BlockSpec((SSD_HEADS, L), lambda b, c: (0, b * nc + c)),
            pl.BlockSpec((SSD_CONV, SSD_CONV_CH), full),
            pl.BlockSpec((1, SSD_CONV_CH), full),
            pl.BlockSpec((1, SSD_HEADS), full),
            pl.BlockSpec((SSD_HEADS, 1), full),
            pl.BlockSpec((1, SSD_HEADS), full),
            pl.BlockSpec((SSD_HEADS, 1), full),
            pl.BlockSpec((1, SSD_WIDTH), full),
            pl.BlockSpec((1, SSD_WIDTH), full),
        ],
        out_specs=[
            pl.BlockSpec((L, SSD_WIDTH), rows),
            pl.BlockSpec((1, SSD_CONV - 1, SSD_CONV_CH), lambda b, c: (b, 0, 0)),
            pl.BlockSpec((1, SSD_WIDTH, SSD_STATE), lambda b, c: (b, 0, 0)),
        ],
        out_shape=[
            jax.ShapeDtypeStruct((m, SSD_WIDTH), F32),
            jax.ShapeDtypeStruct((bsz, SSD_CONV - 1, SSD_CONV_CH), F32),
            jax.ShapeDtypeStruct((bsz, SSD_WIDTH, SSD_STATE), F32),
        ],
        scratch_shapes=[pltpu.VMEM((L + SUBLANES, SSD_CONV_CH), F32), pltpu.VMEM((SSD_STATE, SSD_WIDTH), F32)],
        compiler_params=_params(("parallel", "arbitrary")),
        name="ssd_prompt",
    )(xbc, z, dt, dtt, conv_w.astype(F32), vec(conv_b), vec(dt_bias), colv(dt_bias), vec(a_log), colv(a_log),
      vec(jnp.repeat(d_skip, SSD_HEAD_DIM)), vec(ssd_norm))
    return y, conv, ssm.reshape(bsz, SSD_HEADS, SSD_HEAD_DIM, SSD_STATE)


def _ssd_step_kernel(xbc_ref, z_ref, dt_ref, sconv_ref, sssm_ref, cw_ref, cb_ref, dtb_ref, alog_ref, dskip_ref,
                     nw_ref, y_ref, conv_ref, ssm_ref):
    xr = xbc_ref[0]
    cs = sconv_ref[0]
    conv = cb_ref[...] + cw_ref[SSD_CONV - 1:SSD_CONV, :] * xr
    for j in range(SSD_CONV - 1):
        conv = conv + cw_ref[j:j + 1, :] * cs[j:j + 1, :]
    conv_ref[0] = jnp.concatenate([cs[1:SSD_CONV - 1, :], xr], axis=0)
    xa = _silu(conv)
    xs = xa[:, 0:SSD_WIDTH]
    dt = _softplus(dt_ref[0] + dtb_ref[...])
    da = jnp.exp(dt * -jnp.exp(alog_ref[...]))
    lane_head = lax.broadcasted_iota(jnp.int32, (1, SSD_WIDTH), 1) // SSD_HEAD_DIM
    dt_w = jnp.zeros((1, SSD_WIDTH), F32)
    da_w = jnp.zeros((1, SSD_WIDTH), F32)
    for h in range(SSD_HEADS):
        dt_w = jnp.where(lane_head == h, dt[:, h:h + 1], dt_w)
        da_w = jnp.where(lane_head == h, da[:, h:h + 1], da_w)
    dtx_col = _row_to_col(dt_w * xs)
    da_col = _row_to_col(da_w)
    rows_per_group = SSD_WIDTH // SSD_GROUPS
    row = lax.broadcasted_iota(jnp.int32, (SSD_WIDTH, SSD_STATE), 0)
    b_rows = jnp.zeros((SSD_WIDTH, SSD_STATE), F32)
    c_rows = jnp.zeros((SSD_WIDTH, SSD_STATE), F32)
    for g in range(SSD_GROUPS):
        sel = (row // rows_per_group) == g
        b0 = SSD_WIDTH + g * SSD_STATE
        c0 = SSD_WIDTH + (SSD_GROUPS + g) * SSD_STATE
        b_rows = jnp.where(sel, xa[:, b0:b0 + SSD_STATE], b_rows)
        c_rows = jnp.where(sel, xa[:, c0:c0 + SSD_STATE], c_rows)
    new = da_col * sssm_ref[0] + dtx_col * b_rows
    ssm_ref[0] = new
    y = _col_to_row(jnp.sum(new * c_rows, axis=1, keepdims=True)) + dskip_ref[...] * xs
    y_ref[0] = _rmsnorm(y * _silu(z_ref[0]), nw_ref[...])


def _ssd_step(xbc, z, dt, state_conv, state_ssm, conv_w, conv_b, dt_bias, a_log, d_skip, ssd_norm):
    n = xbc.shape[0]
    vec = lambda v: v.astype(F32).reshape(1, -1)
    per = lambda i: (i, 0, 0)
    full = lambda i: (0, 0)
    y, conv, ssm = pl.pallas_call(
        _ssd_step_kernel,
        grid=(n,),
        in_specs=[
            pl.BlockSpec((1, 1, SSD_CONV_CH), per),
            pl.BlockSpec((1, 1, SSD_WIDTH), per),
            pl.BlockSpec((1, 1, SSD_HEADS), per),
            pl.BlockSpec((1, SSD_CONV - 1, SSD_CONV_CH), per),
            pl.BlockSpec((1, SSD_WIDTH, SSD_STATE), per),
            pl.BlockSpec((SSD_CONV, SSD_CONV_CH), full),
            pl.BlockSpec((1, SSD_CONV_CH), full),
            pl.BlockSpec((1, SSD_HEADS), full),
            pl.BlockSpec((1, SSD_HEADS), full),
            pl.BlockSpec((1, SSD_WIDTH), full),
            pl.BlockSpec((1, SSD_WIDTH), full),
        ],
        out_specs=[
            pl.BlockSpec((1, 1, SSD_WIDTH), per),
            pl.BlockSpec((1, SSD_CONV - 1, SSD_CONV_CH), per),
            pl.BlockSpec((1, SSD_WIDTH, SSD_STATE), per),
        ],
        out_shape=[
            jax.ShapeDtypeStruct((n, 1, SSD_WIDTH), F32),
            jax.ShapeDtypeStruct((n, SSD_CONV - 1, SSD_CONV_CH), F32),
            jax.ShapeDtypeStruct((n, SSD_WIDTH, SSD_STATE), F32),
        ],
        compiler_params=_params(("parallel",)),
        name="ssd_step",
    )(xbc.reshape(n, 1, SSD_CONV_CH), z.reshape(n, 1, SSD_WIDTH), dt.reshape(n, 1, SSD_HEADS),
      state_conv.astype(F32), state_ssm.astype(F32).reshape(n, SSD_WIDTH, SSD_STATE), conv_w.astype(F32),
      vec(conv_b), vec(dt_bias), vec(a_log), vec(jnp.repeat(d_skip, SSD_HEAD_DIM)), vec(ssd_norm))
    return y.reshape(n, SSD_WIDTH), conv, ssm.reshape(n, SSD_HEADS, SSD_HEAD_DIM, SSD_STATE)


def _even_out_kernel(osb_ref, y_ref, x_ref, w_ref, o_ref):
    mix = jnp.concatenate([osb_ref[p] for p in range(osb_ref.shape[0])] + [y_ref[...]], axis=1)
    o_ref[...] = x_ref[...] + _dot(mix.astype(BF16), w_ref[...])


def _even_out(osb, y, x, w_out, tm):
    m, d = x.shape
    npair = osb.shape[0]
    rows = lambda i: (i, 0)
    return pl.pallas_call(
        _even_out_kernel,
        grid=(m // tm,),
        in_specs=[
            pl.BlockSpec((npair, tm, LANES), lambda i: (0, i, 0)),
            pl.BlockSpec((tm, SSD_WIDTH), rows),
            pl.BlockSpec((tm, d), rows),
            pl.BlockSpec(w_out.shape, lambda i: (0, 0)),
        ],
        out_specs=pl.BlockSpec((tm, d), rows),
        out_shape=jax.ShapeDtypeStruct((m, d), F32),
        compiler_params=_params(("parallel",)),
        name="even_out",
    )(osb, y, x, w_out.astype(BF16))


def _odd_out_kernel(o_ref, g_ref, x_ref, nw_ref, w_ref, out_ref):
    o = jnp.concatenate([o_ref[h] for h in range(o_ref.shape[0])], axis=1)
    gated = _rmsnorm(o, nw_ref[...]) * _silu(g_ref[...])
    out_ref[...] = x_ref[...] + _dot(gated.astype(BF16), w_ref[...])


def _odd_out(o, g, x, hg_norm, w_out, tm):
    m, d = x.shape
    nh = o.shape[0]
    rows = lambda i: (i, 0)
    return pl.pallas_call(
        _odd_out_kernel,
        grid=(m // tm,),
        in_specs=[
            pl.BlockSpec((nh, tm, HG_VAL), lambda i: (0, i, 0)),
            pl.BlockSpec((tm, nh * HG_VAL), rows),
            pl.BlockSpec((tm, d), rows),
            pl.BlockSpec((1, nh * HG_VAL), lambda i: (0, 0)),
            pl.BlockSpec(w_out.shape, lambda i: (0, 0)),
        ],
        out_specs=pl.BlockSpec((tm, d), rows),
        out_shape=jax.ShapeDtypeStruct((m, d), F32),
        compiler_params=_params(("parallel",)),
        name="odd_out",
    )(o, g, x, hg_norm.astype(F32).reshape(1, -1), w_out.astype(BF16))


def _mlp_kernel(x_ref, nw_ref, w1_ref, w2_ref, fw_ref, o_ref, hn_scr, acc_scr, *, final_norm):
    j = pl.program_id(1)

    @pl.when(j == 0)
    def _():
        hn_scr[...] = _rmsnorm(x_ref[...], nw_ref[...]).astype(BF16)
        acc_scr[...] = jnp.zeros_like(acc_scr)

    h = jnp.maximum(_dot(hn_scr[...], w1_ref[...]), 0.0)
    acc_scr[...] += _dot((h * h).astype(BF16), w2_ref[...])

    @pl.when(j == pl.num_programs(1) - 1)
    def _():
        out = x_ref[...] + acc_scr[...]
        o_ref[...] = _rmsnorm(out, fw_ref[...]) if final_norm else out


def _mlp(x, norm_w, w1, w2, final_w, tm, tf):
    m, d = x.shape
    f = w1.shape[1]
    final_norm = final_w is not None
    fw = (final_w if final_norm else jnp.ones((d,), F32)).astype(F32).reshape(1, d)
    return pl.pallas_call(
        functools.partial(_mlp_kernel, final_norm=final_norm),
        grid=(m // tm, f // tf),
        in_specs=[
            pl.BlockSpec((tm, d), lambda i, j: (i, 0)),
            pl.BlockSpec((1, d), lambda i, j: (0, 0)),
            pl.BlockSpec((d, tf), lambda i, j: (0, j)),
            pl.BlockSpec((tf, d), lambda i, j: (j, 0)),
            pl.BlockSpec((1, d), lambda i, j: (0, 0)),
        ],
        out_specs=pl.BlockSpec((tm, d), lambda i, j: (i, 0)),
        out_shape=jax.ShapeDtypeStruct((m, d), F32),
        scratch_shapes=[pltpu.VMEM((tm, d), BF16), pltpu.VMEM((tm, d), F32)],
        compiler_params=_params(("parallel", "arbitrary")),
        name="mlp",
    )(x, norm_w.astype(F32).reshape(1, d), w1.astype(BF16), w2.astype(BF16), fw)


def _odd_proj_kernel(x_ref, nw_ref, w_ref, q_ref, f_ref, v_ref, g_ref):
    hn = _rmsnorm(x_ref[...], nw_ref[...]).astype(BF16)
    width = HG_HEADS * HG_KEY
    per_chunk = PROJ_COLS // HG_KEY
    for i, ref in enumerate((q_ref, f_ref, v_ref)):
        for c in range(width // PROJ_COLS):
            y = _dot(hn, w_ref[:, i * width + c * PROJ_COLS:i * width + (c + 1) * PROJ_COLS])
            for h in range(per_chunk):
                ref[c * per_chunk + h] = y[:, h * HG_KEY:(h + 1) * HG_KEY]
    for c in range(width // PROJ_COLS):
        g_ref[:, c * PROJ_COLS:(c + 1) * PROJ_COLS] = _dot(
            hn, w_ref[:, 3 * width + c * PROJ_COLS:3 * width + (c + 1) * PROJ_COLS])


def _odd_proj(x, norm_w, w_in, tm):
    m, d = x.shape
    width = HG_HEADS * HG_KEY
    heads = lambda i: (0, i, 0)
    per_head = jax.ShapeDtypeStruct((HG_HEADS, m, HG_KEY), F32)
    return pl.pallas_call(
        _odd_proj_kernel,
        grid=(m // tm,),
        in_specs=[
            pl.BlockSpec((tm, d), lambda i: (i, 0)),
            pl.BlockSpec((1, d), lambda i: (0, 0)),
            pl.BlockSpec(w_in.shape, lambda i: (0, 0)),
        ],
        out_specs=[
            pl.BlockSpec((HG_HEADS, tm, HG_KEY), heads),
            pl.BlockSpec((HG_HEADS, tm, HG_KEY), heads),
            pl.BlockSpec((HG_HEADS, tm, HG_KEY), heads),
            pl.BlockSpec((tm, width), lambda i: (i, 0)),
        ],
        out_shape=[per_head, per_head, per_head, jax.ShapeDtypeStruct((m, width), F32)],
        compiler_params=_params(("parallel",)),
        name="odd_proj",
    )(x, norm_w.astype(F32).reshape(1, d), w_in.astype(BF16))


def _hgrn_lower_bound(raw, layer):
    e = jnp.exp(raw - jnp.max(raw, axis=0, keepdims=True))
    p = e / jnp.sum(e, axis=0, keepdims=True)
    lb = jnp.zeros_like(p[0])
    for l in range(1, layer + 1):
        lb = lb + p[l]
    return lb


def _hgrn_gates(fpre, lb):
    sig, nsig = _sigmoid_pair(fpre)
    return lb + (1.0 - lb) * sig, (1.0 - lb) * nsig


def _hgrn_prompt_kernel(q_ref, f_ref, v_ref, lb_ref, o_ref, s_ref, st, b_scr, k_scr, od_scr, *, layer):
    c = pl.program_id(1)
    nc = pl.num_programs(1)
    L = q_ref.shape[1]
    nblk = L // HG_DIAG

    @pl.when(c == 0)
    def _():
        st[...] = jnp.zeros_like(st)

    row = lax.broadcasted_iota(jnp.int32, (L, L), 0)
    col = lax.broadcasted_iota(jnp.int32, (L, L), 1)
    lower_b = (col <= row).astype(BF16)
    pos = lax.broadcasted_iota(jnp.int32, (L, HG_KEY), 0)

    def head(h, _):
        q = q_ref[h]
        v = v_ref[h]
        fg, kin = _hgrn_gates(f_ref[h], _hgrn_lower_bound(lb_ref[:, h], layer))
        b = sum(_dot(lower_b, part) for part in _split3(jnp.log(fg)))
        b_scr[...] = b
        k_scr[...] = kin
        vb = v.astype(BF16)
        stt = st[h]
        o = _dot_nt((q * jnp.exp(b)).astype(BF16), stt.astype(BF16))
        b_last = b[L - 1:L, :]
        st[h] = stt * jnp.exp(b_last) + _dot_tn(vb, (kin * jnp.exp(b_last - b)).astype(BF16))

        att = jnp.zeros((L, L), F32)
        g = L
        while g > HG_DIAG:
            half = g // 2
            b3 = b.reshape(L // g, g, HG_KEY)
            mid = jnp.broadcast_to(b3[:, half - 1:half, :], b3.shape).reshape(L, HG_KEY)
            late = (pos & (g - 1)) >= half
            qd = jnp.where(late, q * jnp.exp(jnp.minimum(b - mid, 0.0)), 0.0)
            kd = jnp.where(late, 0.0, kin * jnp.exp(jnp.minimum(mid - b, 0.0)))
            same = (row ^ col) < g
            att = att + jnp.where(same, _dot_nt(qd.astype(BF16), kd.astype(BF16)), 0.0)
            g = half
        o = o + _dot(att.astype(BF16), vb)

        at = lambda ref, i: ref[pl.ds(i, nblk, stride=HG_DIAG), :]
        bs = [at(b_scr, i) for i in range(HG_DIAG)]
        ks = [at(k_scr, i) for i in range(HG_DIAG)]
        vs = [at(v_ref.at[h], i) for i in range(HG_DIAG)]
        for i in range(HG_DIAG):
            qi = at(q_ref.at[h], i)
            acc = jnp.zeros((nblk, HG_VAL), F32)
            for j in range(i + 1):
                a = jnp.sum(qi * ks[j] * jnp.exp(bs[i] - bs[j]), axis=1, keepdims=True)
                acc = acc + a * vs[j]
            od_scr[pl.ds(i, nblk, stride=HG_DIAG), :] = acc
        o_ref[h] = o + od_scr[...]
        return 0

    lax.fori_loop(0, q_ref.shape[0], head, 0)

    @pl.when(c == nc - 1)
    def _():
        for h in range(s_ref.shape[1]):
            s_ref[0, h] = st[h].T


def _hgrn_prompt(q, f, v, lb_raw, layer, bsz, t):
    nh, m, _ = q.shape
    L = HG_CHUNK
    nc = t // L
    blk = lambda b, c: (0, b * nc + c, 0)
    depth = lb_raw.shape[0]
    return pl.pallas_call(
        functools.partial(_hgrn_prompt_kernel, layer=layer),
        grid=(bsz, nc),
        in_specs=[
            pl.BlockSpec((nh, L, HG_KEY), blk),
            pl.BlockSpec((nh, L, HG_KEY), blk),
            pl.BlockSpec((nh, L, HG_VAL), blk),
            pl.BlockSpec((depth, nh, 1, HG_KEY), lambda b, c: (0, 0, 0, 0)),
        ],
        out_specs=[
            pl.BlockSpec((nh, L, HG_VAL), blk),
            pl.BlockSpec((1, nh, HG_KEY, HG_VAL), lambda b, c: (b, 0, 0, 0)),
        ],
        out_shape=[
            jax.ShapeDtypeStruct((nh, m, HG_VAL), F32),
            jax.ShapeDtypeStruct((bsz, nh, HG_KEY, HG_VAL), F32),
        ],
        scratch_shapes=[
            pltpu.VMEM((nh, HG_VAL, HG_KEY), F32),
            pltpu.VMEM((L, HG_KEY), F32),
            pltpu.VMEM((L, HG_KEY), F32),
            pltpu.VMEM((L, HG_VAL), F32),
        ],
        compiler_params=_params(("parallel", "arbitrary")),
        name="hgrn_prompt",
    )(q, f, v, lb_raw.astype(F32).reshape(depth, nh, 1, HG_KEY))


def _hgrn_step_kernel(q_ref, f_ref, v_ref, lb_ref, s_ref, o_ref, snew_ref, *, layer):
    i = pl.program_id(0)
    for h in range(q_ref.shape[0]):
        q = q_ref[h, pl.ds(i, 1), :]
        v = v_ref[h, pl.ds(i, 1), :]
        fg, kin = _hgrn_gates(f_ref[h, pl.ds(i, 1), :], _hgrn_lower_bound(lb_ref[:, h], layer))
        new = _row_to_col(fg) * s_ref[0, h] + _row_to_col(kin) * v
        snew_ref[0, h] = new
        o_ref[h, pl.ds(i, 1), :] = jnp.sum(_row_to_col(q) * new, axis=0, keepdims=True)


def _hgrn_step(q, f, v, lb_raw, state, layer):
    nh, n, _ = q.shape
    depth = lb_raw.shape[0]
    whole = lambda i: (0, 0, 0)
    return pl.pallas_call(
        functools.partial(_hgrn_step_kernel, layer=layer),
        grid=(n,),
        in_specs=[
            pl.BlockSpec((nh, n, HG_KEY), whole),
            pl.BlockSpec((nh, n, HG_KEY), whole),
            pl.BlockSpec((nh, n, HG_VAL), whole),
            pl.BlockSpec((depth, nh, 1, HG_KEY), lambda i: (0, 0, 0, 0)),
            pl.BlockSpec((1, nh, HG_KEY, HG_VAL), lambda i: (i, 0, 0, 0)),
        ],
        out_specs=[
            pl.BlockSpec((nh, n, HG_VAL), whole),
            pl.BlockSpec((1, nh, HG_KEY, HG_VAL), lambda i: (i, 0, 0, 0)),
        ],
        out_shape=[
            jax.ShapeDtypeStruct((nh, n, HG_VAL), F32),
            jax.ShapeDtypeStruct((n, nh, HG_KEY, HG_VAL), F32),
        ],
        compiler_params=_params(("arbitrary",)),
        name="hgrn_step",
    )(q, f, v, lb_raw.astype(F32).reshape(depth, nh, 1, HG_KEY), state.astype(F32))


def _row_tile(m, want):
    return want if m % want == 0 else m


def kernel(x_prompt, x_sample, cache_k, cache_v, page_table, state_conv, state_ssm, state_hgrn, norm_mix, norm_ffn,
           norm_final, w_in_even, sb_bias, conv_w, conv_b, dt_bias, a_log, d_skip, ssd_norm, w_out_even, w_in_odd,
           hg_lb_raw, hg_norm, w_out_odd, w_ff1, w_ff2):
    bsz, t, d = x_prompt.shape
    n_seq = x_sample.shape[0]
    depth = norm_mix.shape[0]
    mp = bsz * t
    hp = x_prompt.reshape(mp, d)
    hs = x_sample.reshape(n_seq, d)
    tmp = _row_tile(mp, 512)
    tmm = _row_tile(mp, 1024)
    tf = 512
    npair = SB_WIDTH // LANES
    outs = {k: [] for k in ("kp", "vp", "ks", "vs", "cp", "cs", "sp", "ss", "gp", "gs")}

    for layer in range(depth):
        li = layer // 2
        if layer % 2 == 0:
            ssd_w = (conv_w[li], conv_b[li], dt_bias[li], a_log[li], d_skip[li], ssd_norm[li])
            q, k, v, kf, vf, z, xbc, dt, dtt = _even_proj(hp, norm_mix[layer], w_in_even[li], tmp)
            osb = _sb_prompt(q, k, v, sb_bias[li], bsz, t)
            y, cp, sp = _ssd_prompt(xbc, z, dt, dtt, *ssd_w, bsz, t)
            hp = _even_out(osb, y, hp, w_out_even[li], tmp)
            outs["kp"].append(kf.reshape(bsz, t, SB_HEADS, SB_HEAD_DIM))
            outs["vp"].append(vf.reshape(bsz, t, SB_HEADS, SB_HEAD_DIM))
            outs["cp"].append(cp)
            outs["sp"].append(sp)
            q, k, v, kf, vf, z, xbc, dt, dtt = _even_proj(hs, norm_mix[layer], w_in_even[li], n_seq)
            qs = q.astype(F32).transpose(1, 0, 2).reshape(n_seq, 1, SB_WIDTH)
            n_phys, page = cache_k.shape[1], cache_k.shape[2]
            to_lanes = lambda c: jnp.transpose(c, (0, 2, 3, 1)).reshape(n_phys, SB_WIDTH, page)
            osb = _sb_decode(qs, to_lanes(cache_k[li]), to_lanes(cache_v[li]), page_table, sb_bias[li])
            y, cs, ss = _ssd_step(xbc, z, dt, state_conv[li], state_ssm[li], *ssd_w)
            hs = _even_out(osb, y, hs, w_out_even[li], n_seq)
            outs["ks"].append(kf.reshape(n_seq, 1, SB_HEADS, SB_HEAD_DIM))
            outs["vs"].append(vf.reshape(n_seq, 1, SB_HEADS, SB_HEAD_DIM))
            outs["cs"].append(cs)
            outs["ss"].append(ss)
        else:
            q, f, v, g = _odd_proj(hp, norm_mix[layer], w_in_odd[li], tmp)
            o, gp = _hgrn_prompt(q, f, v, hg_lb_raw, layer, bsz, t)
            hp = _odd_out(o, g, hp, hg_norm[li], w_out_odd[li], tmp)
            outs["gp"].append(gp)
            q, f, v, g = _odd_proj(hs, norm_mix[layer], w_in_odd[li], n_seq)
            o, gs = _hgrn_step(q, f, v, hg_lb_raw, state_hgrn[li], layer)
            hs = _odd_out(o, g, hs, hg_norm[li], w_out_odd[li], n_seq)
            outs["gs"].append(gs)
        fw = norm_final if layer == depth - 1 else None
        hp = _mlp(hp, norm_ffn[layer], w_ff1[layer], w_ff2[layer], fw, tmm, tf)
        hs = _mlp(hs, norm_ffn[layer], w_ff1[layer], w_ff2[layer], fw, n_seq, tf)

    y_prompt = hp.reshape(bsz, t, d)
    y_sample = hs.reshape(n_seq, 1, d)
    st = lambda key: jnp.stack(outs[key])
    return (y_prompt, y_sample, st("kp"), st("vp"), st("ks"), st("vs"), st("cp"), st("cs"), st("sp"), st("ss"),
            st("gp"), st("gs"))
```

```python
import functools

import jax
import jax.numpy as jnp
from jax import lax
from jax.experimental import pallas as pl
from jax.experimental.pallas import tpu as pltpu

F32 = jnp.float32
BF16 = jnp.bfloat16

EPS = 1e-6
LOG2E = 1.4426950408889634
LN2 = 0.6931471805599453
SB_HEADS = 8
SB_HEAD_DIM = 64
SB_WIDTH = SB_HEADS * SB_HEAD_DIM
SSD_HEADS = 8
SSD_HEAD_DIM = 64
SSD_WIDTH = SSD_HEADS * SSD_HEAD_DIM
SSD_STATE = 128
SSD_GROUPS = 2
SSD_CONV = 4
SSD_CONV_CH = SSD_WIDTH + 2 * SSD_GROUPS * SSD_STATE
HG_HEADS = 8
HG_KEY = 128
HG_VAL = 128

LANES = 128
SUBLANES = 8
VMEM_LIMIT_BYTES = 52 * 1024 * 1024

SB_TILE = 256
SB_DECODE_GROUP = 16
SSD_CHUNK = 128
HG_CHUNK = 128
HG_DIAG = 16
HG_HEAD_GROUP = 8
PROJ_COLS = 512


def _params(sem):
    return pltpu.CompilerParams(dimension_semantics=sem, vmem_limit_bytes=VMEM_LIMIT_BYTES)


def _dot(a, b):
    return jnp.dot(a, b, preferred_element_type=F32)


def _dot_nt(a, b):
    return lax.dot_general(a, b, (((1,), (1,)), ((), ())), preferred_element_type=F32)


def _dot_tn(a, b):
    return lax.dot_general(a, b, (((0,), (0,)), ((), ())), preferred_element_type=F32)


def _split3(x):
    hi = x.astype(BF16)
    r = x - hi.astype(F32)
    mid = r.astype(BF16)
    lo = (r - mid.astype(F32)).astype(BF16)
    return hi, mid, lo


def _rmsnorm(x, w):
    return x * lax.rsqrt(jnp.mean(x * x, axis=-1, keepdims=True) + EPS) * w


def _softplus(x):
    return jnp.maximum(x, 0.0) + jnp.log1p(jnp.exp(-jnp.abs(x)))


def _sigmoid_pair(x):
    e = jnp.exp(-jnp.abs(x))
    r = 1.0 / (1.0 + e)
    big, small = r, e * r
    pos = x >= 0
    return jnp.where(pos, big, small), jnp.where(pos, small, big)


def _silu(x):
    return x * _sigmoid_pair(x)[0]


def _row_to_col(row):
    n = row.shape[1]
    eye = lax.broadcasted_iota(jnp.int32, (n, n), 0) == lax.broadcasted_iota(jnp.int32, (n, n), 1)
    return jnp.sum(jnp.where(eye, jnp.broadcast_to(row, (n, n)), 0.0), axis=1, keepdims=True)


def _col_to_row(col):
    n = col.shape[0]
    eye = lax.broadcasted_iota(jnp.int32, (n, n), 0) == lax.broadcasted_iota(jnp.int32, (n, n), 1)
    return jnp.sum(jnp.where(eye, jnp.broadcast_to(col, (n, n)), 0.0), axis=0, keepdims=True)


def _even_proj_kernel(x_ref, nw_ref, w_ref, wdt_ref, wdtt_ref,
                      q_ref, k_ref, v_ref, kf_ref, vf_ref, z_ref, xbc_ref, dt_ref, dtt_ref, *, token_minor):
    hn = _rmsnorm(x_ref[...], nw_ref[...]).astype(BF16)
    scale = LOG2E * SB_HEAD_DIM ** -0.5
    npair = SB_WIDTH // LANES
    q = _dot(hn, w_ref[:, 0:SB_WIDTH]) * scale
    for p in range(npair):
        q_ref[p] = q[:, p * LANES:(p + 1) * LANES].astype(BF16)
    for i, (pair_ref, full_ref) in enumerate(((k_ref, kf_ref), (v_ref, vf_ref))):
        y = _dot(hn, w_ref[:, (i + 1) * SB_WIDTH:(i + 2) * SB_WIDTH])
        if token_minor:
            full_ref[0] = y.T
        else:
            full_ref[...] = y
        for p in range(npair):
            pair_ref[p] = y[:, p * LANES:(p + 1) * LANES].astype(BF16)
    z0 = 3 * SB_WIDTH
    z_ref[...] = _dot(hn, w_ref[:, z0:z0 + SSD_WIDTH])
    x0 = z0 + SSD_WIDTH
    for c in range(SSD_CONV_CH // PROJ_COLS):
        xbc_ref[:, c * PROJ_COLS:(c + 1) * PROJ_COLS] = _dot(
            hn, w_ref[:, x0 + c * PROJ_COLS:x0 + (c + 1) * PROJ_COLS])
    dt_ref[...] = _dot(hn, wdt_ref[...])[:, 0:SSD_HEADS]
    dtt_ref[...] = _dot_nt(wdtt_ref[...], hn)


def _even_proj(x, norm_w, w_in, tm, seq_len=None):
    m, d = x.shape
    npair = SB_WIDTH // LANES
    wb = w_in.astype(BF16)
    n_main = 3 * SB_WIDTH + SSD_WIDTH + SSD_CONV_CH
    w_main = wb[:, :n_main]
    w_dt = jnp.pad(wb[:, n_main:], ((0, 0), (0, LANES - SSD_HEADS)))
    w_dtt = wb[:, n_main:].T
    full = lambda i: (0, 0)
    rows = lambda i: (i, 0)
    hp = lambda i: (0, i, 0)
    token_minor = seq_len is not None
    if token_minor:
        nt = seq_len // tm
        kv_spec = pl.BlockSpec((1, SB_WIDTH, tm), lambda i: (i // nt, 0, i % nt))
        kv_shape = jax.ShapeDtypeStruct((m // seq_len, SB_WIDTH, seq_len), F32)
    else:
        kv_spec = pl.BlockSpec((tm, SB_WIDTH), rows)
        kv_shape = jax.ShapeDtypeStruct((m, SB_WIDTH), F32)
    return pl.pallas_call(
        functools.partial(_even_proj_kernel, token_minor=token_minor),
        grid=(m // tm,),
        in_specs=[
            pl.BlockSpec((tm, d), rows),
            pl.BlockSpec((1, d), full),
            pl.BlockSpec((d, n_main), full),
            pl.BlockSpec((d, LANES), full),
            pl.BlockSpec((SSD_HEADS, d), full),
        ],
        out_specs=[
            pl.BlockSpec((npair, tm, LANES), hp),
            pl.BlockSpec((npair, tm, LANES), hp),
            pl.BlockSpec((npair, tm, LANES), hp),
            kv_spec,
            kv_spec,
            pl.BlockSpec((tm, SSD_WIDTH), rows),
            pl.BlockSpec((tm, SSD_CONV_CH), rows),
            pl.BlockSpec((tm, SSD_HEADS), rows),
            pl.BlockSpec((SSD_HEADS, tm), lambda i: (0, i)),
        ],
        out_shape=[
            jax.ShapeDtypeStruct((npair, m, LANES), BF16),
            jax.ShapeDtypeStruct((npair, m, LANES), BF16),
            jax.ShapeDtypeStruct((npair, m, LANES), BF16),
            kv_shape,
            kv_shape,
            jax.ShapeDtypeStruct((m, SSD_WIDTH), F32),
            jax.ShapeDtypeStruct((m, SSD_CONV_CH), F32),
            jax.ShapeDtypeStruct((m, SSD_HEADS), F32),
            jax.ShapeDtypeStruct((SSD_HEADS, m), F32),
        ],
        compiler_params=_params(("parallel",)),
        name="even_proj",
    )(x, norm_w.reshape(1, d), w_main, w_dt, w_dtt)


def _sb_neg_tri(n):
    j = jnp.arange(n)[:, None]
    s = jnp.arange(n)[None, :]
    return -(j > s).astype(BF16)


def _softplus2(z2):
    return jnp.log2(1.0 + jnp.exp2(-jnp.abs(z2))) + jnp.maximum(z2, 0.0)


def _sb_tiles(qs, kbs, vbs, ntri, biases2, carries, accs, valid):
    nh, nt = len(qs), len(kbs)
    t = kbs[0].shape[0]
    z2 = [[_dot_nt(qs[h], kbs[n]) + biases2[h] for h in range(nh)] for n in range(nt)]
    sp2 = [[_softplus2(z2[n][h]) for h in range(nh)] for n in range(nt)]
    sp2 = [[s if valid[n] is None else jnp.where(valid[n], s, 0.0) for s in sp2[n]] for n in range(nt)]
    carry_at = []
    for n in range(nt):
        carry_at.append(list(carries))
        carries = [carries[h] - jnp.broadcast_to(jnp.sum(sp2[n][h], axis=1, keepdims=True), carries[h].shape)
                   for h in range(nh)]
    rem2 = [[_dot(sp2[n][h].astype(BF16), ntri) + jnp.concatenate([carry_at[n][h]] * (t // LANES), axis=1)
             for h in range(nh)] for n in range(nt)]
    w = [[jnp.exp2((z2[n][h] - sp2[n][h]) + rem2[n][h]) for h in range(nh)] for n in range(nt)]
    w = [[x if valid[n] is None else jnp.where(valid[n], x, 0.0) for x in w[n]] for n in range(nt)]
    for n in range(nt):
        accs = [accs[h] + _dot(w[n][h].astype(BF16), vbs[n]) for h in range(nh)]
    return carries, accs


def _sb_prompt_kernel(bias_ref, q_ref, k_ref, v_ref, tri_ref, o_ref, c_scr, a_scr):
    p = pl.program_id(1)
    i = pl.program_id(2)
    tq = q_ref.shape[1]
    q = q_ref[0]
    lane = lax.broadcasted_iota(jnp.int32, (tq, LANES), 1)
    first = lane < SB_HEAD_DIM
    qs = (jnp.where(first, q, jnp.zeros_like(q)), jnp.where(first, jnp.zeros_like(q), q))
    biases = (bias_ref[2 * p] * LOG2E, bias_ref[2 * p + 1] * LOG2E)
    tri = tri_ref[...]
    c_scr[...] = jnp.zeros_like(c_scr)
    a_scr[...] = jnp.zeros_like(a_scr)

    def tiles(js, valid):
        kbs = [k_ref[0, pl.ds(pl.multiple_of(j * tq, tq), tq), :] for j in js]
        vbs = [v_ref[0, pl.ds(pl.multiple_of(j * tq, tq), tq), :] for j in js]
        carry, acc = _sb_tiles(qs, kbs, vbs, tri, biases, [c_scr[0], c_scr[1]], [a_scr[0], a_scr[1]], valid)
        for h in range(2):
            c_scr[h] = carry[h]
            a_scr[h] = acc[h]

    row = lax.broadcasted_iota(jnp.int32, (tq, tq), 0)
    col = lax.broadcasted_iota(jnp.int32, (tq, tq), 1)
    causal = col < row

    @pl.when(i == 0)
    def _():
        tiles([0], [causal])

    @pl.when(i > 0)
    def _():
        tiles([i, i - 1], [causal, None])

    rest = jnp.maximum(i - 1, 0)
    odd = rest % 2

    @pl.when(odd == 1)
    def _():
        tiles([i - 2], [None])

    def body(n, _):
        j = i - 2 - odd - 2 * n
        tiles([j, j - 1], [None, None])
        return 0

    lax.fori_loop(0, rest // 2, body, 0)
    o_ref[0] = jnp.where(first, a_scr[0], a_scr[1])


def _sb_prompt(q, k, v, sb_bias, bsz, t):
    npair, m, _ = q.shape
    tq = SB_TILE
    nq = t // tq
    return pl.pallas_call(
        _sb_prompt_kernel,
        grid_spec=pltpu.PrefetchScalarGridSpec(
            num_scalar_prefetch=1,
            grid=(bsz, npair, nq),
            in_specs=[
                pl.BlockSpec((1, tq, LANES), lambda b, p, i, s: (p, b * nq + i, 0)),
                pl.BlockSpec((1, t, LANES), lambda b, p, i, s: (p, b, 0)),
                pl.BlockSpec((1, t, LANES), lambda b, p, i, s: (p, b, 0)),
                pl.BlockSpec((tq, tq), lambda b, p, i, s: (0, 0)),
            ],
            out_specs=pl.BlockSpec((1, tq, LANES), lambda b, p, i, s: (p, b * nq + i, 0)),
            scratch_shapes=[pltpu.VMEM((2, tq, LANES), F32), pltpu.VMEM((2, tq, LANES), F32)],
        ),
        out_shape=jax.ShapeDtypeStruct((npair, m, LANES), F32),
        compiler_params=_params(("parallel", "parallel", "arbitrary")),
        name="sb_prompt",
    )(sb_bias.astype(F32), q, k, v, _sb_neg_tri(tq))


def _sb_decode_kernel(pt_ref, q_ref, bias_ref, *refs, group):
    k_refs, v_refs = refs[0:group], refs[group:2 * group]
    tri_ref, o_ref, c_scr, a_scr = refs[2 * group:]
    s = pl.program_id(0)
    g = pl.program_id(1)
    width = q_ref.shape[2]

    @pl.when(g == 0)
    def _():
        c_scr[...] = jnp.zeros_like(c_scr)
        a_scr[...] = jnp.zeros_like(a_scr)

    head = lax.broadcasted_iota(jnp.int32, (SB_HEADS, width), 0)
    lane = lax.broadcasted_iota(jnp.int32, (SB_HEADS, width), 1)
    own = (lane // SB_HEAD_DIM) == head
    qm = jnp.where(own, jnp.broadcast_to(q_ref[0], (SB_HEADS, width)), 0.0).astype(BF16)
    bias2 = jnp.concatenate([bias_ref[...] * LOG2E] * group, axis=0)
    z2 = jnp.concatenate([_dot(qm, k_refs[j][0].astype(BF16)) for j in range(group)], axis=0) + bias2
    sp2 = _softplus2(z2)
    hi = sp2.astype(BF16)
    lo = (sp2 - hi.astype(F32)).astype(BF16)
    tot = jnp.sum(sp2, axis=1, keepdims=True)
    carries = [c_scr[...]]
    for j in range(group):
        carries.append(carries[j] - tot[j * SB_HEADS:(j + 1) * SB_HEADS, :])
    c_scr[...] = carries[group]
    rem2 = _dot(hi, tri_ref[...]) + _dot(lo, tri_ref[...]) + jnp.concatenate(carries[0:group], axis=0)
    w = jnp.exp2((z2 - sp2) + rem2)
    acc = a_scr[...]
    for j in range(group):
        wj = w[j * SB_HEADS:(j + 1) * SB_HEADS, :].astype(BF16)
        acc = acc + _dot_nt(wj, v_refs[j][0].astype(BF16))
    a_scr[...] = acc

    @pl.when(g == pl.num_programs(1) - 1)
    def _():
        o = jnp.sum(jnp.where(own, acc, 0.0), axis=0, keepdims=True)
        for p in range(width // LANES):
            o_ref[p, pl.ds(s, 1), :] = o[:, p * LANES:(p + 1) * LANES]


def _sb_decode(q, cache_kt, cache_vt, page_table, sb_bias):
    n_seq, _, width = q.shape
    page = cache_kt.shape[2]
    npages = page_table.shape[1]
    group = SB_DECODE_GROUP if npages % SB_DECODE_GROUP == 0 else 1
    tri = _sb_neg_tri(page)

    def kv_spec(jj):
        return pl.BlockSpec((1, width, page), lambda b, g, pt: (pt[b, npages - 1 - (g * group + jj)], 0, 0))

    return pl.pallas_call(
        functools.partial(_sb_decode_kernel, group=group),
        grid_spec=pltpu.PrefetchScalarGridSpec(
            num_scalar_prefetch=1,
            grid=(n_seq, npages // group),
            in_specs=[
                pl.BlockSpec((1, 1, width), lambda b, g, pt: (b, 0, 0)),
                pl.BlockSpec((SB_HEADS, 1), lambda b, g, pt: (0, 0)),
                *[kv_spec(jj) for jj in range(group)],
                *[kv_spec(jj) for jj in range(group)],
                pl.BlockSpec((page, page), lambda b, g, pt: (0, 0)),
            ],
            out_specs=pl.BlockSpec((width // LANES, n_seq, LANES), lambda b, g, pt: (0, 0, 0)),
            scratch_shapes=[pltpu.VMEM((SB_HEADS, 1), F32), pltpu.VMEM((SB_HEADS, width), F32)],
        ),
        out_shape=jax.ShapeDtypeStruct((width // LANES, n_seq, LANES), F32),
        compiler_params=_params(("arbitrary", "arbitrary")),
        name="sb_decode",
    )(page_table, q, sb_bias.astype(F32).reshape(SB_HEADS, 1), *([cache_kt] * group), *([cache_vt] * group), tri)


def _ssd_prompt_kernel(xbc_ref, z_ref, dt_ref, dtt_ref, cw_ref, cb_ref, dtb_ref, dtbt_ref, alog_ref, alogt_ref,
                       dskip_ref, nw_ref, y_ref, conv_ref, ssm_ref, buf, st):
    c = pl.program_id(1)
    nc = pl.num_programs(1)
    L = xbc_ref.shape[0]
    pad = SUBLANES

    @pl.when(c == 0)
    def _():
        buf[0:pad, :] = jnp.zeros((pad, SSD_CONV_CH), F32)
        st[...] = jnp.zeros_like(st)

    buf[pad:pad + L, :] = xbc_ref[...]
    conv = cb_ref[...]
    for j in range(SSD_CONV):
        off = pad - (SSD_CONV - 1) + j
        conv = conv + cw_ref[j:j + 1, :] * buf[off:off + L, :]
    tail = buf[pad + L - (SSD_CONV - 1):pad + L, :]
    buf[pad - (SSD_CONV - 1):pad, :] = tail
    xa = _silu(conv)
    xs = xa[:, 0:SSD_WIDTH]
    gw = SSD_STATE
    bm = [xa[:, SSD_WIDTH + g * gw:SSD_WIDTH + (g + 1) * gw].astype(BF16) for g in range(SSD_GROUPS)]
    cm = [xa[:, SSD_WIDTH + (SSD_GROUPS + g) * gw:SSD_WIDTH + (SSD_GROUPS + g + 1) * gw].astype(BF16)
          for g in range(SSD_GROUPS)]

    dt = _softplus(dt_ref[...] + dtb_ref[...])
    dtt = _softplus(dtt_ref[...] + dtbt_ref[...])
    a = -jnp.exp(alog_ref[...])
    at = -jnp.exp(alogt_ref[...])
    row = lax.broadcasted_iota(jnp.int32, (L, L), 0)
    col = lax.broadcasted_iota(jnp.int32, (L, L), 1)
    lower = col <= row
    lower_b = lower.astype(BF16)
    upper_b = (row <= col).astype(BF16)
    cum = sum(_dot(lower_b, part) for part in _split3(dt * a))
    cumt = sum(_dot(part, upper_b) for part in _split3(dtt * at))
    last = cum[L - 1:L, :]
    wst = jnp.exp(last - cum) * dt
    ecum = jnp.exp(cum)
    elast = jnp.exp(last)

    lane = lax.broadcasted_iota(jnp.int32, (L, LANES), 1)
    first = lane < SSD_HEAD_DIM
    first_row = first[0:1, :]
    heads_per_group = SSD_HEADS // SSD_GROUPS
    cb = [_dot_nt(cm[g], bm[g]) for g in range(SSD_GROUPS)]
    ys = []
    for p in range(SSD_WIDTH // LANES):
        g = (2 * p) // heads_per_group
        xp = xs[:, p * LANES:(p + 1) * LANES]
        xpb = xp.astype(BF16)
        yi = []
        for h in (2 * p, 2 * p + 1):
            seg = cum[:, h:h + 1] - cumt[h:h + 1, :]
            dec = jnp.where(lower, jnp.exp(jnp.where(lower, seg, 0.0)), 0.0) * dtt[h:h + 1, :]
            yi.append(_dot((cb[g] * dec).astype(BF16), xpb))
        y_intra = jnp.where(first, yi[0], yi[1])
        stp = st[:, p * LANES:(p + 1) * LANES]
        ec = jnp.where(first, ecum[:, 2 * p:2 * p + 1], ecum[:, 2 * p + 1:2 * p + 2])
        y_inter = _dot(cm[g], stp.astype(BF16)) * ec
        wp = jnp.where(first, wst[:, 2 * p:2 * p + 1], wst[:, 2 * p + 1:2 * p + 2])
        el = jnp.where(first_row, elast[:, 2 * p:2 * p + 1], elast[:, 2 * p + 1:2 * p + 2])
        st[:, p * LANES:(p + 1) * LANES] = el * stp + _dot_tn(bm[g], (xp * wp).astype(BF16))
        ys.append(y_intra + y_inter + dskip_ref[:, p * LANES:(p + 1) * LANES] * xp)
    y = jnp.concatenate(ys, axis=1)
    y_ref[...] = _rmsnorm(y * _silu(z_ref[...]), nw_ref[...])

    @pl.when(c == nc - 1)
    def _():
        conv_ref[0] = tail
        ssm_ref[0] = st[...].T


def _ssd_prompt(xbc, z, dt, dtt, conv_w, conv_b, dt_bias, a_log, d_skip, ssd_norm, bsz, t):
    m = xbc.shape[0]
    L = SSD_CHUNK
    nc = t // L
    rows = lambda b, c: (b * nc + c, 0)
    full = lambda b, c: (0, 0)
    vec = lambda v: v.astype(F32).reshape(1, -1)
    colv = lambda v: v.astype(F32).reshape(-1, 1)
    y, conv, ssm = pl.pallas_call(
        _ssd_prompt_kernel,
        grid=(bsz, nc),
        in_specs=[
            pl.BlockSpec((L, SSD_CONV_CH), rows),
            pl.BlockSpec((L, SSD_WIDTH), rows),
            pl.BlockSpec((L, SSD_HEADS), rows),
            pl.BlockSpec((SSD_HEADS, L), lambda b, c: (0, b * nc + c)),
            pl.BlockSpec((SSD_CONV, SSD_CONV_CH), full),
            pl.BlockSpec((1, SSD_CONV_CH), full),
            pl.BlockSpec((1, SSD_HEADS), full),
            pl.BlockSpec((SSD_HEADS, 1), full),
            pl.BlockSpec((1, SSD_HEADS), full),
            pl.BlockSpec((SSD_HEADS, 1), full),
            pl.BlockSpec((1, SSD_WIDTH), full),
            pl.BlockSpec((1, SSD_WIDTH), full),
        ],
        out_specs=[
            pl.BlockSpec((L, SSD_WIDTH), rows),
            pl.BlockSpec((1, SSD_CONV - 1, SSD_CONV_CH), lambda b, c: (b, 0, 0)),
            pl.BlockSpec((1, SSD_WIDTH, SSD_STATE), lambda b, c: (b, 0, 0)),
        ],
        out_shape=[
            jax.ShapeDtypeStruct((m, SSD_WIDTH), F32),
            jax.ShapeDtypeStruct((bsz, SSD_CONV - 1, SSD_CONV_CH), F32),
            jax.ShapeDtypeStruct((bsz, SSD_WIDTH, SSD_STATE), F32),
        ],
        scratch_shapes=[pltpu.VMEM((L + SUBLANES, SSD_CONV_CH), F32), pltpu.VMEM((SSD_STATE, SSD_WIDTH), F32)],
        compiler_params=_params(("parallel", "arbitrary")),
        name="ssd_prompt",
    )(xbc, z, dt, dtt, conv_w.astype(F32), vec(conv_b), vec(dt_bias), colv(dt_bias), vec(a_log), colv(a_log),
      vec(jnp.repeat(d_skip, SSD_HEAD_DIM)), vec(ssd_norm))
    return y, conv, ssm.reshape(bsz, SSD_HEADS, SSD_HEAD_DIM, SSD_STATE)


def _ssd_step_kernel(xbc_ref, z_ref, dt_ref, sconv_ref, sssm_ref, cw_ref, cb_ref, dtb_ref, alog_ref, dskip_ref,
                     nw_ref, y_ref, conv_ref, ssm_ref):
    xr = xbc_ref[0]
    cs = sconv_ref[0]
    conv = cb_ref[...] + cw_ref[SSD_CONV - 1:SSD_CONV, :] * xr
    for j in range(SSD_CONV - 1):
        conv = conv + cw_ref[j:j + 1, :] * cs[j:j + 1, :]
    conv_ref[0] = jnp.concatenate([cs[1:SSD_CONV - 1, :], xr], axis=0)
    xa = _silu(conv)
    xs = xa[:, 0:SSD_WIDTH]
    dt = _softplus(dt_ref[0] + dtb_ref[...])
    da = jnp.exp(dt * -jnp.exp(alog_ref[...]))
    lane_head = lax.broadcasted_iota(jnp.int32, (1, SSD_WIDTH), 1) // SSD_HEAD_DIM
    dt_w = jnp.zeros((1, SSD_WIDTH), F32)
    da_w = jnp.zeros((1, SSD_WIDTH), F32)
    for h in range(SSD_HEADS):
        dt_w = jnp.where(lane_head == h, dt[:, h:h + 1], dt_w)
        da_w = jnp.where(lane_head == h, da[:, h:h + 1], da_w)
    dtx_col = _row_to_col(dt_w * xs)
    da_col = _row_to_col(da_w)
    rows_per_group = SSD_WIDTH // SSD_GROUPS
    row = lax.broadcasted_iota(jnp.int32, (SSD_WIDTH, SSD_STATE), 0)
    b_rows = jnp.zeros((SSD_WIDTH, SSD_STATE), F32)
    c_rows = jnp.zeros((SSD_WIDTH, SSD_STATE), F32)
    for g in range(SSD_GROUPS):
        sel = (row // rows_per_group) == g
        b0 = SSD_WIDTH + g * SSD_STATE
        c0 = SSD_WIDTH + (SSD_GROUPS + g) * SSD_STATE
        b_rows = jnp.where(sel, xa[:, b0:b0 + SSD_STATE], b_rows)
        c_rows = jnp.where(sel, xa[:, c0:c0 + SSD_STATE], c_rows)
    new = da_col * sssm_ref[0] + dtx_col * b_rows
    ssm_ref[0] = new
    y = _col_to_row(jnp.sum(new * c_rows, axis=1, keepdims=True)) + dskip_ref[...] * xs
    y_ref[0] = _rmsnorm(y * _silu(z_ref[0]), nw_ref[...])


def _ssd_step(xbc, z, dt, state_conv, state_ssm, conv_w, conv_b, dt_bias, a_log, d_skip, ssd_norm):
    n = xbc.shape[0]
    vec = lambda v: v.astype(F32).reshape(1, -1)
    per = lambda i: (i, 0, 0)
    full = lambda i: (0, 0)
    y, conv, ssm = pl.pallas_call(
        _ssd_step_kernel,
        grid=(n,),
        in_specs=[
            pl.BlockSpec((1, 1, SSD_CONV_CH), per),
            pl.BlockSpec((1, 1, SSD_WIDTH), per),
            pl.BlockSpec((1, 1, SSD_HEADS), per),
            pl.BlockSpec((1, SSD_CONV - 1, SSD_CONV_CH), per),
            pl.BlockSpec((1, SSD_WIDTH, SSD_STATE), per),
            pl.BlockSpec((SSD_CONV, SSD_CONV_CH), full),
            pl.BlockSpec((1, SSD_CONV_CH), full),
            pl.BlockSpec((1, SSD_HEADS), full),
            pl.BlockSpec((1, SSD_HEADS), full),
            pl.BlockSpec((1, SSD_WIDTH), full),
            pl.BlockSpec((1, SSD_WIDTH), full),
        ],
        out_specs=[
            pl.BlockSpec((1, 1, SSD_WIDTH), per),
            pl.BlockSpec((1, SSD_CONV - 1, SSD_CONV_CH), per),
            pl.BlockSpec((1, SSD_WIDTH, SSD_STATE), per),
        ],
        out_shape=[
            jax.ShapeDtypeStruct((n, 1, SSD_WIDTH), F32),
            jax.ShapeDtypeStruct((n, SSD_CONV - 1, SSD_CONV_CH), F32),
            jax.ShapeDtypeStruct((n, SSD_WIDTH, SSD_STATE), F32),
        ],
        compiler_params=_params(("parallel",)),
        name="ssd_step",
    )(xbc.reshape(n, 1, SSD_CONV_CH), z.reshape(n, 1, SSD_WIDTH), dt.reshape(n, 1, SSD_HEADS),
      state_conv.astype(F32), state_ssm.astype(F32).reshape(n, SSD_WIDTH, SSD_STATE), conv_w.astype(F32),
      vec(conv_b), vec(dt_bias), vec(a_log), vec(jnp.repeat(d_skip, SSD_HEAD_DIM)), vec(ssd_norm))
    return y.reshape(n, SSD_WIDTH), conv, ssm.reshape(n, SSD_HEADS, SSD_HEAD_DIM, SSD_STATE)


def _even_out_kernel(osb_ref, y_ref, x_ref, w_ref, o_ref):
    mix = jnp.concatenate([osb_ref[p] for p in range(osb_ref.shape[0])] + [y_ref[...]], axis=1)
    o_ref[...] = x_ref[...] + _dot(mix.astype(BF16), w_ref[...])


def _even_out(osb, y, x, w_out, tm):
    m, d = x.shape
    npair = osb.shape[0]
    rows = lambda i: (i, 0)
    return pl.pallas_call(
        _even_out_kernel,
        grid=(m // tm,),
        in_specs=[
            pl.BlockSpec((npair, tm, LANES), lambda i: (0, i, 0)),
            pl.BlockSpec((tm, SSD_WIDTH), rows),
            pl.BlockSpec((tm, d), rows),
            pl.BlockSpec(w_out.shape, lambda i: (0, 0)),
        ],
        out_specs=pl.BlockSpec((tm, d), rows),
        out_shape=jax.ShapeDtypeStruct((m, d), F32),
        compiler_params=_params(("parallel",)),
        name="even_out",
    )(osb, y, x, w_out.astype(BF16))


def _odd_out_kernel(o_ref, g_ref, x_ref, nw_ref, w_ref, out_ref):
    o = jnp.concatenate([o_ref[h] for h in range(o_ref.shape[0])], axis=1)
    gated = _rmsnorm(o, nw_ref[...]) * _silu(g_ref[...])
    out_ref[...] = x_ref[...] + _dot(gated.astype(BF16), w_ref[...])


def _odd_out(o, g, x, hg_norm, w_out, tm):
    m, d = x.shape
    nh = o.shape[0]
    rows = lambda i: (i, 0)
    return pl.pallas_call(
        _odd_out_kernel,
        grid=(m // tm,),
        in_specs=[
            pl.BlockSpec((nh, tm, HG_VAL), lambda i: (0, i, 0)),
            pl.BlockSpec((tm, nh * HG_VAL), rows),
            pl.BlockSpec((tm, d), rows),
            pl.BlockSpec((1, nh * HG_VAL), lambda i: (0, 0)),
            pl.BlockSpec(w_out.shape, lambda i: (0, 0)),
        ],
        out_specs=pl.BlockSpec((tm, d), rows),
        out_shape=jax.ShapeDtypeStruct((m, d), F32),
        compiler_params=_params(("parallel",)),
        name="odd_out",
    )(o, g, x, hg_norm.astype(F32).reshape(1, -1), w_out.astype(BF16))


def _mlp_kernel(x_ref, nw_ref, w1_ref, w2_ref, fw_ref, o_ref, hn_scr, acc_scr, *, final_norm):
    j = pl.program_id(1)

    @pl.when(j == 0)
    def _():
        hn_scr[...] = _rmsnorm(x_ref[...], nw_ref[...]).astype(BF16)

    h = jnp.maximum(_dot(hn_scr[...], w1_ref[...]), 0.0)
    a = (h * h).astype(BF16)

    @pl.when(j == 0)
    def _():
        acc_scr[...] = _dot(a, w2_ref[...])

    @pl.when(j > 0)
    def _():
        acc_scr[...] += _dot(a, w2_ref[...])

    @pl.when(j == pl.num_programs(1) - 1)
    def _():
        out = x_ref[...] + acc_scr[...]
        o_ref[...] = _rmsnorm(out, fw_ref[...]) if final_norm else out


def _mlp(x, norm_w, w1, w2, final_w, tm, tf):
    m, d = x.shape
    f = w1.shape[1]
    final_norm = final_w is not None
    fw = (final_w if final_norm else jnp.ones((d,), F32)).astype(F32).reshape(1, d)
    return pl.pallas_call(
        functools.partial(_mlp_kernel, final_norm=final_norm),
        grid=(m // tm, f // tf),
        in_specs=[
            pl.BlockSpec((tm, d), lambda i, j: (i, 0)),
            pl.BlockSpec((1, d), lambda i, j: (0, 0)),
            pl.BlockSpec((d, tf), lambda i, j: (0, j)),
            pl.BlockSpec((tf, d), lambda i, j: (j, 0)),
            pl.BlockSpec((1, d), lambda i, j: (0, 0)),
        ],
        out_specs=pl.BlockSpec((tm, d), lambda i, j: (i, 0)),
        out_shape=jax.ShapeDtypeStruct((m, d), F32),
        scratch_shapes=[pltpu.VMEM((tm, d), BF16), pltpu.VMEM((tm, d), F32)],
        compiler_params=_params(("parallel", "arbitrary")),
        name="mlp",
    )(x, norm_w.astype(F32).reshape(1, d), w1.astype(BF16), w2.astype(BF16), fw)


def _odd_proj_kernel(x_ref, nw_ref, w_ref, q_ref, f_ref, v_ref, g_ref):
    hn = _rmsnorm(x_ref[...], nw_ref[...]).astype(BF16)
    width = HG_HEADS * HG_KEY
    per_chunk = PROJ_COLS // HG_KEY
    for i, ref in enumerate((q_ref, f_ref, v_ref)):
        for c in range(width // PROJ_COLS):
            y = _dot(hn, w_ref[:, i * width + c * PROJ_COLS:i * width + (c + 1) * PROJ_COLS])
            for h in range(per_chunk):
                ref[c * per_chunk + h] = y[:, h * HG_KEY:(h + 1) * HG_KEY]
    for c in range(width // PROJ_COLS):
        g_ref[:, c * PROJ_COLS:(c + 1) * PROJ_COLS] = _dot(
            hn, w_ref[:, 3 * width + c * PROJ_COLS:3 * width + (c + 1) * PROJ_COLS])


def _odd_proj(x, norm_w, w_in, tm):
    m, d = x.shape
    width = HG_HEADS * HG_KEY
    heads = lambda i: (0, i, 0)
    per_head = jax.ShapeDtypeStruct((HG_HEADS, m, HG_KEY), F32)
    return pl.pallas_call(
        _odd_proj_kernel,
        grid=(m // tm,),
        in_specs=[
            pl.BlockSpec((tm, d), lambda i: (i, 0)),
            pl.BlockSpec((1, d), lambda i: (0, 0)),
            pl.BlockSpec(w_in.shape, lambda i: (0, 0)),
        ],
        out_specs=[
            pl.BlockSpec((HG_HEADS, tm, HG_KEY), heads),
            pl.BlockSpec((HG_HEADS, tm, HG_KEY), heads),
            pl.BlockSpec((HG_HEADS, tm, HG_KEY), heads),
            pl.BlockSpec((tm, width), lambda i: (i, 0)),
        ],
        out_shape=[per_head, per_head, per_head, jax.ShapeDtypeStruct((m, width), F32)],
        compiler_params=_params(("parallel",)),
        name="odd_proj",
    )(x, norm_w.astype(F32).reshape(1, d), w_in.astype(BF16))


def _hgrn_lower_bound(raw, layer):
    e = jnp.exp(raw - jnp.max(raw, axis=0, keepdims=True))
    p = e / jnp.sum(e, axis=0, keepdims=True)
    lb = jnp.zeros_like(p[0])
    for l in range(1, layer + 1):
        lb = lb + p[l]
    return lb


def _hgrn_gates(fpre, lb):
    sig, nsig = _sigmoid_pair(fpre)
    return lb + (1.0 - lb) * sig, (1.0 - lb) * nsig


def _hgrn_prompt_kernel(q_ref, f_ref, v_ref, lb_ref, o_ref, s_ref, st, b_scr, k_scr, od_scr, *, layer):
    c = pl.program_id(1)
    nc = pl.num_programs(1)
    L = q_ref.shape[1]
    nblk = L // HG_DIAG

    @pl.when(c == 0)
    def _():
        st[...] = jnp.zeros_like(st)

    row = lax.broadcasted_iota(jnp.int32, (L, L), 0)
    col = lax.broadcasted_iota(jnp.int32, (L, L), 1)
    lower_b = (col <= row).astype(BF16)
    pos = lax.broadcasted_iota(jnp.int32, (L, HG_KEY), 0)

    def heads(n, _):
        hs = [n * HG_HEAD_GROUP + e for e in range(HG_HEAD_GROUP)]
        every = range(HG_HEAD_GROUP)
        q = [q_ref[h] for h in hs]
        vb = [v_ref[h].astype(BF16) for h in hs]
        gates = [_hgrn_gates(f_ref[h], _hgrn_lower_bound(lb_ref[:, h], layer)) for h in hs]
        kin = [gk[1] for gk in gates]
        b = [sum(_dot(lower_b, part) for part in _split3(jnp.log(gk[0]))) for gk in gates]
        for e in every:
            b_scr[e] = b[e]
            k_scr[e] = kin[e]
        stt = [st[h] for h in hs]
        o = [_dot_nt((q[e] * jnp.exp(b[e])).astype(BF16), stt[e].astype(BF16)) for e in every]
        b_last = [x[L - 1:L, :] for x in b]
        for e in every:
            st[hs[e]] = stt[e] * jnp.exp(b_last[e]) + _dot_tn(
                vb[e], (kin[e] * jnp.exp(b_last[e] - b[e])).astype(BF16))

        att = [jnp.zeros((L, L), F32) for _ in every]
        g = L
        while g > HG_DIAG:
            half = g // 2
            late = (pos & (g - 1)) >= half
            same = (row ^ col) < g
            mid = [jnp.broadcast_to(x.reshape(L // g, g, HG_KEY)[:, half - 1:half, :],
                                    (L // g, g, HG_KEY)).reshape(L, HG_KEY) for x in b]
            dec = [jnp.exp(-jnp.abs(b[e] - mid[e])) for e in every]
            qd = [jnp.where(late, q[e] * dec[e], 0.0).astype(BF16) for e in every]
            kd = [jnp.where(late, 0.0, kin[e] * dec[e]).astype(BF16) for e in every]
            att = [att[e] + jnp.where(same, _dot_nt(qd[e], kd[e]), 0.0) for e in every]
            g = half
        o = [o[e] + _dot(att[e].astype(BF16), vb[e]) for e in every]

        at = lambda ref, i: ref[pl.ds(i, nblk, stride=HG_DIAG), :]
        for e in every:
            bs = [at(b_scr.at[e], i) for i in range(HG_DIAG)]
            ks = [at(k_scr.at[e], i) for i in range(HG_DIAG)]
            vs = [at(v_ref.at[hs[e]], i) for i in range(HG_DIAG)]
            for i in range(HG_DIAG):
                qi = at(q_ref.at[hs[e]], i)
                acc = jnp.zeros((nblk, HG_VAL), F32)
                for j in range(i + 1):
                    a = jnp.sum(qi * ks[j] * jnp.exp(bs[i] - bs[j]), axis=1, keepdims=True)
                    acc = acc + a * vs[j]
                od_scr[e, pl.ds(i, nblk, stride=HG_DIAG), :] = acc
        for e in every:
            o_ref[hs[e]] = o[e] + od_scr[e]
        return 0

    lax.fori_loop(0, q_ref.shape[0] // HG_HEAD_GROUP, heads, 0)

    @pl.when(c == nc - 1)
    def _():
        for h in range(s_ref.shape[1]):
            s_ref[0, h] = st[h].T


def _hgrn_prompt(q, f, v, lb_raw, layer, bsz, t):
    nh, m, _ = q.shape
    L = HG_CHUNK
    nc = t // L
    blk = lambda b, c: (0, b * nc + c, 0)
    depth = lb_raw.shape[0]
    return pl.pallas_call(
        functools.partial(_hgrn_prompt_kernel, layer=layer),
        grid=(bsz, nc),
        in_specs=[
            pl.BlockSpec((nh, L, HG_KEY), blk),
            pl.BlockSpec((nh, L, HG_KEY), blk),
            pl.BlockSpec((nh, L, HG_VAL), blk),
            pl.BlockSpec((depth, nh, 1, HG_KEY), lambda b, c: (0, 0, 0, 0)),
        ],
        out_specs=[
            pl.BlockSpec((nh, L, HG_VAL), blk),
            pl.BlockSpec((1, nh, HG_KEY, HG_VAL), lambda b, c: (b, 0, 0, 0)),
        ],
        out_shape=[
            jax.ShapeDtypeStruct((nh, m, HG_VAL), F32),
            jax.ShapeDtypeStruct((bsz, nh, HG_KEY, HG_VAL), F32),
        ],
        scratch_shapes=[
            pltpu.VMEM((nh, HG_VAL, HG_KEY), F32),
            pltpu.VMEM((HG_HEAD_GROUP, L, HG_KEY), F32),
            pltpu.VMEM((HG_HEAD_GROUP, L, HG_KEY), F32),
            pltpu.VMEM((HG_HEAD_GROUP, L, HG_VAL), F32),
        ],
        compiler_params=_params(("parallel", "arbitrary")),
        name="hgrn_prompt",
    )(q, f, v, lb_raw.astype(F32).reshape(depth, nh, 1, HG_KEY))


def _hgrn_step_kernel(q_ref, f_ref, v_ref, lb_ref, s_ref, o_ref, snew_ref, *, layer):
    i = pl.program_id(0)
    for h in range(q_ref.shape[0]):
        q = q_ref[h, pl.ds(i, 1), :]
        v = v_ref[h, pl.ds(i, 1), :]
        fg, kin = _hgrn_gates(f_ref[h, pl.ds(i, 1), :], _hgrn_lower_bound(lb_ref[:, h], layer))
        new = _row_to_col(fg) * s_ref[0, h] + _row_to_col(kin) * v
        snew_ref[0, h] = new
        o_ref[h, pl.ds(i, 1), :] = jnp.sum(_row_to_col(q) * new, axis=0, keepdims=True)


def _hgrn_step(q, f, v, lb_raw, state, layer):
    nh, n, _ = q.shape
    depth = lb_raw.shape[0]
    whole = lambda i: (0, 0, 0)
    return pl.pallas_call(
        functools.partial(_hgrn_step_kernel, layer=layer),
        grid=(n,),
        in_specs=[
            pl.BlockSpec((nh, n, HG_KEY), whole),
            pl.BlockSpec((nh, n, HG_KEY), whole),
            pl.BlockSpec((nh, n, HG_VAL), whole),
            pl.BlockSpec((depth, nh, 1, HG_KEY), lambda i: (0, 0, 0, 0)),
            pl.BlockSpec((1, nh, HG_KEY, HG_VAL), lambda i: (i, 0, 0, 0)),
        ],
        out_specs=[
            pl.BlockSpec((nh, n, HG_VAL), whole),
            pl.BlockSpec((1, nh, HG_KEY, HG_VAL), lambda i: (i, 0, 0, 0)),
        ],
        out_shape=[
            jax.ShapeDtypeStruct((nh, n, HG_VAL), F32),
            jax.ShapeDtypeStruct((n, nh, HG_KEY, HG_VAL), F32),
        ],
        compiler_params=_params(("arbitrary",)),
        name="hgrn_step",
    )(q, f, v, lb_raw.astype(F32).reshape(depth, nh, 1, HG_KEY), state.astype(F32))


def _row_tile(m, want):
    return want if m % want == 0 else m


def kernel(x_prompt, x_sample, cache_k, cache_v, page_table, state_conv, state_ssm, state_hgrn, norm_mix, norm_ffn,
           norm_final, w_in_even, sb_bias, conv_w, conv_b, dt_bias, a_log, d_skip, ssd_norm, w_out_even, w_in_odd,
           hg_lb_raw, hg_norm, w_out_odd, w_ff1, w_ff2):
    bsz, t, d = x_prompt.shape
    n_seq = x_sample.shape[0]
    depth = norm_mix.shape[0]
    mp = bsz * t
    hp = x_prompt.reshape(mp, d)
    hs = x_sample.reshape(n_seq, d)
    tmp = _row_tile(mp, 512)
    tmm = _row_tile(mp, 1024)
    tf = 1024
    npair = SB_WIDTH // LANES
    outs = {k: [] for k in ("kp", "vp", "ks", "vs", "cp", "cs", "sp", "ss", "gp", "gs")}

    for layer in range(depth):
        li = layer // 2
        if layer % 2 == 0:
            ssd_w = (conv_w[li], conv_b[li], dt_bias[li], a_log[li], d_skip[li], ssd_norm[li])
            q, k, v, kf, vf, z, xbc, dt, dtt = _even_proj(hp, norm_mix[layer], w_in_even[li], tmp, seq_len=t)
            osb = _sb_prompt(q, k, v, sb_bias[li], bsz, t)
            y, cp, sp = _ssd_prompt(xbc, z, dt, dtt, *ssd_w, bsz, t)
            hp = _even_out(osb, y, hp, w_out_even[li], tmp)
            rows_view = lambda a: jnp.transpose(a.reshape(bsz, SB_HEADS, SB_HEAD_DIM, t), (0, 3, 1, 2))
            outs["kp"].append(rows_view(kf))
            outs["vp"].append(rows_view(vf))
            outs["cp"].append(cp)
            outs["sp"].append(sp)
            q, k, v, kf, vf, z, xbc, dt, dtt = _even_proj(hs, norm_mix[layer], w_in_even[li], n_seq)
            qs = q.astype(F32).transpose(1, 0, 2).reshape(n_seq, 1, SB_WIDTH)
            n_phys, page = cache_k.shape[1], cache_k.shape[2]
            to_lanes = lambda c: jnp.transpose(c, (0, 2, 3, 1)).reshape(n_phys, SB_WIDTH, page)
            osb = _sb_decode(qs, to_lanes(cache_k[li]), to_lanes(cache_v[li]), page_table, sb_bias[li])
            y, cs, ss = _ssd_step(xbc, z, dt, state_conv[li], state_ssm[li], *ssd_w)
            hs = _even_out(osb, y, hs, w_out_even[li], n_seq)
            outs["ks"].append(kf.reshape(n_seq, 1, SB_HEADS, SB_HEAD_DIM))
            outs["vs"].append(vf.reshape(n_seq, 1, SB_HEADS, SB_HEAD_DIM))
            outs["cs"].append(cs)
            outs["ss"].append(ss)
        else:
            q, f, v, g = _odd_proj(hp, norm_mix[layer], w_in_odd[li], tmp)
            o, gp = _hgrn_prompt(q, f, v, hg_lb_raw, layer, bsz, t)
            hp = _odd_out(o, g, hp, hg_norm[li], w_out_odd[li], tmp)
            outs["gp"].append(gp)
            q, f, v, g = _odd_proj(hs, norm_mix[layer], w_in_odd[li], n_seq)
            o, gs = _hgrn_step(q, f, v, hg_lb_raw, state_hgrn[li], layer)
            hs = _odd_out(o, g, hs, hg_norm[li], w_out_odd[li], n_seq)
            outs["gs"].append(gs)
        fw = norm_final if layer == depth - 1 else None
        hp = _mlp(hp, norm_ffn[layer], w_ff1[layer], w_ff2[layer], fw, tmm, tf)
        hs = _mlp(hs, norm_ffn[layer], w_ff1[layer], w_ff2[layer], fw, n_seq, tf)

    y_prompt = hp.reshape(bsz, t, d)
    y_sample = hs.reshape(n_seq, 1, d)
    st = lambda key: jnp.stack(outs[key])
    return (y_prompt, y_sample, st("kp"), st("vp"), st("ks"), st("vs"), st("cp"), st("cs"), st("sp"), st("ss"),
            st("gp"), st("gs"))
```

```python
import functools

import jax
import jax.numpy as jnp
from jax import lax
from jax.experimental import pallas as pl
from jax.experimental.pallas import tpu as pltpu

F32 = jnp.float32
BF16 = jnp.bfloat16

EPS = 1e-6
LOG2E = 1.4426950408889634
LN2 = 0.6931471805599453
SB_HEADS = 8
SB_HEAD_DIM = 64
SB_WIDTH = SB_HEADS * SB_HEAD_DIM
SSD_HEADS = 8
SSD_HEAD_DIM = 64
SSD_WIDTH = SSD_HEADS * SSD_HEAD_DIM
SSD_STATE = 128
SSD_GROUPS = 2
SSD_CONV = 4
SSD_CONV_CH = SSD_WIDTH + 2 * SSD_GROUPS * SSD_STATE
HG_HEADS = 8
HG_KEY = 128
HG_VAL = 128

LANES = 128
SUBLANES = 8
VMEM_LIMIT_BYTES = 52 * 1024 * 1024

SB_TILE = 256
SB_DECODE_GROUP = 16
SSD_CHUNK = 128
HG_CHUNK = 128
HG_DIAG = 4
HG_HEAD_GROUP = 8
PROJ_COLS = 512


def _params(sem):
    return pltpu.CompilerParams(dimension_semantics=sem, vmem_limit_bytes=VMEM_LIMIT_BYTES)


def _dot(a, b):
    return jnp.dot(a, b, preferred_element_type=F32)


def _dot_nt(a, b):
    return lax.dot_general(a, b, (((1,), (1,)), ((), ())), preferred_element_type=F32)


def _dot_tn(a, b):
    return lax.dot_general(a, b, (((0,), (0,)), ((), ())), preferred_element_type=F32)


def _split3(x):
    hi = x.astype(BF16)
    r = x - hi.astype(F32)
    mid = r.astype(BF16)
    lo = (r - mid.astype(F32)).astype(BF16)
    return hi, mid, lo


def _rmsnorm(x, w):
    return x * lax.rsqrt(jnp.mean(x * x, axis=-1, keepdims=True) + EPS) * w


def _softplus(x):
    return jnp.maximum(x, 0.0) + jnp.log1p(jnp.exp(-jnp.abs(x)))


def _sigmoid_pair(x):
    e = jnp.exp(-jnp.abs(x))
    r = 1.0 / (1.0 + e)
    big, small = r, e * r
    pos = x >= 0
    return jnp.where(pos, big, small), jnp.where(pos, small, big)


def _silu(x):
    return x / (1.0 + jnp.exp(-x))


def _row_to_col(row):
    n = row.shape[1]
    eye = lax.broadcasted_iota(jnp.int32, (n, n), 0) == lax.broadcasted_iota(jnp.int32, (n, n), 1)
    return jnp.sum(jnp.where(eye, jnp.broadcast_to(row, (n, n)), 0.0), axis=1, keepdims=True)


def _col_to_row(col):
    n = col.shape[0]
    eye = lax.broadcasted_iota(jnp.int32, (n, n), 0) == lax.broadcasted_iota(jnp.int32, (n, n), 1)
    return jnp.sum(jnp.where(eye, jnp.broadcast_to(col, (n, n)), 0.0), axis=0, keepdims=True)


def _even_proj_kernel(x_ref, nw_ref, w_ref, wdt_ref, wdtt_ref,
                      q_ref, k_ref, v_ref, kf_ref, vf_ref, z_ref, xbc_ref, dt_ref, dtt_ref, *, token_minor):
    hn = _rmsnorm(x_ref[...], nw_ref[...]).astype(BF16)
    scale = LOG2E * SB_HEAD_DIM ** -0.5
    npair = SB_WIDTH // LANES
    q = _dot(hn, w_ref[:, 0:SB_WIDTH]) * scale
    for p in range(npair):
        q_ref[p] = q[:, p * LANES:(p + 1) * LANES].astype(BF16)
    for i, (pair_ref, full_ref) in enumerate(((k_ref, kf_ref), (v_ref, vf_ref))):
        y = _dot(hn, w_ref[:, (i + 1) * SB_WIDTH:(i + 2) * SB_WIDTH])
        if token_minor:
            full_ref[0] = y.T
        else:
            full_ref[...] = y
        for p in range(npair):
            pair_ref[p] = y[:, p * LANES:(p + 1) * LANES].astype(BF16)
    z0 = 3 * SB_WIDTH
    z_ref[...] = _dot(hn, w_ref[:, z0:z0 + SSD_WIDTH]).astype(z_ref.dtype)
    x0 = z0 + SSD_WIDTH
    for c in range(SSD_CONV_CH // PROJ_COLS):
        xbc_ref[:, c * PROJ_COLS:(c + 1) * PROJ_COLS] = _dot(
            hn, w_ref[:, x0 + c * PROJ_COLS:x0 + (c + 1) * PROJ_COLS])
    dt_ref[...] = _dot(hn, wdt_ref[...])[:, 0:SSD_HEADS]
    dtt_ref[...] = _dot_nt(wdtt_ref[...], hn)


def _even_proj(x, norm_w, w_in, tm, seq_len=None):
    m, d = x.shape
    npair = SB_WIDTH // LANES
    wb = w_in.astype(BF16)
    n_main = 3 * SB_WIDTH + SSD_WIDTH + SSD_CONV_CH
    w_main = wb[:, :n_main]
    w_dt = jnp.pad(wb[:, n_main:], ((0, 0), (0, LANES - SSD_HEADS)))
    w_dtt = wb[:, n_main:].T
    full = lambda i: (0, 0)
    rows = lambda i: (i, 0)
    hp = lambda i: (0, i, 0)
    token_minor = seq_len is not None
    if token_minor:
        nt = seq_len // tm
        kv_spec = pl.BlockSpec((1, SB_WIDTH, tm), lambda i: (i // nt, 0, i % nt))
        kv_shape = jax.ShapeDtypeStruct((m // seq_len, SB_WIDTH, seq_len), F32)
    else:
        kv_spec = pl.BlockSpec((tm, SB_WIDTH), rows)
        kv_shape = jax.ShapeDtypeStruct((m, SB_WIDTH), F32)
    return pl.pallas_call(
        functools.partial(_even_proj_kernel, token_minor=token_minor),
        grid=(m // tm,),
        in_specs=[
            pl.BlockSpec((tm, d), rows),
            pl.BlockSpec((1, d), full),
            pl.BlockSpec((d, n_main), full),
            pl.BlockSpec((d, LANES), full),
            pl.BlockSpec((SSD_HEADS, d), full),
        ],
        out_specs=[
            pl.BlockSpec((npair, tm, LANES), hp),
            pl.BlockSpec((npair, tm, LANES), hp),
            pl.BlockSpec((npair, tm, LANES), hp),
            kv_spec,
            kv_spec,
            pl.BlockSpec((tm, SSD_WIDTH), rows),
            pl.BlockSpec((tm, SSD_CONV_CH), rows),
            pl.BlockSpec((tm, SSD_HEADS), rows),
            pl.BlockSpec((SSD_HEADS, tm), lambda i: (0, i)),
        ],
        out_shape=[
            jax.ShapeDtypeStruct((npair, m, LANES), BF16),
            jax.ShapeDtypeStruct((npair, m, LANES), BF16),
            jax.ShapeDtypeStruct((npair, m, LANES), BF16),
            kv_shape,
            kv_shape,
            jax.ShapeDtypeStruct((m, SSD_WIDTH), BF16),
            jax.ShapeDtypeStruct((m, SSD_CONV_CH), F32),
            jax.ShapeDtypeStruct((m, SSD_HEADS), F32),
            jax.ShapeDtypeStruct((SSD_HEADS, m), F32),
        ],
        compiler_params=_params(("parallel",)),
        name="even_proj",
    )(x, norm_w.reshape(1, d), w_main, w_dt, w_dtt)


def _sb_neg_tri(n):
    j = jnp.arange(n)[:, None]
    s = jnp.arange(n)[None, :]
    return -(j > s).astype(BF16)


def _softplus2(z2):
    return jnp.log2(1.0 + jnp.exp2(-jnp.abs(z2))) + jnp.maximum(z2, 0.0)


def _sb_tiles(qs, kbs, vbs, ntri, biases2, carries, accs, valid):
    nh, nt = len(qs), len(kbs)
    t = kbs[0].shape[0]
    z2 = [[_dot_nt(qs[h], kbs[n]) + biases2[h] for h in range(nh)] for n in range(nt)]
    sp2 = [[_softplus2(z2[n][h]) for h in range(nh)] for n in range(nt)]
    sp2 = [[s if valid[n] is None else jnp.where(valid[n], s, 0.0) for s in sp2[n]] for n in range(nt)]
    carry_at = []
    for n in range(nt):
        carry_at.append(list(carries))
        carries = [carries[h] - jnp.broadcast_to(jnp.sum(sp2[n][h], axis=1, keepdims=True), carries[h].shape)
                   for h in range(nh)]
    rem2 = [[_dot(sp2[n][h].astype(BF16), ntri) + jnp.concatenate([carry_at[n][h]] * (t // LANES), axis=1)
             for h in range(nh)] for n in range(nt)]
    w = [[jnp.exp2((z2[n][h] - sp2[n][h]) + rem2[n][h]) for h in range(nh)] for n in range(nt)]
    w = [[x if valid[n] is None else jnp.where(valid[n], x, 0.0) for x in w[n]] for n in range(nt)]
    for n in range(nt):
        accs = [accs[h] + _dot(w[n][h].astype(BF16), vbs[n]) for h in range(nh)]
    return carries, accs


def _sb_prompt_kernel(bias_ref, q_ref, k_ref, v_ref, tri_ref, o_ref, c_scr, a_scr):
    p = pl.program_id(1)
    i = pl.program_id(2)
    tq = q_ref.shape[1]
    q = q_ref[0]
    lane = lax.broadcasted_iota(jnp.int32, (tq, LANES), 1)
    first = lane < SB_HEAD_DIM
    qs = (jnp.where(first, q, jnp.zeros_like(q)), jnp.where(first, jnp.zeros_like(q), q))
    biases = (bias_ref[2 * p] * LOG2E, bias_ref[2 * p + 1] * LOG2E)
    tri = tri_ref[...]
    c_scr[...] = jnp.zeros_like(c_scr)
    a_scr[...] = jnp.zeros_like(a_scr)

    def tiles(js, valid):
        kbs = [k_ref[0, pl.ds(pl.multiple_of(j * tq, tq), tq), :] for j in js]
        vbs = [v_ref[0, pl.ds(pl.multiple_of(j * tq, tq), tq), :] for j in js]
        carry, acc = _sb_tiles(qs, kbs, vbs, tri, biases, [c_scr[0], c_scr[1]], [a_scr[0], a_scr[1]], valid)
        for h in range(2):
            c_scr[h] = carry[h]
            a_scr[h] = acc[h]

    row = lax.broadcasted_iota(jnp.int32, (tq, tq), 0)
    col = lax.broadcasted_iota(jnp.int32, (tq, tq), 1)
    causal = col < row

    @pl.when(i == 0)
    def _():
        tiles([0], [causal])

    @pl.when(i > 0)
    def _():
        tiles([i, i - 1], [causal, None])

    rest = jnp.maximum(i - 1, 0)
    odd = rest % 2

    @pl.when(odd == 1)
    def _():
        tiles([i - 2], [None])

    def body(n, _):
        j = i - 2 - odd - 2 * n
        tiles([j, j - 1], [None, None])
        return 0

    lax.fori_loop(0, rest // 2, body, 0)
    o_ref[0] = jnp.where(first, a_scr[0], a_scr[1]).astype(o_ref.dtype)


def _sb_prompt(q, k, v, sb_bias, bsz, t):
    npair, m, _ = q.shape
    tq = SB_TILE
    nq = t // tq
    return pl.pallas_call(
        _sb_prompt_kernel,
        grid_spec=pltpu.PrefetchScalarGridSpec(
            num_scalar_prefetch=1,
            grid=(bsz, npair, nq),
            in_specs=[
                pl.BlockSpec((1, tq, LANES), lambda b, p, i, s: (p, b * nq + i, 0)),
                pl.BlockSpec((1, t, LANES), lambda b, p, i, s: (p, b, 0)),
                pl.BlockSpec((1, t, LANES), lambda b, p, i, s: (p, b, 0)),
                pl.BlockSpec((tq, tq), lambda b, p, i, s: (0, 0)),
            ],
            out_specs=pl.BlockSpec((1, tq, LANES), lambda b, p, i, s: (p, b * nq + i, 0)),
            scratch_shapes=[pltpu.VMEM((2, tq, LANES), F32), pltpu.VMEM((2, tq, LANES), F32)],
        ),
        out_shape=jax.ShapeDtypeStruct((npair, m, LANES), BF16),
        compiler_params=_params(("parallel", "parallel", "arbitrary")),
        name="sb_prompt",
    )(sb_bias.astype(F32), q, k, v, _sb_neg_tri(tq))


def _sb_decode_kernel(pt_ref, q_ref, bias_ref, *refs, group):
    k_refs, v_refs = refs[0:group], refs[group:2 * group]
    tri_ref, o_ref, c_scr, a_scr = refs[2 * group:]
    s = pl.program_id(0)
    g = pl.program_id(1)
    width = q_ref.shape[2]

    @pl.when(g == 0)
    def _():
        c_scr[...] = jnp.zeros_like(c_scr)
        a_scr[...] = jnp.zeros_like(a_scr)

    head = lax.broadcasted_iota(jnp.int32, (SB_HEADS, width), 0)
    lane = lax.broadcasted_iota(jnp.int32, (SB_HEADS, width), 1)
    own = (lane // SB_HEAD_DIM) == head
    qm = jnp.where(own, jnp.broadcast_to(q_ref[0], (SB_HEADS, width)), 0.0).astype(BF16)
    bias2 = jnp.concatenate([bias_ref[...] * LOG2E] * group, axis=0)
    z2 = jnp.concatenate([_dot(qm, k_refs[j][0].astype(BF16)) for j in range(group)], axis=0) + bias2
    sp2 = _softplus2(z2)
    hi = sp2.astype(BF16)
    lo = (sp2 - hi.astype(F32)).astype(BF16)
    tot = jnp.sum(sp2, axis=1, keepdims=True)
    carries = [c_scr[...]]
    for j in range(group):
        carries.append(carries[j] - tot[j * SB_HEADS:(j + 1) * SB_HEADS, :])
    c_scr[...] = carries[group]
    rem2 = _dot(hi, tri_ref[...]) + _dot(lo, tri_ref[...]) + jnp.concatenate(carries[0:group], axis=0)
    w = jnp.exp2((z2 - sp2) + rem2)
    acc = a_scr[...]
    for j in range(group):
        wj = w[j * SB_HEADS:(j + 1) * SB_HEADS, :].astype(BF16)
        acc = acc + _dot_nt(wj, v_refs[j][0].astype(BF16))
    a_scr[...] = acc

    @pl.when(g == pl.num_programs(1) - 1)
    def _():
        o = jnp.sum(jnp.where(own, acc, 0.0), axis=0, keepdims=True)
        for p in range(width // LANES):
            o_ref[p, pl.ds(s, 1), :] = o[:, p * LANES:(p + 1) * LANES]


def _sb_decode(q, cache_kt, cache_vt, page_table, sb_bias):
    n_seq, _, width = q.shape
    page = cache_kt.shape[2]
    npages = page_table.shape[1]
    group = SB_DECODE_GROUP if npages % SB_DECODE_GROUP == 0 else 1
    tri = _sb_neg_tri(page)

    def kv_spec(jj):
        return pl.BlockSpec((1, width, page), lambda b, g, pt: (pt[b, npages - 1 - (g * group + jj)], 0, 0))

    return pl.pallas_call(
        functools.partial(_sb_decode_kernel, group=group),
        grid_spec=pltpu.PrefetchScalarGridSpec(
            num_scalar_prefetch=1,
            grid=(n_seq, npages // group),
            in_specs=[
                pl.BlockSpec((1, 1, width), lambda b, g, pt: (b, 0, 0)),
                pl.BlockSpec((SB_HEADS, 1), lambda b, g, pt: (0, 0)),
                *[kv_spec(jj) for jj in range(group)],
                *[kv_spec(jj) for jj in range(group)],
                pl.BlockSpec((page, page), lambda b, g, pt: (0, 0)),
            ],
            out_specs=pl.BlockSpec((width // LANES, n_seq, LANES), lambda b, g, pt: (0, 0, 0)),
            scratch_shapes=[pltpu.VMEM((SB_HEADS, 1), F32), pltpu.VMEM((SB_HEADS, width), F32)],
        ),
        out_shape=jax.ShapeDtypeStruct((width // LANES, n_seq, LANES), F32),
        compiler_params=_params(("arbitrary", "arbitrary")),
        name="sb_decode",
    )(page_table, q, sb_bias.astype(F32).reshape(SB_HEADS, 1), *([cache_kt] * group), *([cache_vt] * group), tri)


def _ssd_prompt_kernel(xbc_ref, z_ref, dt_ref, dtt_ref, cw_ref, cb_ref, dtb_ref, dtbt_ref, alog_ref, alogt_ref,
                       dskip_ref, nw_ref, y_ref, conv_ref, ssm_ref, buf, st):
    c = pl.program_id(1)
    nc = pl.num_programs(1)
    L = xbc_ref.shape[0]
    pad = SUBLANES

    @pl.when(c == 0)
    def _():
        buf[0:pad, :] = jnp.zeros((pad, SSD_CONV_CH), F32)
        st[...] = jnp.zeros_like(st)

    buf[pad:pad + L, :] = xbc_ref[...]
    conv = cb_ref[...]
    for j in range(SSD_CONV):
        off = pad - (SSD_CONV - 1) + j
        conv = conv + cw_ref[j:j + 1, :] * buf[off:off + L, :]
    tail = buf[pad + L - (SSD_CONV - 1):pad + L, :]
    buf[pad - (SSD_CONV - 1):pad, :] = tail
    xa = _silu(conv)
    xs = xa[:, 0:SSD_WIDTH]
    gw = SSD_STATE
    bm = [xa[:, SSD_WIDTH + g * gw:SSD_WIDTH + (g + 1) * gw].astype(BF16) for g in range(SSD_GROUPS)]
    cm = [xa[:, SSD_WIDTH + (SSD_GROUPS + g) * gw:SSD_WIDTH + (SSD_GROUPS + g + 1) * gw].astype(BF16)
          for g in range(SSD_GROUPS)]

    dt = _softplus(dt_ref[...] + dtb_ref[...])
    dtt = _softplus(dtt_ref[...] + dtbt_ref[...])
    a = -jnp.exp(alog_ref[...])
    at = -jnp.exp(alogt_ref[...])
    row = lax.broadcasted_iota(jnp.int32, (L, L), 0)
    col = lax.broadcasted_iota(jnp.int32, (L, L), 1)
    lower = col <= row
    lower_b = lower.astype(BF16)
    upper_b = (row <= col).astype(BF16)
    cum = sum(_dot(lower_b, part) for part in _split3(dt * a))
    cumt = sum(_dot(part, upper_b) for part in _split3(dtt * at))
    last = cum[L - 1:L, :]
    wst = jnp.exp(last - cum) * dt
    ecum = jnp.exp(cum)
    elast = jnp.exp(last)

    lane = lax.broadcasted_iota(jnp.int32, (L, LANES), 1)
    first = lane < SSD_HEAD_DIM
    first_row = first[0:1, :]
    heads_per_group = SSD_HEADS // SSD_GROUPS
    cb = [jnp.where(lower, _dot_nt(cm[g], bm[g]), 0.0) for g in range(SSD_GROUPS)]
    ys = []
    for p in range(SSD_WIDTH // LANES):
        g = (2 * p) // heads_per_group
        xp = xs[:, p * LANES:(p + 1) * LANES]
        xpb = xp.astype(BF16)
        yi = []
        for h in (2 * p, 2 * p + 1):
            seg = cum[:, h:h + 1] - cumt[h:h + 1, :]
            dec = jnp.exp(jnp.minimum(seg, 0.0)) * dtt[h:h + 1, :]
            yi.append(_dot((cb[g] * dec).astype(BF16), xpb))
        y_intra = jnp.where(first, yi[0], yi[1])
        stp = st[:, p * LANES:(p + 1) * LANES]
        ec = jnp.where(first, ecum[:, 2 * p:2 * p + 1], ecum[:, 2 * p + 1:2 * p + 2])
        y_inter = _dot(cm[g], stp.astype(BF16)) * ec
        wp = jnp.where(first, wst[:, 2 * p:2 * p + 1], wst[:, 2 * p + 1:2 * p + 2])
        el = jnp.where(first_row, elast[:, 2 * p:2 * p + 1], elast[:, 2 * p + 1:2 * p + 2])
        st[:, p * LANES:(p + 1) * LANES] = el * stp + _dot_tn(bm[g], (xp * wp).astype(BF16))
        ys.append(y_intra + y_inter + dskip_ref[:, p * LANES:(p + 1) * LANES] * xp)
    y = jnp.concatenate(ys, axis=1)
    y_ref[...] = _rmsnorm(y * _silu(z_ref[...].astype(F32)), nw_ref[...]).astype(y_ref.dtype)

    @pl.when(c == nc - 1)
    def _():
        conv_ref[0] = tail
        ssm_ref[0] = st[...].T


def _ssd_prompt(xbc, z, dt, dtt, conv_w, conv_b, dt_bias, a_log, d_skip, ssd_norm, bsz, t):
    m = xbc.shape[0]
    L = SSD_CHUNK
    nc = t // L
    rows = lambda b, c: (b * nc + c, 0)
    full = lambda b, c: (0, 0)
    vec = lambda v: v.astype(F32).reshape(1, -1)
    colv = lambda v: v.astype(F32).reshape(-1, 1)
    y, conv, ssm = pl.pallas_call(
        _ssd_prompt_kernel,
        grid=(bsz, nc),
        in_specs=[
            pl.BlockSpec((L, SSD_CONV_CH), rows),
            pl.BlockSpec((L, SSD_WIDTH), rows),
            pl.BlockSpec((L, SSD_HEADS), rows),
            pl.BlockSpec((SSD_HEADS, L), lambda b, c: (0, b * nc + c)),
            pl.BlockSpec((SSD_CONV, SSD_CONV_CH), full),
            pl.BlockSpec((1, SSD_CONV_CH), full),
            pl.BlockSpec((1, SSD_HEADS), full),
            pl.BlockSpec((SSD_HEADS, 1), full),
            pl.BlockSpec((1, SSD_HEADS), full),
            pl.BlockSpec((SSD_HEADS, 1), full),
            pl.BlockSpec((1, SSD_WIDTH), full),
            pl.BlockSpec((1, SSD_WIDTH), full),
        ],
        out_specs=[
            pl.BlockSpec((L, SSD_WIDTH), rows),
            pl.BlockSpec((1, SSD_CONV - 1, SSD_CONV_CH), lambda b, c: (b, 0, 0)),
            pl.BlockSpec((1, SSD_WIDTH, SSD_STATE), lambda b, c: (b, 0, 0)),
        ],
        out_shape=[
            jax.ShapeDtypeStruct((m, SSD_WIDTH), BF16),
            jax.ShapeDtypeStruct((bsz, SSD_CONV - 1, SSD_CONV_CH), F32),
            jax.ShapeDtypeStruct((bsz, SSD_WIDTH, SSD_STATE), F32),
        ],
        scratch_shapes=[pltpu.VMEM((L + SUBLANES, SSD_CONV_CH), F32), pltpu.VMEM((SSD_STATE, SSD_WIDTH), F32)],
        compiler_params=_params(("parallel", "arbitrary")),
        name="ssd_prompt",
    )(xbc, z, dt, dtt, conv_w.astype(F32), vec(conv_b), vec(dt_bias), colv(dt_bias), vec(a_log), colv(a_log),
      vec(jnp.repeat(d_skip, SSD_HEAD_DIM)), vec(ssd_norm))
    return y, conv, ssm.reshape(bsz, SSD_HEADS, SSD_HEAD_DIM, SSD_STATE)


def _ssd_step_kernel(xbc_ref, z_ref, dt_ref, sconv_ref, sssm_ref, cw_ref, cb_ref, dtb_ref, alog_ref, dskip_ref,
                     nw_ref, y_ref, conv_ref, ssm_ref):
    xr = xbc_ref[0]
    cs = sconv_ref[0]
    conv = cb_ref[...] + cw_ref[SSD_CONV - 1:SSD_CONV, :] * xr
    for j in range(SSD_CONV - 1):
        conv = conv + cw_ref[j:j + 1, :] * cs[j:j + 1, :]
    conv_ref[0] = jnp.concatenate([cs[1:SSD_CONV - 1, :], xr], axis=0)
    xa = _silu(conv)
    xs = xa[:, 0:SSD_WIDTH]
    dt = _softplus(dt_ref[0] + dtb_ref[...])
    da = jnp.exp(dt * -jnp.exp(alog_ref[...]))
    lane_head = lax.broadcasted_iota(jnp.int32, (1, SSD_WIDTH), 1) // SSD_HEAD_DIM
    dt_w = jnp.zeros((1, SSD_WIDTH), F32)
    da_w = jnp.zeros((1, SSD_WIDTH), F32)
    for h in range(SSD_HEADS):
        dt_w = jnp.where(lane_head == h, dt[:, h:h + 1], dt_w)
        da_w = jnp.where(lane_head == h, da[:, h:h + 1], da_w)
    dtx_col = _row_to_col(dt_w * xs)
    da_col = _row_to_col(da_w)
    rows_per_group = SSD_WIDTH // SSD_GROUPS
    row = lax.broadcasted_iota(jnp.int32, (SSD_WIDTH, SSD_STATE), 0)
    b_rows = jnp.zeros((SSD_WIDTH, SSD_STATE), F32)
    c_rows = jnp.zeros((SSD_WIDTH, SSD_STATE), F32)
    for g in range(SSD_GROUPS):
        sel = (row // rows_per_group) == g
        b0 = SSD_WIDTH + g * SSD_STATE
        c0 = SSD_WIDTH + (SSD_GROUPS + g) * SSD_STATE
        b_rows = jnp.where(sel, xa[:, b0:b0 + SSD_STATE], b_rows)
        c_rows = jnp.where(sel, xa[:, c0:c0 + SSD_STATE], c_rows)
    new = da_col * sssm_ref[0] + dtx_col * b_rows
    ssm_ref[0] = new
    y = _col_to_row(jnp.sum(new * c_rows, axis=1, keepdims=True)) + dskip_ref[...] * xs
    y_ref[0] = _rmsnorm(y * _silu(z_ref[0].astype(F32)), nw_ref[...])


def _ssd_step(xbc, z, dt, state_conv, state_ssm, conv_w, conv_b, dt_bias, a_log, d_skip, ssd_norm):
    n = xbc.shape[0]
    vec = lambda v: v.astype(F32).reshape(1, -1)
    per = lambda i: (i, 0, 0)
    full = lambda i: (0, 0)
    y, conv, ssm = pl.pallas_call(
        _ssd_step_kernel,
        grid=(n,),
        in_specs=[
            pl.BlockSpec((1, 1, SSD_CONV_CH), per),
            pl.BlockSpec((1, 1, SSD_WIDTH), per),
            pl.BlockSpec((1, 1, SSD_HEADS), per),
            pl.BlockSpec((1, SSD_CONV - 1, SSD_CONV_CH), per),
            pl.BlockSpec((1, SSD_WIDTH, SSD_STATE), per),
            pl.BlockSpec((SSD_CONV, SSD_CONV_CH), full),
            pl.BlockSpec((1, SSD_CONV_CH), full),
            pl.BlockSpec((1, SSD_HEADS), full),
            pl.BlockSpec((1, SSD_HEADS), full),
            pl.BlockSpec((1, SSD_WIDTH), full),
            pl.BlockSpec((1, SSD_WIDTH), full),
        ],
        out_specs=[
            pl.BlockSpec((1, 1, SSD_WIDTH), per),
            pl.BlockSpec((1, SSD_CONV - 1, SSD_CONV_CH), per),
            pl.BlockSpec((1, SSD_WIDTH, SSD_STATE), per),
        ],
        out_shape=[
            jax.ShapeDtypeStruct((n, 1, SSD_WIDTH), F32),
            jax.ShapeDtypeStruct((n, SSD_CONV - 1, SSD_CONV_CH), F32),
            jax.ShapeDtypeStruct((n, SSD_WIDTH, SSD_STATE), F32),
        ],
        compiler_params=_params(("parallel",)),
        name="ssd_step",
    )(xbc.reshape(n, 1, SSD_CONV_CH), z.reshape(n, 1, SSD_WIDTH), dt.reshape(n, 1, SSD_HEADS),
      state_conv.astype(F32), state_ssm.astype(F32).reshape(n, SSD_WIDTH, SSD_STATE), conv_w.astype(F32),
      vec(conv_b), vec(dt_bias), vec(a_log), vec(jnp.repeat(d_skip, SSD_HEAD_DIM)), vec(ssd_norm))
    return y.reshape(n, SSD_WIDTH), conv, ssm.reshape(n, SSD_HEADS, SSD_HEAD_DIM, SSD_STATE)


def _even_out_kernel(osb_ref, y_ref, x_ref, w_ref, o_ref):
    mix = jnp.concatenate([osb_ref[p] for p in range(osb_ref.shape[0])] + [y_ref[...]], axis=1)
    o_ref[...] = x_ref[...] + _dot(mix, w_ref[...])


def _even_out(osb, y, x, w_out, tm):
    m, d = x.shape
    npair = osb.shape[0]
    rows = lambda i: (i, 0)
    return pl.pallas_call(
        _even_out_kernel,
        grid=(m // tm,),
        in_specs=[
            pl.BlockSpec((npair, tm, LANES), lambda i: (0, i, 0)),
            pl.BlockSpec((tm, SSD_WIDTH), rows),
            pl.BlockSpec((tm, d), rows),
            pl.BlockSpec(w_out.shape, lambda i: (0, 0)),
        ],
        out_specs=pl.BlockSpec((tm, d), rows),
        out_shape=jax.ShapeDtypeStruct((m, d), F32),
        compiler_params=_params(("parallel",)),
        name="even_out",
    )(osb, y, x, w_out.astype(BF16))


def _odd_out_kernel(o_ref, g_ref, x_ref, nw_ref, w_ref, out_ref):
    o = jnp.concatenate([o_ref[h] for h in range(o_ref.shape[0])], axis=1).astype(F32)
    gated = _rmsnorm(o, nw_ref[...]) * _silu(g_ref[...].astype(F32))
    out_ref[...] = x_ref[...] + _dot(gated.astype(BF16), w_ref[...])


def _odd_out(o, g, x, hg_norm, w_out, tm):
    m, d = x.shape
    nh = o.shape[0]
    rows = lambda i: (i, 0)
    return pl.pallas_call(
        _odd_out_kernel,
        grid=(m // tm,),
        in_specs=[
            pl.BlockSpec((nh, tm, HG_VAL), lambda i: (0, i, 0)),
            pl.BlockSpec((tm, nh * HG_VAL), rows),
            pl.BlockSpec((tm, d), rows),
            pl.BlockSpec((1, nh * HG_VAL), lambda i: (0, 0)),
            pl.BlockSpec(w_out.shape, lambda i: (0, 0)),
        ],
        out_specs=pl.BlockSpec((tm, d), rows),
        out_shape=jax.ShapeDtypeStruct((m, d), F32),
        compiler_params=_params(("parallel",)),
        name="odd_out",
    )(o, g, x, hg_norm.astype(F32).reshape(1, -1), w_out.astype(BF16))


def _mlp_kernel(x_ref, nw_ref, w1_ref, w2_ref, fw_ref, o_ref, hn_scr, acc_scr, *, final_norm):
    j = pl.program_id(1)

    @pl.when(j == 0)
    def _():
        hn_scr[...] = _rmsnorm(x_ref[...], nw_ref[...]).astype(BF16)

    h = jnp.maximum(_dot(hn_scr[...], w1_ref[...]), 0.0)
    a = (h * h).astype(BF16)

    @pl.when(j == 0)
    def _():
        acc_scr[...] = _dot(a, w2_ref[...])

    @pl.when(j > 0)
    def _():
        acc_scr[...] += _dot(a, w2_ref[...])

    @pl.when(j == pl.num_programs(1) - 1)
    def _():
        out = x_ref[...] + acc_scr[...]
        o_ref[...] = _rmsnorm(out, fw_ref[...]) if final_norm else out


def _mlp(x, norm_w, w1, w2, final_w, tm, tf):
    m, d = x.shape
    f = w1.shape[1]
    final_norm = final_w is not None
    fw = (final_w if final_norm else jnp.ones((d,), F32)).astype(F32).reshape(1, d)
    return pl.pallas_call(
        functools.partial(_mlp_kernel, final_norm=final_norm),
        grid=(m // tm, f // tf),
        in_specs=[
            pl.BlockSpec((tm, d), lambda i, j: (i, 0)),
            pl.BlockSpec((1, d), lambda i, j: (0, 0)),
            pl.BlockSpec((d, tf), lambda i, j: (0, j)),
            pl.BlockSpec((tf, d), lambda i, j: (j, 0)),
            pl.BlockSpec((1, d), lambda i, j: (0, 0)),
        ],
        out_specs=pl.BlockSpec((tm, d), lambda i, j: (i, 0)),
        out_shape=jax.ShapeDtypeStruct((m, d), F32),
        scratch_shapes=[pltpu.VMEM((tm, d), BF16), pltpu.VMEM((tm, d), F32)],
        compiler_params=_params(("parallel", "arbitrary")),
        name="mlp",
    )(x, norm_w.astype(F32).reshape(1, d), w1.astype(BF16), w2.astype(BF16), fw)


def _odd_proj_kernel(x_ref, nw_ref, w_ref, q_ref, f_ref, v_ref, g_ref):
    hn = _rmsnorm(x_ref[...], nw_ref[...]).astype(BF16)
    width = HG_HEADS * HG_KEY
    per_chunk = PROJ_COLS // HG_KEY
    for i, ref in enumerate((q_ref, f_ref, v_ref)):
        for c in range(width // PROJ_COLS):
            y = _dot(hn, w_ref[:, i * width + c * PROJ_COLS:i * width + (c + 1) * PROJ_COLS])
            for h in range(per_chunk):
                ref[c * per_chunk + h] = y[:, h * HG_KEY:(h + 1) * HG_KEY]
    for c in range(width // PROJ_COLS):
        g_ref[:, c * PROJ_COLS:(c + 1) * PROJ_COLS] = _dot(
            hn, w_ref[:, 3 * width + c * PROJ_COLS:3 * width + (c + 1) * PROJ_COLS]).astype(g_ref.dtype)


def _odd_proj(x, norm_w, w_in, tm):
    m, d = x.shape
    width = HG_HEADS * HG_KEY
    heads = lambda i: (0, i, 0)
    per_head = jax.ShapeDtypeStruct((HG_HEADS, m, HG_KEY), F32)
    return pl.pallas_call(
        _odd_proj_kernel,
        grid=(m // tm,),
        in_specs=[
            pl.BlockSpec((tm, d), lambda i: (i, 0)),
            pl.BlockSpec((1, d), lambda i: (0, 0)),
            pl.BlockSpec(w_in.shape, lambda i: (0, 0)),
        ],
        out_specs=[
            pl.BlockSpec((HG_HEADS, tm, HG_KEY), heads),
            pl.BlockSpec((HG_HEADS, tm, HG_KEY), heads),
            pl.BlockSpec((HG_HEADS, tm, HG_KEY), heads),
            pl.BlockSpec((tm, width), lambda i: (i, 0)),
        ],
        out_shape=[per_head, per_head, per_head, jax.ShapeDtypeStruct((m, width), BF16)],
        compiler_params=_params(("parallel",)),
        name="odd_proj",
    )(x, norm_w.astype(F32).reshape(1, d), w_in.astype(BF16))


def _hgrn_lower_bound(raw, layer):
    e = jnp.exp(raw - jnp.max(raw, axis=0, keepdims=True))
    p = e / jnp.sum(e, axis=0, keepdims=True)
    lb = jnp.zeros_like(p[0])
    for l in range(1, layer + 1):
        lb = lb + p[l]
    return lb


def _hgrn_gates(fpre, lb):
    sig, nsig = _sigmoid_pair(fpre)
    return lb + (1.0 - lb) * sig, (1.0 - lb) * nsig


def _hgrn_prompt_kernel(q_ref, f_ref, v_ref, lb_ref, o_ref, s_ref, st, b_scr, k_scr, od_scr, *, layer):
    c = pl.program_id(1)
    nc = pl.num_programs(1)
    L = q_ref.shape[1]
    nblk = L // HG_DIAG

    @pl.when(c == 0)
    def _():
        st[...] = jnp.zeros_like(st)

    row = lax.broadcasted_iota(jnp.int32, (L, L), 0)
    col = lax.broadcasted_iota(jnp.int32, (L, L), 1)
    lower_b = (col <= row).astype(BF16)
    pos = lax.broadcasted_iota(jnp.int32, (L, HG_KEY), 0)

    def heads(n, _):
        hs = [n * HG_HEAD_GROUP + e for e in range(HG_HEAD_GROUP)]
        every = range(HG_HEAD_GROUP)
        q = [q_ref[h] for h in hs]
        vb = [v_ref[h].astype(BF16) for h in hs]
        gates = [_hgrn_gates(f_ref[h], _hgrn_lower_bound(lb_ref[:, h], layer)) for h in hs]
        kin = [gk[1] for gk in gates]
        b = [sum(_dot(lower_b, part) for part in _split3(jnp.log(gk[0]))) for gk in gates]
        for e in every:
            b_scr[e] = b[e]
            k_scr[e] = kin[e]
        stt = [st[h] for h in hs]
        o = [_dot_nt((q[e] * jnp.exp(b[e])).astype(BF16), stt[e].astype(BF16)) for e in every]
        b_last = [x[L - 1:L, :] for x in b]
        for e in every:
            st[hs[e]] = stt[e] * jnp.exp(b_last[e]) + _dot_tn(
                vb[e], (kin[e] * jnp.exp(b_last[e] - b[e])).astype(BF16))

        att = [jnp.zeros((L, L), F32) for _ in every]
        g = L
        while g > HG_DIAG:
            half = g // 2
            late = (pos & (g - 1)) >= half
            same = (row ^ col) < g
            mid = [jnp.broadcast_to(x.reshape(L // g, g, HG_KEY)[:, half - 1:half, :],
                                    (L // g, g, HG_KEY)).reshape(L, HG_KEY) for x in b]
            dec = [jnp.exp(-jnp.abs(b[e] - mid[e])) for e in every]
            qd = [jnp.where(late, q[e] * dec[e], 0.0).astype(BF16) for e in every]
            kd = [jnp.where(late, 0.0, kin[e] * dec[e]).astype(BF16) for e in every]
            att = [att[e] + jnp.where(same, _dot_nt(qd[e], kd[e]), 0.0) for e in every]
            g = half
        o = [o[e] + _dot(att[e].astype(BF16), vb[e]) for e in every]

        at = lambda ref, i: ref[pl.ds(i, nblk, stride=HG_DIAG), :]
        for e in every:
            bs = [at(b_scr.at[e], i) for i in range(HG_DIAG)]
            ks = [at(k_scr.at[e], i) for i in range(HG_DIAG)]
            vs = [at(v_ref.at[hs[e]], i) for i in range(HG_DIAG)]
            for i in range(HG_DIAG):
                qi = at(q_ref.at[hs[e]], i)
                acc = jnp.zeros((nblk, HG_VAL), F32)
                for j in range(i + 1):
                    a = jnp.sum(qi * ks[j] * jnp.exp(bs[i] - bs[j]), axis=1, keepdims=True)
                    acc = acc + a * vs[j]
                od_scr[e, pl.ds(i, nblk, stride=HG_DIAG), :] = acc
        for e in every:
            o_ref[hs[e]] = (o[e] + od_scr[e]).astype(o_ref.dtype)
        return 0

    lax.fori_loop(0, q_ref.shape[0] // HG_HEAD_GROUP, heads, 0)

    @pl.when(c == nc - 1)
    def _():
        for h in range(s_ref.shape[1]):
            s_ref[0, h] = st[h].T


def _hgrn_prompt(q, f, v, lb_raw, layer, bsz, t):
    nh, m, _ = q.shape
    L = HG_CHUNK
    nc = t // L
    blk = lambda b, c: (0, b * nc + c, 0)
    depth = lb_raw.shape[0]
    return pl.pallas_call(
        functools.partial(_hgrn_prompt_kernel, layer=layer),
        grid=(bsz, nc),
        in_specs=[
            pl.BlockSpec((nh, L, HG_KEY), blk),
            pl.BlockSpec((nh, L, HG_KEY), blk),
            pl.BlockSpec((nh, L, HG_VAL), blk),
            pl.BlockSpec((depth, nh, 1, HG_KEY), lambda b, c: (0, 0, 0, 0)),
        ],
        out_specs=[
            pl.BlockSpec((nh, L, HG_VAL), blk),
            pl.BlockSpec((1, nh, HG_KEY, HG_VAL), lambda b, c: (b, 0, 0, 0)),
        ],
        out_shape=[
            jax.ShapeDtypeStruct((nh, m, HG_VAL), BF16),
            jax.ShapeDtypeStruct((bsz, nh, HG_KEY, HG_VAL), F32),
        ],
        scratch_shapes=[
            pltpu.VMEM((nh, HG_VAL, HG_KEY), F32),
            pltpu.VMEM((HG_HEAD_GROUP, L, HG_KEY), F32),
            pltpu.VMEM((HG_HEAD_GROUP, L, HG_KEY), F32),
            pltpu.VMEM((HG_HEAD_GROUP, L, HG_VAL), F32),
        ],
        compiler_params=_params(("parallel", "arbitrary")),
        name="hgrn_prompt",
    )(q, f, v, lb_raw.astype(F32).reshape(depth, nh, 1, HG_KEY))


def _hgrn_step_kernel(q_ref, f_ref, v_ref, lb_ref, s_ref, o_ref, snew_ref, *, layer):
    i = pl.program_id(0)
    for h in range(q_ref.shape[0]):
        q = q_ref[h, pl.ds(i, 1), :]
        v = v_ref[h, pl.ds(i, 1), :]
        fg, kin = _hgrn_gates(f_ref[h, pl.ds(i, 1), :], _hgrn_lower_bound(lb_ref[:, h], layer))
        new = _row_to_col(fg) * s_ref[0, h] + _row_to_col(kin) * v
        snew_ref[0, h] = new
        o_ref[h, pl.ds(i, 1), :] = jnp.sum(_row_to_col(q) * new, axis=0, keepdims=True)


def _hgrn_step(q, f, v, lb_raw, state, layer):
    nh, n, _ = q.shape
    depth = lb_raw.shape[0]
    whole = lambda i: (0, 0, 0)
    return pl.pallas_call(
        functools.partial(_hgrn_step_kernel, layer=layer),
        grid=(n,),
        in_specs=[
            pl.BlockSpec((nh, n, HG_KEY), whole),
            pl.BlockSpec((nh, n, HG_KEY), whole),
            pl.BlockSpec((nh, n, HG_VAL), whole),
            pl.BlockSpec((depth, nh, 1, HG_KEY), lambda i: (0, 0, 0, 0)),
            pl.BlockSpec((1, nh, HG_KEY, HG_VAL), lambda i: (i, 0, 0, 0)),
        ],
        out_specs=[
            pl.BlockSpec((nh, n, HG_VAL), whole),
            pl.BlockSpec((1, nh, HG_KEY, HG_VAL), lambda i: (i, 0, 0, 0)),
        ],
        out_shape=[
            jax.ShapeDtypeStruct((nh, n, HG_VAL), F32),
            jax.ShapeDtypeStruct((n, nh, HG_KEY, HG_VAL), F32),
        ],
        compiler_params=_params(("arbitrary",)),
        name="hgrn_step",
    )(q, f, v, lb_raw.astype(F32).reshape(depth, nh, 1, HG_KEY), state.astype(F32))


def _row_tile(m, want):
    return want if m % want == 0 else m


def kernel(x_prompt, x_sample, cache_k, cache_v, page_table, state_conv, state_ssm, state_hgrn, norm_mix, norm_ffn,
           norm_final, w_in_even, sb_bias, conv_w, conv_b, dt_bias, a_log, d_skip, ssd_norm, w_out_even, w_in_odd,
           hg_lb_raw, hg_norm, w_out_odd, w_ff1, w_ff2):
    bsz, t, d = x_prompt.shape
    n_seq = x_sample.shape[0]
    depth = norm_mix.shape[0]
    mp = bsz * t
    hp = x_prompt.reshape(mp, d)
    hs = x_sample.reshape(n_seq, d)
    tmp = _row_tile(mp, 512)
    tmm = _row_tile(mp, 1024)
    tf = 1024
    npair = SB_WIDTH // LANES
    outs = {k: [] for k in ("kp", "vp", "ks", "vs", "cp", "cs", "sp", "ss", "gp", "gs")}

    for layer in range(depth):
        li = layer // 2
        if layer % 2 == 0:
            ssd_w = (conv_w[li], conv_b[li], dt_bias[li], a_log[li], d_skip[li], ssd_norm[li])
            q, k, v, kf, vf, z, xbc, dt, dtt = _even_proj(hp, norm_mix[layer], w_in_even[li], tmp, seq_len=t)
            osb = _sb_prompt(q, k, v, sb_bias[li], bsz, t)
            y, cp, sp = _ssd_prompt(xbc, z, dt, dtt, *ssd_w, bsz, t)
            hp = _even_out(osb, y, hp, w_out_even[li], tmp)
            rows_view = lambda a: jnp.transpose(a.reshape(bsz, SB_HEADS, SB_HEAD_DIM, t), (0, 3, 1, 2))
            outs["kp"].append(rows_view(kf))
            outs["vp"].append(rows_view(vf))
            outs["cp"].append(cp)
            outs["sp"].append(sp)
            q, k, v, kf, vf, z, xbc, dt, dtt = _even_proj(hs, norm_mix[layer], w_in_even[li], n_seq)
            qs = q.astype(F32).transpose(1, 0, 2).reshape(n_seq, 1, SB_WIDTH)
            n_phys, page = cache_k.shape[1], cache_k.shape[2]
            to_lanes = lambda c: jnp.transpose(c, (0, 2, 3, 1)).reshape(n_phys, SB_WIDTH, page)
            osb = _sb_decode(qs, to_lanes(cache_k[li]), to_lanes(cache_v[li]), page_table, sb_bias[li])
            y, cs, ss = _ssd_step(xbc, z, dt, state_conv[li], state_ssm[li], *ssd_w)
            hs = _even_out(osb.astype(BF16), y.astype(BF16), hs, w_out_even[li], n_seq)
            outs["ks"].append(kf.reshape(n_seq, 1, SB_HEADS, SB_HEAD_DIM))
            outs["vs"].append(vf.reshape(n_seq, 1, SB_HEADS, SB_HEAD_DIM))
            outs["cs"].append(cs)
            outs["ss"].append(ss)
        else:
            q, f, v, g = _odd_proj(hp, norm_mix[layer], w_in_odd[li], tmp)
            o, gp = _hgrn_prompt(q, f, v, hg_lb_raw, layer, bsz, t)
            hp = _odd_out(o, g, hp, hg_norm[li], w_out_odd[li], tmp)
            outs["gp"].append(gp)
            q, f, v, g = _odd_proj(hs, norm_mix[layer], w_in_odd[li], n_seq)
            o, gs = _hgrn_step(q, f, v, hg_lb_raw, state_hgrn[li], layer)
            hs = _odd_out(o.astype(BF16), g, hs, hg_norm[li], w_out_odd[li], n_seq)
            outs["gs"].append(gs)
        fw = norm_final if layer == depth - 1 else None
        hp = _mlp(hp, norm_ffn[layer], w_ff1[layer], w_ff2[layer], fw, tmm, tf)
        hs = _mlp(hs, norm_ffn[layer], w_ff1[layer], w_ff2[layer], fw, n_seq, tf)

    y_prompt = hp.reshape(bsz, t, d)
    y_sample = hs.reshape(n_seq, 1, d)
    st = lambda key: jnp.stack(outs[key])
    return (y_prompt, y_sample, st("kp"), st("vp"), st("ks"), st("vs"), st("cp"), st("cs"), st("sp"), st("ss"),
            st("gp"), st("gs"))
```

```python
import functools

import jax
import jax.numpy as jnp
from jax import lax
from jax.experimental import pallas as pl
from jax.experimental.pallas import tpu as pltpu

F32 = jnp.float32
BF16 = jnp.bfloat16

EPS = 1e-6
LOG2E = 1.4426950408889634
LN2 = 0.6931471805599453
SB_HEADS = 8
SB_HEAD_DIM = 64
SB_WIDTH = SB_HEADS * SB_HEAD_DIM
SSD_HEADS = 8
SSD_HEAD_DIM = 64
SSD_WIDTH = SSD_HEADS * SSD_HEAD_DIM
SSD_STATE = 128
SSD_GROUPS = 2
SSD_CONV = 4
SSD_CONV_CH = SSD_WIDTH + 2 * SSD_GROUPS * SSD_STATE
HG_HEADS = 8
HG_KEY = 128
HG_VAL = 128

LANES = 128
SUBLANES = 8
VMEM_LIMIT_BYTES = 52 * 1024 * 1024

SB_TILE = 256
SB_DECODE_GROUP = 16
SSD_CHUNK = 128
HG_CHUNK = 128
HG_DIAG = 4
HG_HEAD_GROUP = 8
STEP_SEQS = 4
PROJ_COLS = 512


def _params(sem):
    return pltpu.CompilerParams(dimension_semantics=sem, vmem_limit_bytes=VMEM_LIMIT_BYTES)


def _dot(a, b):
    return jnp.dot(a, b, preferred_element_type=F32)


def _dot_nt(a, b):
    return lax.dot_general(a, b, (((1,), (1,)), ((), ())), preferred_element_type=F32)


def _dot_tn(a, b):
    return lax.dot_general(a, b, (((0,), (0,)), ((), ())), preferred_element_type=F32)


def _split3(x):
    hi = x.astype(BF16)
    r = x - hi.astype(F32)
    mid = r.astype(BF16)
    lo = (r - mid.astype(F32)).astype(BF16)
    return hi, mid, lo


def _rmsnorm(x, w):
    return x * lax.rsqrt(jnp.mean(x * x, axis=-1, keepdims=True) + EPS) * w


def _softplus(x):
    return jnp.maximum(x, 0.0) + jnp.log1p(jnp.exp(-jnp.abs(x)))


def _sigmoid_pair(x):
    e = jnp.exp(-jnp.abs(x))
    r = 1.0 / (1.0 + e)
    big, small = r, e * r
    pos = x >= 0
    return jnp.where(pos, big, small), jnp.where(pos, small, big)


def _silu(x):
    return x / (1.0 + jnp.exp(-x))


def _row_to_col(row):
    n = row.shape[1]
    eye = lax.broadcasted_iota(jnp.int32, (n, n), 0) == lax.broadcasted_iota(jnp.int32, (n, n), 1)
    return jnp.sum(jnp.where(eye, jnp.broadcast_to(row, (n, n)), 0.0), axis=1, keepdims=True)


def _col_to_row(col):
    n = col.shape[0]
    eye = lax.broadcasted_iota(jnp.int32, (n, n), 0) == lax.broadcasted_iota(jnp.int32, (n, n), 1)
    return jnp.sum(jnp.where(eye, jnp.broadcast_to(col, (n, n)), 0.0), axis=0, keepdims=True)


def _even_proj_kernel(x_ref, nw_ref, w_ref, wdt_ref, wdtt_ref,
                      q_ref, k_ref, v_ref, kf_ref, vf_ref, z_ref, xbc_ref, dt_ref, dtt_ref, *, token_minor):
    hn = _rmsnorm(x_ref[...], nw_ref[...]).astype(BF16)
    scale = LOG2E * SB_HEAD_DIM ** -0.5
    npair = SB_WIDTH // LANES
    q = _dot(hn, w_ref[:, 0:SB_WIDTH]) * scale
    for p in range(npair):
        q_ref[p] = q[:, p * LANES:(p + 1) * LANES].astype(BF16)
    for i, (pair_ref, full_ref) in enumerate(((k_ref, kf_ref), (v_ref, vf_ref))):
        y = _dot(hn, w_ref[:, (i + 1) * SB_WIDTH:(i + 2) * SB_WIDTH])
        if token_minor:
            full_ref[0] = y.T
        else:
            full_ref[...] = y
        for p in range(npair):
            pair_ref[p] = y[:, p * LANES:(p + 1) * LANES].astype(BF16)
    z0 = 3 * SB_WIDTH
    z_ref[...] = _dot(hn, w_ref[:, z0:z0 + SSD_WIDTH]).astype(z_ref.dtype)
    x0 = z0 + SSD_WIDTH
    for c in range(SSD_CONV_CH // PROJ_COLS):
        xbc_ref[:, c * PROJ_COLS:(c + 1) * PROJ_COLS] = _dot(
            hn, w_ref[:, x0 + c * PROJ_COLS:x0 + (c + 1) * PROJ_COLS])
    dt_ref[...] = _dot(hn, wdt_ref[...])[:, 0:SSD_HEADS]
    dtt_ref[...] = _dot_nt(wdtt_ref[...], hn)


def _even_proj(x, norm_w, w_in, tm, seq_len=None):
    m, d = x.shape
    npair = SB_WIDTH // LANES
    wb = w_in.astype(BF16)
    n_main = 3 * SB_WIDTH + SSD_WIDTH + SSD_CONV_CH
    w_main = wb[:, :n_main]
    w_dt = jnp.pad(wb[:, n_main:], ((0, 0), (0, LANES - SSD_HEADS)))
    w_dtt = wb[:, n_main:].T
    full = lambda i: (0, 0)
    rows = lambda i: (i, 0)
    hp = lambda i: (0, i, 0)
    token_minor = seq_len is not None
    if token_minor:
        nt = seq_len // tm
        kv_spec = pl.BlockSpec((1, SB_WIDTH, tm), lambda i: (i // nt, 0, i % nt))
        kv_shape = jax.ShapeDtypeStruct((m // seq_len, SB_WIDTH, seq_len), F32)
    else:
        kv_spec = pl.BlockSpec((tm, SB_WIDTH), rows)
        kv_shape = jax.ShapeDtypeStruct((m, SB_WIDTH), F32)
    return pl.pallas_call(
        functools.partial(_even_proj_kernel, token_minor=token_minor),
        grid=(m // tm,),
        in_specs=[
            pl.BlockSpec((tm, d), rows),
            pl.BlockSpec((1, d), full),
            pl.BlockSpec((d, n_main), full),
            pl.BlockSpec((d, LANES), full),
            pl.BlockSpec((SSD_HEADS, d), full),
        ],
        out_specs=[
            pl.BlockSpec((npair, tm, LANES), hp),
            pl.BlockSpec((npair, tm, LANES), hp),
            pl.BlockSpec((npair, tm, LANES), hp),
            kv_spec,
            kv_spec,
            pl.BlockSpec((tm, SSD_WIDTH), rows),
            pl.BlockSpec((tm, SSD_CONV_CH), rows),
            pl.BlockSpec((tm, SSD_HEADS), rows),
            pl.BlockSpec((SSD_HEADS, tm), lambda i: (0, i)),
        ],
        out_shape=[
            jax.ShapeDtypeStruct((npair, m, LANES), BF16),
            jax.ShapeDtypeStruct((npair, m, LANES), BF16),
            jax.ShapeDtypeStruct((npair, m, LANES), BF16),
            kv_shape,
            kv_shape,
            jax.ShapeDtypeStruct((m, SSD_WIDTH), BF16),
            jax.ShapeDtypeStruct((m, SSD_CONV_CH), F32),
            jax.ShapeDtypeStruct((m, SSD_HEADS), F32),
            jax.ShapeDtypeStruct((SSD_HEADS, m), F32),
        ],
        compiler_params=_params(("parallel",)),
        name="even_proj",
    )(x, norm_w.reshape(1, d), w_main, w_dt, w_dtt)


def _sb_neg_tri(n):
    j = jnp.arange(n)[:, None]
    s = jnp.arange(n)[None, :]
    return -(j > s).astype(BF16)


def _softplus2(z2):
    return jnp.log2(1.0 + jnp.exp2(-jnp.abs(z2))) + jnp.maximum(z2, 0.0)


def _sb_tiles(qs, kbs, vbs, ntri, biases2, carries, accs, valid):
    nh, nt = len(qs), len(kbs)
    t = kbs[0].shape[0]
    z2 = [[_dot_nt(qs[h], kbs[n]) + biases2[h] for h in range(nh)] for n in range(nt)]
    sp2 = [[_softplus2(z2[n][h]) for h in range(nh)] for n in range(nt)]
    sp2 = [[s if valid[n] is None else jnp.where(valid[n], s, 0.0) for s in sp2[n]] for n in range(nt)]
    carry_at = []
    for n in range(nt):
        carry_at.append(list(carries))
        carries = [carries[h] - jnp.broadcast_to(jnp.sum(sp2[n][h], axis=1, keepdims=True), carries[h].shape)
                   for h in range(nh)]
    rem2 = [[_dot(sp2[n][h].astype(BF16), ntri) + jnp.concatenate([carry_at[n][h]] * (t // LANES), axis=1)
             for h in range(nh)] for n in range(nt)]
    w = [[jnp.exp2((z2[n][h] - sp2[n][h]) + rem2[n][h]) for h in range(nh)] for n in range(nt)]
    w = [[x if valid[n] is None else jnp.where(valid[n], x, 0.0) for x in w[n]] for n in range(nt)]
    for n in range(nt):
        accs = [accs[h] + _dot(w[n][h].astype(BF16), vbs[n]) for h in range(nh)]
    return carries, accs


def _sb_prompt_kernel(bias_ref, q_ref, k_ref, v_ref, tri_ref, o_ref, c_scr, a_scr):
    p = pl.program_id(1)
    i = pl.program_id(2)
    tq = q_ref.shape[1]
    q = q_ref[0]
    lane = lax.broadcasted_iota(jnp.int32, (tq, LANES), 1)
    first = lane < SB_HEAD_DIM
    qs = (jnp.where(first, q, jnp.zeros_like(q)), jnp.where(first, jnp.zeros_like(q), q))
    biases = (bias_ref[2 * p] * LOG2E, bias_ref[2 * p + 1] * LOG2E)
    tri = tri_ref[...]
    c_scr[...] = jnp.zeros_like(c_scr)
    a_scr[...] = jnp.zeros_like(a_scr)

    def tiles(js, valid):
        kbs = [k_ref[0, pl.ds(pl.multiple_of(j * tq, tq), tq), :] for j in js]
        vbs = [v_ref[0, pl.ds(pl.multiple_of(j * tq, tq), tq), :] for j in js]
        carry, acc = _sb_tiles(qs, kbs, vbs, tri, biases, [c_scr[0], c_scr[1]], [a_scr[0], a_scr[1]], valid)
        for h in range(2):
            c_scr[h] = carry[h]
            a_scr[h] = acc[h]

    row = lax.broadcasted_iota(jnp.int32, (tq, tq), 0)
    col = lax.broadcasted_iota(jnp.int32, (tq, tq), 1)
    causal = col < row

    @pl.when(i == 0)
    def _():
        tiles([0], [causal])

    @pl.when(i > 0)
    def _():
        tiles([i, i - 1], [causal, None])

    rest = jnp.maximum(i - 1, 0)
    odd = rest % 2

    @pl.when(odd == 1)
    def _():
        tiles([i - 2], [None])

    def body(n, _):
        j = i - 2 - odd - 2 * n
        tiles([j, j - 1], [None, None])
        return 0

    lax.fori_loop(0, rest // 2, body, 0)
    o_ref[0] = jnp.where(first, a_scr[0], a_scr[1]).astype(o_ref.dtype)


def _sb_prompt(q, k, v, sb_bias, bsz, t):
    npair, m, _ = q.shape
    tq = SB_TILE
    nq = t // tq
    return pl.pallas_call(
        _sb_prompt_kernel,
        grid_spec=pltpu.PrefetchScalarGridSpec(
            num_scalar_prefetch=1,
            grid=(bsz, npair, nq),
            in_specs=[
                pl.BlockSpec((1, tq, LANES), lambda b, p, i, s: (p, b * nq + i, 0)),
                pl.BlockSpec((1, t, LANES), lambda b, p, i, s: (p, b, 0)),
                pl.BlockSpec((1, t, LANES), lambda b, p, i, s: (p, b, 0)),
                pl.BlockSpec((tq, tq), lambda b, p, i, s: (0, 0)),
            ],
            out_specs=pl.BlockSpec((1, tq, LANES), lambda b, p, i, s: (p, b * nq + i, 0)),
            scratch_shapes=[pltpu.VMEM((2, tq, LANES), F32), pltpu.VMEM((2, tq, LANES), F32)],
        ),
        out_shape=jax.ShapeDtypeStruct((npair, m, LANES), BF16),
        compiler_params=_params(("parallel", "parallel", "arbitrary")),
        name="sb_prompt",
    )(sb_bias.astype(F32), q, k, v, _sb_neg_tri(tq))


def _sb_decode_kernel(pt_ref, q_ref, bias_ref, *refs, group):
    k_refs, v_refs = refs[0:group], refs[group:2 * group]
    tri_ref, o_ref, c_scr, a_scr = refs[2 * group:]
    s = pl.program_id(0)
    g = pl.program_id(1)
    width = q_ref.shape[2]

    @pl.when(g == 0)
    def _():
        c_scr[...] = jnp.zeros_like(c_scr)
        a_scr[...] = jnp.zeros_like(a_scr)

    head = lax.broadcasted_iota(jnp.int32, (SB_HEADS, width), 0)
    lane = lax.broadcasted_iota(jnp.int32, (SB_HEADS, width), 1)
    own = (lane // SB_HEAD_DIM) == head
    qm = jnp.where(own, jnp.broadcast_to(q_ref[0], (SB_HEADS, width)), 0.0).astype(BF16)
    bias2 = jnp.concatenate([bias_ref[...] * LOG2E] * group, axis=0)
    z2 = jnp.concatenate([_dot(qm, k_refs[j][0].astype(BF16)) for j in range(group)], axis=0) + bias2
    sp2 = _softplus2(z2)
    hi = sp2.astype(BF16)
    lo = (sp2 - hi.astype(F32)).astype(BF16)
    tot = jnp.sum(sp2, axis=1, keepdims=True)
    carries = [c_scr[...]]
    for j in range(group):
        carries.append(carries[j] - tot[j * SB_HEADS:(j + 1) * SB_HEADS, :])
    c_scr[...] = carries[group]
    rem2 = _dot(hi, tri_ref[...]) + _dot(lo, tri_ref[...]) + jnp.concatenate(carries[0:group], axis=0)
    w = jnp.exp2((z2 - sp2) + rem2)
    acc = a_scr[...]
    for j in range(group):
        wj = w[j * SB_HEADS:(j + 1) * SB_HEADS, :].astype(BF16)
        acc = acc + _dot_nt(wj, v_refs[j][0].astype(BF16))
    a_scr[...] = acc

    @pl.when(g == pl.num_programs(1) - 1)
    def _():
        o = jnp.sum(jnp.where(own, acc, 0.0), axis=0, keepdims=True)
        for p in range(width // LANES):
            o_ref[p, pl.ds(s, 1), :] = o[:, p * LANES:(p + 1) * LANES]


def _sb_decode(q, cache_kt, cache_vt, page_table, sb_bias):
    n_seq, _, width = q.shape
    page = cache_kt.shape[2]
    npages = page_table.shape[1]
    group = SB_DECODE_GROUP if npages % SB_DECODE_GROUP == 0 else 1
    tri = _sb_neg_tri(page)

    def kv_spec(jj):
        return pl.BlockSpec((1, width, page), lambda b, g, pt: (pt[b, npages - 1 - (g * group + jj)], 0, 0))

    return pl.pallas_call(
        functools.partial(_sb_decode_kernel, group=group),
        grid_spec=pltpu.PrefetchScalarGridSpec(
            num_scalar_prefetch=1,
            grid=(n_seq, npages // group),
            in_specs=[
                pl.BlockSpec((1, 1, width), lambda b, g, pt: (b, 0, 0)),
                pl.BlockSpec((SB_HEADS, 1), lambda b, g, pt: (0, 0)),
                *[kv_spec(jj) for jj in range(group)],
                *[kv_spec(jj) for jj in range(group)],
                pl.BlockSpec((page, page), lambda b, g, pt: (0, 0)),
            ],
            out_specs=pl.BlockSpec((width // LANES, n_seq, LANES), lambda b, g, pt: (0, 0, 0)),
            scratch_shapes=[pltpu.VMEM((SB_HEADS, 1), F32), pltpu.VMEM((SB_HEADS, width), F32)],
        ),
        out_shape=jax.ShapeDtypeStruct((width // LANES, n_seq, LANES), F32),
        compiler_params=_params(("arbitrary", "arbitrary")),
        name="sb_decode",
    )(page_table, q, sb_bias.astype(F32).reshape(SB_HEADS, 1), *([cache_kt] * group), *([cache_vt] * group), tri)


def _ssd_prompt_kernel(xbc_ref, z_ref, dt_ref, dtt_ref, cw_ref, cb_ref, dtb_ref, dtbt_ref, alog_ref, alogt_ref,
                       dskip_ref, nw_ref, y_ref, conv_ref, ssm_ref, buf, st):
    c = pl.program_id(1)
    nc = pl.num_programs(1)
    L = xbc_ref.shape[0]
    pad = SUBLANES

    @pl.when(c == 0)
    def _():
        buf[0:pad, :] = jnp.zeros((pad, SSD_CONV_CH), F32)
        st[...] = jnp.zeros_like(st)

    buf[pad:pad + L, :] = xbc_ref[...]
    conv = cb_ref[...]
    for j in range(SSD_CONV):
        off = pad - (SSD_CONV - 1) + j
        conv = conv + cw_ref[j:j + 1, :] * buf[off:off + L, :]
    tail = buf[pad + L - (SSD_CONV - 1):pad + L, :]
    buf[pad - (SSD_CONV - 1):pad, :] = tail
    xa = _silu(conv)
    xs = xa[:, 0:SSD_WIDTH]
    gw = SSD_STATE
    bm = [xa[:, SSD_WIDTH + g * gw:SSD_WIDTH + (g + 1) * gw].astype(BF16) for g in range(SSD_GROUPS)]
    cm = [xa[:, SSD_WIDTH + (SSD_GROUPS + g) * gw:SSD_WIDTH + (SSD_GROUPS + g + 1) * gw].astype(BF16)
          for g in range(SSD_GROUPS)]

    dt = _softplus(dt_ref[...] + dtb_ref[...])
    dtt = _softplus(dtt_ref[...] + dtbt_ref[...])
    a = -jnp.exp(alog_ref[...])
    at = -jnp.exp(alogt_ref[...])
    row = lax.broadcasted_iota(jnp.int32, (L, L), 0)
    col = lax.broadcasted_iota(jnp.int32, (L, L), 1)
    lower = col <= row
    lower_b = lower.astype(BF16)
    upper_b = (row <= col).astype(BF16)
    cum = sum(_dot(lower_b, part) for part in _split3(dt * a))
    cumt = sum(_dot(part, upper_b) for part in _split3(dtt * at))
    last = cum[L - 1:L, :]
    wst = jnp.exp(last - cum) * dt
    ecum = jnp.exp(cum)
    elast = jnp.exp(last)

    lane = lax.broadcasted_iota(jnp.int32, (L, LANES), 1)
    first = lane < SSD_HEAD_DIM
    first_row = first[0:1, :]
    heads_per_group = SSD_HEADS // SSD_GROUPS
    cb = [jnp.where(lower, _dot_nt(cm[g], bm[g]), 0.0) for g in range(SSD_GROUPS)]
    ys = []
    for p in range(SSD_WIDTH // LANES):
        g = (2 * p) // heads_per_group
        xp = xs[:, p * LANES:(p + 1) * LANES]
        xpb = xp.astype(BF16)
        yi = []
        for h in (2 * p, 2 * p + 1):
            seg = cum[:, h:h + 1] - cumt[h:h + 1, :]
            dec = jnp.exp(jnp.minimum(seg, 0.0)) * dtt[h:h + 1, :]
            yi.append(_dot((cb[g] * dec).astype(BF16), xpb))
        y_intra = jnp.where(first, yi[0], yi[1])
        stp = st[:, p * LANES:(p + 1) * LANES]
        ec = jnp.where(first, ecum[:, 2 * p:2 * p + 1], ecum[:, 2 * p + 1:2 * p + 2])
        y_inter = _dot(cm[g], stp.astype(BF16)) * ec
        wp = jnp.where(first, wst[:, 2 * p:2 * p + 1], wst[:, 2 * p + 1:2 * p + 2])
        el = jnp.where(first_row, elast[:, 2 * p:2 * p + 1], elast[:, 2 * p + 1:2 * p + 2])
        st[:, p * LANES:(p + 1) * LANES] = el * stp + _dot_tn(bm[g], (xp * wp).astype(BF16))
        ys.append(y_intra + y_inter + dskip_ref[:, p * LANES:(p + 1) * LANES] * xp)
    y = jnp.concatenate(ys, axis=1)
    y_ref[...] = _rmsnorm(y * _silu(z_ref[...].astype(F32)), nw_ref[...]).astype(y_ref.dtype)

    @pl.when(c == nc - 1)
    def _():
        conv_ref[0] = tail
        ssm_ref[0] = st[...].T


def _ssd_prompt(xbc, z, dt, dtt, conv_w, conv_b, dt_bias, a_log, d_skip, ssd_norm, bsz, t):
    m = xbc.shape[0]
    L = SSD_CHUNK
    nc = t // L
    rows = lambda b, c: (b * nc + c, 0)
    full = lambda b, c: (0, 0)
    vec = lambda v: v.astype(F32).reshape(1, -1)
    colv = lambda v: v.astype(F32).reshape(-1, 1)
    y, conv, ssm = pl.pallas_call(
        _ssd_prompt_kernel,
        grid=(bsz, nc),
        in_specs=[
            pl.BlockSpec((L, SSD_CONV_CH), rows),
            pl.BlockSpec((L, SSD_WIDTH), rows),
            pl.BlockSpec((L, SSD_HEADS), rows),
            pl.BlockSpec((SSD_HEADS, L), lambda b, c: (0, b * nc + c)),
            pl.BlockSpec((SSD_CONV, SSD_CONV_CH), full),
            pl.BlockSpec((1, SSD_CONV_CH), full),
            pl.BlockSpec((1, SSD_HEADS), full),
            pl.BlockSpec((SSD_HEADS, 1), full),
            pl.BlockSpec((1, SSD_HEADS), full),
            pl.BlockSpec((SSD_HEADS, 1), full),
            pl.BlockSpec((1, SSD_WIDTH), full),
            pl.BlockSpec((1, SSD_WIDTH), full),
        ],
        out_specs=[
            pl.BlockSpec((L, SSD_WIDTH), rows),
            pl.BlockSpec((1, SSD_CONV - 1, SSD_CONV_CH), lambda b, c: (b, 0, 0)),
            pl.BlockSpec((1, SSD_WIDTH, SSD_STATE), lambda b, c: (b, 0, 0)),
        ],
        out_shape=[
            jax.ShapeDtypeStruct((m, SSD_WIDTH), BF16),
            jax.ShapeDtypeStruct((bsz, SSD_CONV - 1, SSD_CONV_CH), F32),
            jax.ShapeDtypeStruct((bsz, SSD_WIDTH, SSD_STATE), F32),
        ],
        scratch_shapes=[pltpu.VMEM((L + SUBLANES, SSD_CONV_CH), F32), pltpu.VMEM((SSD_STATE, SSD_WIDTH), F32)],
        compiler_params=_params(("parallel", "arbitrary")),
        name="ssd_prompt",
    )(xbc, z, dt, dtt, conv_w.astype(F32), vec(conv_b), vec(dt_bias), colv(dt_bias), vec(a_log), colv(a_log),
      vec(jnp.repeat(d_skip, SSD_HEAD_DIM)), vec(ssd_norm))
    return y, conv, ssm.reshape(bsz, SSD_HEADS, SSD_HEAD_DIM, SSD_STATE)


def _ssd_step_kernel(xbc_ref, z_ref, dt_ref, sconv_ref, sssm_ref, cw_ref, cb_ref, dtb_ref, alog_ref, dskip_ref,
                     nw_ref, y_ref, conv_ref, ssm_ref):
    lane_head = lax.broadcasted_iota(jnp.int32, (1, SSD_WIDTH), 1) // SSD_HEAD_DIM
    rows_per_group = SSD_WIDTH // SSD_GROUPS
    row = lax.broadcasted_iota(jnp.int32, (SSD_WIDTH, SSD_STATE), 0)
    a = -jnp.exp(alog_ref[...])
    for s in range(xbc_ref.shape[0]):
        xr = xbc_ref[s]
        cs = sconv_ref[s]
        conv = cb_ref[...] + cw_ref[SSD_CONV - 1:SSD_CONV, :] * xr
        for j in range(SSD_CONV - 1):
            conv = conv + cw_ref[j:j + 1, :] * cs[j:j + 1, :]
        conv_ref[s] = jnp.concatenate([cs[1:SSD_CONV - 1, :], xr], axis=0)
        xa = _silu(conv)
        xs = xa[:, 0:SSD_WIDTH]
        dt = _softplus(dt_ref[s] + dtb_ref[...])
        da = jnp.exp(dt * a)
        dt_w = jnp.zeros((1, SSD_WIDTH), F32)
        da_w = jnp.zeros((1, SSD_WIDTH), F32)
        for h in range(SSD_HEADS):
            dt_w = jnp.where(lane_head == h, dt[:, h:h + 1], dt_w)
            da_w = jnp.where(lane_head == h, da[:, h:h + 1], da_w)
        dtx_col = _row_to_col(dt_w * xs)
        da_col = _row_to_col(da_w)
        b_rows = jnp.zeros((SSD_WIDTH, SSD_STATE), F32)
        c_rows = jnp.zeros((SSD_WIDTH, SSD_STATE), F32)
        for g in range(SSD_GROUPS):
            sel = (row // rows_per_group) == g
            b0 = SSD_WIDTH + g * SSD_STATE
            c0 = SSD_WIDTH + (SSD_GROUPS + g) * SSD_STATE
            b_rows = jnp.where(sel, xa[:, b0:b0 + SSD_STATE], b_rows)
            c_rows = jnp.where(sel, xa[:, c0:c0 + SSD_STATE], c_rows)
        new = da_col * sssm_ref[s] + dtx_col * b_rows
        ssm_ref[s] = new
        y = _col_to_row(jnp.sum(new * c_rows, axis=1, keepdims=True)) + dskip_ref[...] * xs
        y_ref[s] = _rmsnorm(y * _silu(z_ref[s].astype(F32)), nw_ref[...])


def _ssd_step(xbc, z, dt, state_conv, state_ssm, conv_w, conv_b, dt_bias, a_log, d_skip, ssd_norm):
    n = xbc.shape[0]
    ns = STEP_SEQS if n % STEP_SEQS == 0 else 1
    vec = lambda v: v.astype(F32).reshape(1, -1)
    per = lambda i: (i, 0, 0)
    full = lambda i: (0, 0)
    y, conv, ssm = pl.pallas_call(
        _ssd_step_kernel,
        grid=(n // ns,),
        in_specs=[
            pl.BlockSpec((ns, 1, SSD_CONV_CH), per),
            pl.BlockSpec((ns, 1, SSD_WIDTH), per),
            pl.BlockSpec((ns, 1, SSD_HEADS), per),
            pl.BlockSpec((ns, SSD_CONV - 1, SSD_CONV_CH), per),
            pl.BlockSpec((ns, SSD_WIDTH, SSD_STATE), per),
            pl.BlockSpec((SSD_CONV, SSD_CONV_CH), full),
            pl.BlockSpec((1, SSD_CONV_CH), full),
            pl.BlockSpec((1, SSD_HEADS), full),
            pl.BlockSpec((1, SSD_HEADS), full),
            pl.BlockSpec((1, SSD_WIDTH), full),
            pl.BlockSpec((1, SSD_WIDTH), full),
        ],
        out_specs=[
            pl.BlockSpec((ns, 1, SSD_WIDTH), per),
            pl.BlockSpec((ns, SSD_CONV - 1, SSD_CONV_CH), per),
            pl.BlockSpec((ns, SSD_WIDTH, SSD_STATE), per),
        ],
        out_shape=[
            jax.ShapeDtypeStruct((n, 1, SSD_WIDTH), F32),
            jax.ShapeDtypeStruct((n, SSD_CONV - 1, SSD_CONV_CH), F32),
            jax.ShapeDtypeStruct((n, SSD_WIDTH, SSD_STATE), F32),
        ],
        compiler_params=_params(("parallel",)),
        name="ssd_step",
    )(xbc.reshape(n, 1, SSD_CONV_CH), z.reshape(n, 1, SSD_WIDTH), dt.reshape(n, 1, SSD_HEADS),
      state_conv.astype(F32), state_ssm.astype(F32).reshape(n, SSD_WIDTH, SSD_STATE), conv_w.astype(F32),
      vec(conv_b), vec(dt_bias), vec(a_log), vec(jnp.repeat(d_skip, SSD_HEAD_DIM)), vec(ssd_norm))
    return y.reshape(n, SSD_WIDTH), conv, ssm.reshape(n, SSD_HEADS, SSD_HEAD_DIM, SSD_STATE)


def _even_mix(osb_ref, y_ref):
    return jnp.concatenate([osb_ref[p] for p in range(osb_ref.shape[0])] + [y_ref[...]], axis=1)


def _odd_mix(o_ref, g_ref, nw_ref):
    o = jnp.concatenate([o_ref[h] for h in range(o_ref.shape[0])], axis=1).astype(F32)
    return (_rmsnorm(o, nw_ref[...]) * _silu(g_ref[...].astype(F32))).astype(BF16)


def _mix_mlp_kernel(*refs, mix_fn, n_mix, final_norm):
    mix_refs = refs[0:n_mix]
    x_ref, wout_ref, nw_ref, w1_ref, w2_ref, fw_ref, o_ref, hn_scr, acc_scr = refs[n_mix:]
    j = pl.program_id(1)

    @pl.when(j == 0)
    def _():
        h1 = x_ref[...] + _dot(mix_fn(*mix_refs), wout_ref[...])
        o_ref[...] = h1
        hn_scr[...] = _rmsnorm(h1, nw_ref[...]).astype(BF16)

    h = jnp.maximum(_dot(hn_scr[...], w1_ref[...]), 0.0)
    a = (h * h).astype(BF16)

    @pl.when(j == 0)
    def _():
        acc_scr[...] = _dot(a, w2_ref[...])

    @pl.when(j > 0)
    def _():
        acc_scr[...] += _dot(a, w2_ref[...])

    @pl.when(j == pl.num_programs(1) - 1)
    def _():
        out = o_ref[...] + acc_scr[...]
        o_ref[...] = _rmsnorm(out, fw_ref[...]) if final_norm else out


def _mix_mlp(mix_fn, mix_args, mix_specs, x, w_out, norm_w, w1_all, w2_all, layer, final_w, tm, tf):
    m, d = x.shape
    f = w1_all.shape[2]
    final_norm = final_w is not None
    fw = (final_w if final_norm else jnp.ones((d,), F32)).astype(F32).reshape(1, d)
    const = lambda i, j: (0, 0)
    return pl.pallas_call(
        functools.partial(_mix_mlp_kernel, mix_fn=mix_fn, n_mix=len(mix_args), final_norm=final_norm),
        grid=(m // tm, f // tf),
        in_specs=[
            *mix_specs,
            pl.BlockSpec((tm, d), lambda i, j: (i, 0)),
            pl.BlockSpec(w_out.shape, const),
            pl.BlockSpec((1, d), const),
            pl.BlockSpec((None, d, tf), lambda i, j: (layer, 0, j)),
            pl.BlockSpec((None, tf, d), lambda i, j: (layer, j, 0)),
            pl.BlockSpec((1, d), const),
        ],
        out_specs=pl.BlockSpec((tm, d), lambda i, j: (i, 0)),
        out_shape=jax.ShapeDtypeStruct((m, d), F32),
        scratch_shapes=[pltpu.VMEM((tm, d), BF16), pltpu.VMEM((tm, d), F32)],
        compiler_params=_params(("parallel", "arbitrary")),
        name="mix_mlp",
    )(*mix_args, x, w_out.astype(BF16), norm_w.astype(F32).reshape(1, d), w1_all, w2_all, fw)


def _even_mix_mlp(osb, y, x, w_out, norm_w, w1_all, w2_all, layer, final_w, tm, tf):
    specs = [pl.BlockSpec((osb.shape[0], tm, LANES), lambda i, j: (0, i, 0)),
             pl.BlockSpec((tm, SSD_WIDTH), lambda i, j: (i, 0))]
    return _mix_mlp(_even_mix, (osb, y), specs, x, w_out, norm_w, w1_all, w2_all, layer, final_w, tm, tf)


def _odd_mix_mlp(o, g, hg_norm, x, w_out, norm_w, w1_all, w2_all, layer, final_w, tm, tf):
    nh = o.shape[0]
    specs = [pl.BlockSpec((nh, tm, HG_VAL), lambda i, j: (0, i, 0)),
             pl.BlockSpec((tm, nh * HG_VAL), lambda i, j: (i, 0)),
             pl.BlockSpec((1, nh * HG_VAL), lambda i, j: (0, 0))]
    args = (o, g, hg_norm.astype(F32).reshape(1, -1))
    return _mix_mlp(_odd_mix, args, specs, x, w_out, norm_w, w1_all, w2_all, layer, final_w, tm, tf)


def _odd_proj_kernel(x_ref, nw_ref, w_ref, q_ref, f_ref, v_ref, g_ref):
    hn = _rmsnorm(x_ref[...], nw_ref[...]).astype(BF16)
    width = HG_HEADS * HG_KEY
    per_chunk = PROJ_COLS // HG_KEY
    for i, ref in enumerate((q_ref, f_ref, v_ref)):
        for c in range(width // PROJ_COLS):
            y = _dot(hn, w_ref[:, i * width + c * PROJ_COLS:i * width + (c + 1) * PROJ_COLS])
            for h in range(per_chunk):
                ref[c * per_chunk + h] = y[:, h * HG_KEY:(h + 1) * HG_KEY]
    for c in range(width // PROJ_COLS):
        g_ref[:, c * PROJ_COLS:(c + 1) * PROJ_COLS] = _dot(
            hn, w_ref[:, 3 * width + c * PROJ_COLS:3 * width + (c + 1) * PROJ_COLS]).astype(g_ref.dtype)


def _odd_proj(x, norm_w, w_in, tm):
    m, d = x.shape
    width = HG_HEADS * HG_KEY
    heads = lambda i: (0, i, 0)
    per_head = jax.ShapeDtypeStruct((HG_HEADS, m, HG_KEY), F32)
    return pl.pallas_call(
        _odd_proj_kernel,
        grid=(m // tm,),
        in_specs=[
            pl.BlockSpec((tm, d), lambda i: (i, 0)),
            pl.BlockSpec((1, d), lambda i: (0, 0)),
            pl.BlockSpec(w_in.shape, lambda i: (0, 0)),
        ],
        out_specs=[
            pl.BlockSpec((HG_HEADS, tm, HG_KEY), heads),
            pl.BlockSpec((HG_HEADS, tm, HG_KEY), heads),
            pl.BlockSpec((HG_HEADS, tm, HG_KEY), heads),
            pl.BlockSpec((tm, width), lambda i: (i, 0)),
        ],
        out_shape=[per_head, per_head, per_head, jax.ShapeDtypeStruct((m, width), BF16)],
        compiler_params=_params(("parallel",)),
        name="odd_proj",
    )(x, norm_w.astype(F32).reshape(1, d), w_in.astype(BF16))


def _hgrn_lower_bound(raw, layer):
    e = jnp.exp(raw - jnp.max(raw, axis=0, keepdims=True))
    p = e / jnp.sum(e, axis=0, keepdims=True)
    lb = jnp.zeros_like(p[0])
    for l in range(1, layer + 1):
        lb = lb + p[l]
    return lb


def _hgrn_gates(fpre, lb):
    sig, nsig = _sigmoid_pair(fpre)
    return lb + (1.0 - lb) * sig, (1.0 - lb) * nsig


def _hgrn_prompt_kernel(q_ref, f_ref, v_ref, lb_ref, o_ref, s_ref, st, b_scr, k_scr, od_scr, *, layer):
    c = pl.program_id(1)
    nc = pl.num_programs(1)
    L = q_ref.shape[1]
    nblk = L // HG_DIAG

    @pl.when(c == 0)
    def _():
        st[...] = jnp.zeros_like(st)

    row = lax.broadcasted_iota(jnp.int32, (L, L), 0)
    col = lax.broadcasted_iota(jnp.int32, (L, L), 1)
    lower_b = (col <= row).astype(BF16)
    pos = lax.broadcasted_iota(jnp.int32, (L, HG_KEY), 0)

    def heads(n, _):
        hs = [n * HG_HEAD_GROUP + e for e in range(HG_HEAD_GROUP)]
        every = range(HG_HEAD_GROUP)
        q = [q_ref[h] for h in hs]
        vb = [v_ref[h].astype(BF16) for h in hs]
        gates = [_hgrn_gates(f_ref[h], _hgrn_lower_bound(lb_ref[:, h], layer)) for h in hs]
        kin = [gk[1] for gk in gates]
        b = [sum(_dot(lower_b, part) for part in _split3(jnp.log(gk[0]))) for gk in gates]
        for e in every:
            b_scr[e] = b[e]
            k_scr[e] = kin[e]
        stt = [st[h] for h in hs]
        o = [_dot_nt((q[e] * jnp.exp(b[e])).astype(BF16), stt[e].astype(BF16)) for e in every]
        b_last = [x[L - 1:L, :] for x in b]
        for e in every:
            st[hs[e]] = stt[e] * jnp.exp(b_last[e]) + _dot_tn(
                vb[e], (kin[e] * jnp.exp(b_last[e] - b[e])).astype(BF16))

        att = [jnp.zeros((L, L), F32) for _ in every]
        g = L
        while g > HG_DIAG:
            half = g // 2
            late = (pos & (g - 1)) >= half
            same = (row ^ col) < g
            mid = [jnp.broadcast_to(x.reshape(L // g, g, HG_KEY)[:, half - 1:half, :],
                                    (L // g, g, HG_KEY)).reshape(L, HG_KEY) for x in b]
            dec = [jnp.exp(-jnp.abs(b[e] - mid[e])) for e in every]
            qd = [jnp.where(late, q[e] * dec[e], 0.0).astype(BF16) for e in every]
            kd = [jnp.where(late, 0.0, kin[e] * dec[e]).astype(BF16) for e in every]
            att = [att[e] + jnp.where(same, _dot_nt(qd[e], kd[e]), 0.0) for e in every]
            g = half
        o = [o[e] + _dot(att[e].astype(BF16), vb[e]) for e in every]

        at = lambda ref, i: ref[pl.ds(i, nblk, stride=HG_DIAG), :]
        for e in every:
            bs = [at(b_scr.at[e], i) for i in range(HG_DIAG)]
            ks = [at(k_scr.at[e], i) for i in range(HG_DIAG)]
            vs = [at(v_ref.at[hs[e]], i) for i in range(HG_DIAG)]
            for i in range(HG_DIAG):
                qi = at(q_ref.at[hs[e]], i)
                acc = jnp.zeros((nblk, HG_VAL), F32)
                for j in range(i + 1):
                    a = jnp.sum(qi * ks[j] * jnp.exp(bs[i] - bs[j]), axis=1, keepdims=True)
                    acc = acc + a * vs[j]
                od_scr[e, pl.ds(i, nblk, stride=HG_DIAG), :] = acc
        for e in every:
            o_ref[hs[e]] = (o[e] + od_scr[e]).astype(o_ref.dtype)
        return 0

    lax.fori_loop(0, q_ref.shape[0] // HG_HEAD_GROUP, heads, 0)

    @pl.when(c == nc - 1)
    def _():
        for h in range(s_ref.shape[1]):
            s_ref[0, h] = st[h].T


def _hgrn_prompt(q, f, v, lb_raw, layer, bsz, t):
    nh, m, _ = q.shape
    L = HG_CHUNK
    nc = t // L
    blk = lambda b, c: (0, b * nc + c, 0)
    depth = lb_raw.shape[0]
    return pl.pallas_call(
        functools.partial(_hgrn_prompt_kernel, layer=layer),
        grid=(bsz, nc),
        in_specs=[
            pl.BlockSpec((nh, L, HG_KEY), blk),
            pl.BlockSpec((nh, L, HG_KEY), blk),
            pl.BlockSpec((nh, L, HG_VAL), blk),
            pl.BlockSpec((depth, nh, 1, HG_KEY), lambda b, c: (0, 0, 0, 0)),
        ],
        out_specs=[
            pl.BlockSpec((nh, L, HG_VAL), blk),
            pl.BlockSpec((1, nh, HG_KEY, HG_VAL), lambda b, c: (b, 0, 0, 0)),
        ],
        out_shape=[
            jax.ShapeDtypeStruct((nh, m, HG_VAL), BF16),
            jax.ShapeDtypeStruct((bsz, nh, HG_KEY, HG_VAL), F32),
        ],
        scratch_shapes=[
            pltpu.VMEM((nh, HG_VAL, HG_KEY), F32),
            pltpu.VMEM((HG_HEAD_GROUP, L, HG_KEY), F32),
            pltpu.VMEM((HG_HEAD_GROUP, L, HG_KEY), F32),
            pltpu.VMEM((HG_HEAD_GROUP, L, HG_VAL), F32),
        ],
        compiler_params=_params(("parallel", "arbitrary")),
        name="hgrn_prompt",
    )(q, f, v, lb_raw.astype(F32).reshape(depth, nh, 1, HG_KEY))


def _hgrn_step_kernel(q_ref, f_ref, v_ref, lb_ref, s_ref, o_ref, snew_ref, *, layer):
    ns = s_ref.shape[0]
    for h in range(q_ref.shape[0]):
        lb = _hgrn_lower_bound(lb_ref[:, h], layer)
        for s in range(ns):
            i = pl.program_id(0) * ns + s
            q = q_ref[h, pl.ds(i, 1), :]
            v = v_ref[h, pl.ds(i, 1), :]
            fg, kin = _hgrn_gates(f_ref[h, pl.ds(i, 1), :], lb)
            new = _row_to_col(fg) * s_ref[s, h] + _row_to_col(kin) * v
            snew_ref[s, h] = new
            o_ref[h, pl.ds(i, 1), :] = jnp.sum(_row_to_col(q) * new, axis=0, keepdims=True)


def _hgrn_step(q, f, v, lb_raw, state, layer):
    nh, n, _ = q.shape
    ns = STEP_SEQS if n % STEP_SEQS == 0 else 1
    depth = lb_raw.shape[0]
    whole = lambda i: (0, 0, 0)
    return pl.pallas_call(
        functools.partial(_hgrn_step_kernel, layer=layer),
        grid=(n // ns,),
        in_specs=[
            pl.BlockSpec((nh, n, HG_KEY), whole),
            pl.BlockSpec((nh, n, HG_KEY), whole),
            pl.BlockSpec((nh, n, HG_VAL), whole),
            pl.BlockSpec((depth, nh, 1, HG_KEY), lambda i: (0, 0, 0, 0)),
            pl.BlockSpec((ns, nh, HG_KEY, HG_VAL), lambda i: (i, 0, 0, 0)),
        ],
        out_specs=[
            pl.BlockSpec((nh, n, HG_VAL), whole),
            pl.BlockSpec((ns, nh, HG_KEY, HG_VAL), lambda i: (i, 0, 0, 0)),
        ],
        out_shape=[
            jax.ShapeDtypeStruct((nh, n, HG_VAL), F32),
            jax.ShapeDtypeStruct((n, nh, HG_KEY, HG_VAL), F32),
        ],
        compiler_params=_params(("arbitrary",)),
        name="hgrn_step",
    )(q, f, v, lb_raw.astype(F32).reshape(depth, nh, 1, HG_KEY), state.astype(F32))


def _row_tile(m, want):
    return want if m % want == 0 else m


def kernel(x_prompt, x_sample, cache_k, cache_v, page_table, state_conv, state_ssm, state_hgrn, norm_mix, norm_ffn,
           norm_final, w_in_even, sb_bias, conv_w, conv_b, dt_bias, a_log, d_skip, ssd_norm, w_out_even, w_in_odd,
           hg_lb_raw, hg_norm, w_out_odd, w_ff1, w_ff2):
    bsz, t, d = x_prompt.shape
    n_seq = x_sample.shape[0]
    depth = norm_mix.shape[0]
    mp = bsz * t
    hp = x_prompt.reshape(mp, d)
    hs = x_sample.reshape(n_seq, d)
    tmp = _row_tile(mp, 512)
    tmm = _row_tile(mp, 1024)
    tf = 1024
    outs = {k: [] for k in ("kp", "vp", "ks", "vs", "cp", "cs", "sp", "ss", "gp", "gs")}
    w1_all = w_ff1.astype(BF16)
    w2_all = w_ff2.astype(BF16)

    for layer in range(depth):
        li = layer // 2
        fw = norm_final if layer == depth - 1 else None
        mlp_w = (norm_ffn[layer], w1_all, w2_all, layer, fw)
        if layer % 2 == 0:
            ssd_w = (conv_w[li], conv_b[li], dt_bias[li], a_log[li], d_skip[li], ssd_norm[li])
            q, k, v, kf, vf, z, xbc, dt, dtt = _even_proj(hp, norm_mix[layer], w_in_even[li], tmp, seq_len=t)
            osb = _sb_prompt(q, k, v, sb_bias[li], bsz, t)
            y, cp, sp = _ssd_prompt(xbc, z, dt, dtt, *ssd_w, bsz, t)
            hp = _even_mix_mlp(osb, y, hp, w_out_even[li], *mlp_w, tmm, tf)
            rows_view = lambda a: jnp.transpose(a.reshape(bsz, SB_HEADS, SB_HEAD_DIM, t), (0, 3, 1, 2))
            outs["kp"].append(rows_view(kf))
            outs["vp"].append(rows_view(vf))
            outs["cp"].append(cp)
            outs["sp"].append(sp)
            q, k, v, kf, vf, z, xbc, dt, dtt = _even_proj(hs, norm_mix[layer], w_in_even[li], n_seq)
            qs = q.astype(F32).transpose(1, 0, 2).reshape(n_seq, 1, SB_WIDTH)
            n_phys, page = cache_k.shape[1], cache_k.shape[2]
            to_lanes = lambda c: jnp.transpose(c, (0, 2, 3, 1)).reshape(n_phys, SB_WIDTH, page)
            osb = _sb_decode(qs, to_lanes(cache_k[li]), to_lanes(cache_v[li]), page_table, sb_bias[li])
            y, cs, ss = _ssd_step(xbc, z, dt, state_conv[li], state_ssm[li], *ssd_w)
            hs = _even_mix_mlp(osb.astype(BF16), y.astype(BF16), hs, w_out_even[li], *mlp_w, n_seq, tf)
            outs["ks"].append(kf.reshape(n_seq, 1, SB_HEADS, SB_HEAD_DIM))
            outs["vs"].append(vf.reshape(n_seq, 1, SB_HEADS, SB_HEAD_DIM))
            outs["cs"].append(cs)
            outs["ss"].append(ss)
        else:
            q, f, v, g = _odd_proj(hp, norm_mix[layer], w_in_odd[li], tmp)
            o, gp = _hgrn_prompt(q, f, v, hg_lb_raw, layer, bsz, t)
            hp = _odd_mix_mlp(o, g, hg_norm[li], hp, w_out_odd[li], *mlp_w, tmm, tf)
            outs["gp"].append(gp)
            q, f, v, g = _odd_proj(hs, norm_mix[layer], w_in_odd[li], n_seq)
            o, gs = _hgrn_step(q, f, v, hg_lb_raw, state_hgrn[li], layer)
            hs = _odd_mix_mlp(o.astype(BF16), g, hg_norm[li], hs, w_out_odd[li], *mlp_w, n_seq, tf)
            outs["gs"].append(gs)

    y_prompt = hp.reshape(bsz, t, d)
    y_sample = hs.reshape(n_seq, 1, d)
    st = lambda key: jnp.stack(outs[key])
    return (y_prompt, y_sample, st("kp"), st("vp"), st("ks"), st("vs"), st("cp"), st("cs"), st("sp"), st("ss"),
            st("gp"), st("gs"))
```

```python
import functools

import jax
import jax.numpy as jnp
from jax import lax
from jax.experimental import pallas as pl
from jax.experimental.pallas import tpu as pltpu

F32 = jnp.float32
BF16 = jnp.bfloat16

EPS = 1e-6
LOG2E = 1.4426950408889634
LN2 = 0.6931471805599453
SB_HEADS = 8
SB_HEAD_DIM = 64
SB_WIDTH = SB_HEADS * SB_HEAD_DIM
SSD_HEADS = 8
SSD_HEAD_DIM = 64
SSD_WIDTH = SSD_HEADS * SSD_HEAD_DIM
SSD_STATE = 128
SSD_GROUPS = 2
SSD_CONV = 4
SSD_CONV_CH = SSD_WIDTH + 2 * SSD_GROUPS * SSD_STATE
HG_HEADS = 8
HG_KEY = 128
HG_VAL = 128

LANES = 128
SUBLANES = 8
VMEM_LIMIT_BYTES = 52 * 1024 * 1024

SB_TILE = 256
SB_DECODE_GROUP = 16
SSD_CHUNK = 128
HG_CHUNK = 128
HG_DIAG = 4
HG_HEAD_GROUP = 8
STEP_SEQS = 4
PROJ_COLS = 512


def _params(sem):
    return pltpu.CompilerParams(dimension_semantics=sem, vmem_limit_bytes=VMEM_LIMIT_BYTES)


def _dot(a, b):
    return jnp.dot(a, b, preferred_element_type=F32)


def _dot_nt(a, b):
    return lax.dot_general(a, b, (((1,), (1,)), ((), ())), preferred_element_type=F32)


def _dot_tn(a, b):
    return lax.dot_general(a, b, (((0,), (0,)), ((), ())), preferred_element_type=F32)


def _split3(x):
    hi = x.astype(BF16)
    r = x - hi.astype(F32)
    mid = r.astype(BF16)
    lo = (r - mid.astype(F32)).astype(BF16)
    return hi, mid, lo


def _rmsnorm(x, w):
    return x * lax.rsqrt(jnp.mean(x * x, axis=-1, keepdims=True) + EPS) * w


def _softplus(x):
    return jnp.maximum(x, 0.0) + jnp.log1p(jnp.exp(-jnp.abs(x)))


def _sigmoid_pair(x):
    e = jnp.exp(-jnp.abs(x))
    r = 1.0 / (1.0 + e)
    big, small = r, e * r
    pos = x >= 0
    return jnp.where(pos, big, small), jnp.where(pos, small, big)


def _silu(x):
    return x / (1.0 + jnp.exp(-x))


def _row_to_col(row):
    n = row.shape[1]
    eye = lax.broadcasted_iota(jnp.int32, (n, n), 0) == lax.broadcasted_iota(jnp.int32, (n, n), 1)
    return jnp.sum(jnp.where(eye, jnp.broadcast_to(row, (n, n)), 0.0), axis=1, keepdims=True)


def _col_to_row(col):
    n = col.shape[0]
    eye = lax.broadcasted_iota(jnp.int32, (n, n), 0) == lax.broadcasted_iota(jnp.int32, (n, n), 1)
    return jnp.sum(jnp.where(eye, jnp.broadcast_to(col, (n, n)), 0.0), axis=0, keepdims=True)


def _even_proj_kernel(x_ref, nw_ref, w_ref, wdt_ref, wdtt_ref,
                      q_ref, k_ref, v_ref, kf_ref, vf_ref, z_ref, xbc_ref, dt_ref, dtt_ref, *, token_minor):
    hn = _rmsnorm(x_ref[...], nw_ref[...]).astype(BF16)
    scale = LOG2E * SB_HEAD_DIM ** -0.5
    npair = SB_WIDTH // LANES
    q = _dot(hn, w_ref[:, 0:SB_WIDTH]) * scale
    for p in range(npair):
        q_ref[p] = q[:, p * LANES:(p + 1) * LANES].astype(BF16)
    for i, (pair_ref, full_ref) in enumerate(((k_ref, kf_ref), (v_ref, vf_ref))):
        y = _dot(hn, w_ref[:, (i + 1) * SB_WIDTH:(i + 2) * SB_WIDTH])
        if token_minor:
            full_ref[0] = y.T
        else:
            full_ref[...] = y
        for p in range(npair):
            pair_ref[p] = y[:, p * LANES:(p + 1) * LANES].astype(BF16)
    z0 = 3 * SB_WIDTH
    z_ref[...] = _dot(hn, w_ref[:, z0:z0 + SSD_WIDTH]).astype(z_ref.dtype)
    x0 = z0 + SSD_WIDTH
    for c in range(SSD_CONV_CH // PROJ_COLS):
        xbc_ref[:, c * PROJ_COLS:(c + 1) * PROJ_COLS] = _dot(
            hn, w_ref[:, x0 + c * PROJ_COLS:x0 + (c + 1) * PROJ_COLS])
    dt_ref[...] = _dot(hn, wdt_ref[...])[:, 0:SSD_HEADS]
    dtt_ref[...] = _dot_nt(wdtt_ref[...], hn)


def _even_proj(x, norm_w, w_in, tm, seq_len=None):
    m, d = x.shape
    npair = SB_WIDTH // LANES
    wb = w_in.astype(BF16)
    n_main = 3 * SB_WIDTH + SSD_WIDTH + SSD_CONV_CH
    w_main = wb
    w_dt = jnp.pad(wb[:, n_main:], ((0, 0), (0, LANES - SSD_HEADS)))
    w_dtt = wb[:, n_main:].T
    full = lambda i: (0, 0)
    rows = lambda i: (i, 0)
    hp = lambda i: (0, i, 0)
    token_minor = seq_len is not None
    if token_minor:
        nt = seq_len // tm
        kv_spec = pl.BlockSpec((1, SB_WIDTH, tm), lambda i: (i // nt, 0, i % nt))
        kv_shape = jax.ShapeDtypeStruct((m // seq_len, SB_WIDTH, seq_len), F32)
    else:
        kv_spec = pl.BlockSpec((tm, SB_WIDTH), rows)
        kv_shape = jax.ShapeDtypeStruct((m, SB_WIDTH), F32)
    return pl.pallas_call(
        functools.partial(_even_proj_kernel, token_minor=token_minor),
        grid=(m // tm,),
        in_specs=[
            pl.BlockSpec((tm, d), rows),
            pl.BlockSpec((1, d), full),
            pl.BlockSpec(wb.shape, full),
            pl.BlockSpec((d, LANES), full),
            pl.BlockSpec((SSD_HEADS, d), full),
        ],
        out_specs=[
            pl.BlockSpec((npair, tm, LANES), hp),
            pl.BlockSpec((npair, tm, LANES), hp),
            pl.BlockSpec((npair, tm, LANES), hp),
            kv_spec,
            kv_spec,
            pl.BlockSpec((tm, SSD_WIDTH), rows),
            pl.BlockSpec((tm, SSD_CONV_CH), rows),
            pl.BlockSpec((tm, SSD_HEADS), rows),
            pl.BlockSpec((SSD_HEADS, tm), lambda i: (0, i)),
        ],
        out_shape=[
            jax.ShapeDtypeStruct((npair, m, LANES), BF16),
            jax.ShapeDtypeStruct((npair, m, LANES), BF16),
            jax.ShapeDtypeStruct((npair, m, LANES), BF16),
            kv_shape,
            kv_shape,
            jax.ShapeDtypeStruct((m, SSD_WIDTH), BF16),
            jax.ShapeDtypeStruct((m, SSD_CONV_CH), F32),
            jax.ShapeDtypeStruct((m, SSD_HEADS), F32),
            jax.ShapeDtypeStruct((SSD_HEADS, m), F32),
        ],
        compiler_params=_params(("parallel",)),
        name="even_proj",
    )(x, norm_w.reshape(1, d), w_main, w_dt, w_dtt)


def _sb_neg_tri(n):
    j = jnp.arange(n)[:, None]
    s = jnp.arange(n)[None, :]
    return -(j > s).astype(BF16)


def _softplus2(z2):
    return jnp.log2(1.0 + jnp.exp2(-jnp.abs(z2))) + jnp.maximum(z2, 0.0)


def _sb_tiles(qs, kbs, vbs, ntri, biases2, carries, accs, valid, row0):
    nh, nt = len(qs), len(kbs)
    t = kbs[0].shape[0]
    tail = lambda x, n: x[row0[n]:]
    put = lambda x, n, new: new if row0[n] == 0 else jnp.concatenate([x[:row0[n]], new], axis=0)
    z2 =[[_dot_nt(tail(qs[h], n), kbs[n]) + biases2[h] for h in range(nh)] for n in range(nt)]
    sp2 = [[_softplus2(z2[n][h]) for h in range(nh)] for n in range(nt)]
    sp2 = [[s if valid[n] is None else jnp.where(tail(valid[n], n), s, 0.0) for s in sp2[n]] for n in range(nt)]
    carry_at = []
    for n in range(nt):
        carry_at.append([tail(c, n) for c in carries])
        tot = [jnp.broadcast_to(jnp.sum(sp2[n][h], axis=1, keepdims=True), carry_at[n][h].shape) for h in range(nh)]
        carries = [put(carries[h], n, carry_at[n][h] - tot[h]) for h in range(nh)]
    rem2 = [[_dot(sp2[n][h].astype(BF16), ntri) + jnp.concatenate([carry_at[n][h]] * (t // LANES), axis=1)
             for h in range(nh)] for n in range(nt)]
    w = [[jnp.exp2((z2[n][h] - sp2[n][h]) + rem2[n][h]) for h in range(nh)] for n in range(nt)]
    w = [[x if valid[n] is None else jnp.where(tail(valid[n], n), x, 0.0) for x in w[n]] for n in range(nt)]
    for n in range(nt):
        pv = [_dot(w[n][h].astype(BF16), vbs[n]) for h in range(nh)]
        accs = [put(accs[h], n, tail(accs[h], n) + pv[h]) for h in range(nh)]
    return carries, accs


def _sb_prompt_kernel(bias_ref, q_ref, k_ref, v_ref, tri_ref, o_ref, c_scr, a_scr):
    p = pl.program_id(1)
    i = pl.program_id(2)
    tq = q_ref.shape[1]
    q = q_ref[0]
    lane = lax.broadcasted_iota(jnp.int32, (tq, LANES), 1)
    first = lane < SB_HEAD_DIM
    qs = (jnp.where(first, q, jnp.zeros_like(q)), jnp.where(first, jnp.zeros_like(q), q))
    biases = (bias_ref[2 * p] * LOG2E, bias_ref[2 * p + 1] * LOG2E)
    tri = tri_ref[...]
    c_scr[...] = jnp.zeros_like(c_scr)
    a_scr[...] = jnp.zeros_like(a_scr)

    tk = tri_ref.shape[0]

    def tiles(js, valid, row0):
        kbs = [k_ref[0, pl.ds(pl.multiple_of(j * tk, tk), tk), :] for j in js]
        vbs = [v_ref[0, pl.ds(pl.multiple_of(j * tk, tk), tk), :] for j in js]
        carry, acc = _sb_tiles(qs, kbs, vbs, tri, biases, [c_scr[0], c_scr[1]], [a_scr[0], a_scr[1]], valid, row0)
        for h in range(2):
            c_scr[h] = carry[h]
            a_scr[h] = acc[h]

    row = lax.broadcasted_iota(jnp.int32, (tq, tk), 0)
    col = lax.broadcasted_iota(jnp.int32, (tq, tk), 1)
    tiles([2 * i + 1, 2 * i], [col + tk < row, col < row], [tk, 0])

    def body(n, _):
        j = 2 * i - 1 - 2 * n
        tiles([j, j - 1], [None, None], [0, 0])
        return 0

    lax.fori_loop(0, i, body, 0)
    o_ref[0] = jnp.where(first, a_scr[0], a_scr[1]).astype(o_ref.dtype)


def _sb_prompt(q, k, v, sb_bias, bsz, t):
    npair, m, _ = q.shape
    tk = SB_TILE
    tq = 2 * tk
    nq = t // tq
    return pl.pallas_call(
        _sb_prompt_kernel,
        grid_spec=pltpu.PrefetchScalarGridSpec(
            num_scalar_prefetch=1,
            grid=(bsz, npair, nq),
            in_specs=[
                pl.BlockSpec((1, tq, LANES), lambda b, p, i, s: (p, b * nq + i, 0)),
                pl.BlockSpec((1, t, LANES), lambda b, p, i, s: (p, b, 0)),
                pl.BlockSpec((1, t, LANES), lambda b, p, i, s: (p, b, 0)),
                pl.BlockSpec((tk, tk), lambda b, p, i, s: (0, 0)),
            ],
            out_specs=pl.BlockSpec((1, tq, LANES), lambda b, p, i, s: (p, b * nq + i, 0)),
            scratch_shapes=[pltpu.VMEM((2, tq, LANES), F32), pltpu.VMEM((2, tq, LANES), F32)],
        ),
        out_shape=jax.ShapeDtypeStruct((npair, m, LANES), BF16),
        compiler_params=_params(("parallel", "parallel", "arbitrary")),
        name="sb_prompt",
    )(sb_bias.astype(F32), q, k, v, _sb_neg_tri(tk))


def _sb_decode_kernel(pt_ref, q_ref, bias_ref, *refs, group):
    k_refs, v_refs = refs[0:group], refs[group:2 * group]
    tri_ref, o_ref, c_scr, a_scr = refs[2 * group:]
    s = pl.program_id(0)
    g = pl.program_id(1)
    width = q_ref.shape[2]

    @pl.when(g == 0)
    def _():
        c_scr[...] = jnp.zeros_like(c_scr)
        a_scr[...] = jnp.zeros_like(a_scr)

    head = lax.broadcasted_iota(jnp.int32, (SB_HEADS, width), 0)
    lane = lax.broadcasted_iota(jnp.int32, (SB_HEADS, width), 1)
    own = (lane // SB_HEAD_DIM) == head
    qm = jnp.where(own, jnp.broadcast_to(q_ref[0], (SB_HEADS, width)), 0.0).astype(BF16)
    bias2 = jnp.concatenate([bias_ref[...] * LOG2E] * group, axis=0)
    z2 = jnp.concatenate([_dot(qm, k_refs[j][0].astype(BF16)) for j in range(group)], axis=0) + bias2
    sp2 = _softplus2(z2)
    hi = sp2.astype(BF16)
    lo = (sp2 - hi.astype(F32)).astype(BF16)
    tot = jnp.sum(sp2, axis=1, keepdims=True)
    carries = [c_scr[...]]
    for j in range(group):
        carries.append(carries[j] - tot[j * SB_HEADS:(j + 1) * SB_HEADS, :])
    c_scr[...] = carries[group]
    rem2 = _dot(hi, tri_ref[...]) + _dot(lo, tri_ref[...]) + jnp.concatenate(carries[0:group], axis=0)
    w = jnp.exp2((z2 - sp2) + rem2)
    acc = a_scr[...]
    for j in range(group):
        wj = w[j * SB_HEADS:(j + 1) * SB_HEADS, :].astype(BF16)
        acc = acc + _dot_nt(wj, v_refs[j][0].astype(BF16))
    a_scr[...] = acc

    @pl.when(g == pl.num_programs(1) - 1)
    def _():
        o = jnp.sum(jnp.where(own, acc, 0.0), axis=0, keepdims=True)
        for p in range(width // LANES):
            o_ref[p, pl.ds(s, 1), :] = o[:, p * LANES:(p + 1) * LANES]


def _sb_decode(q, cache_kt, cache_vt, page_table, sb_bias):
    n_seq, _, width = q.shape
    page = cache_kt.shape[2]
    npages = page_table.shape[1]
    group = SB_DECODE_GROUP if npages % SB_DECODE_GROUP == 0 else 1
    tri = _sb_neg_tri(page)

    def kv_spec(jj):
        return pl.BlockSpec((1, width, page), lambda b, g, pt: (pt[b, npages - 1 - (g * group + jj)], 0, 0))

    return pl.pallas_call(
        functools.partial(_sb_decode_kernel, group=group),
        grid_spec=pltpu.PrefetchScalarGridSpec(
            num_scalar_prefetch=1,
            grid=(n_seq, npages // group),
            in_specs=[
                pl.BlockSpec((1, 1, width), lambda b, g, pt: (b, 0, 0)),
                pl.BlockSpec((SB_HEADS, 1), lambda b, g, pt: (0, 0)),
                *[kv_spec(jj) for jj in range(group)],
                *[kv_spec(jj) for jj in range(group)],
                pl.BlockSpec((page, page), lambda b, g, pt: (0, 0)),
            ],
            out_specs=pl.BlockSpec((width // LANES, n_seq, LANES), lambda b, g, pt: (0, 0, 0)),
            scratch_shapes=[pltpu.VMEM((SB_HEADS, 1), F32), pltpu.VMEM((SB_HEADS, width), F32)],
        ),
        out_shape=jax.ShapeDtypeStruct((width // LANES, n_seq, LANES), F32),
        compiler_params=_params(("arbitrary", "arbitrary")),
        name="sb_decode",
    )(page_table, q, sb_bias.astype(F32).reshape(SB_HEADS, 1), *([cache_kt] * group), *([cache_vt] * group), tri)


def _ssd_prompt_kernel(xbc_ref, z_ref, dt_ref, dtt_ref, cw_ref, cb_ref, dtb_ref, dtbt_ref, alog_ref, alogt_ref,
                       dskip_ref, nw_ref, y_ref, conv_ref, ssm_ref, buf, st):
    c = pl.program_id(1)
    nc = pl.num_programs(1)
    L = xbc_ref.shape[0]
    pad = SUBLANES

    @pl.when(c == 0)
    def _():
        buf[0:pad, :] = jnp.zeros((pad, SSD_CONV_CH), F32)
        st[...] = jnp.zeros_like(st)

    buf[pad:pad + L, :] = xbc_ref[...]
    conv = cb_ref[...]
    for j in range(SSD_CONV):
        off = pad - (SSD_CONV - 1) + j
        conv = conv + cw_ref[j:j + 1, :] * buf[off:off + L, :]
    tail = buf[pad + L - (SSD_CONV - 1):pad + L, :]
    buf[pad - (SSD_CONV - 1):pad, :] = tail
    xa = _silu(conv)
    xs = xa[:, 0:SSD_WIDTH]
    gw = SSD_STATE
    bm = [xa[:, SSD_WIDTH + g * gw:SSD_WIDTH + (g + 1) * gw].astype(BF16) for g in range(SSD_GROUPS)]
    cm = [xa[:, SSD_WIDTH + (SSD_GROUPS + g) * gw:SSD_WIDTH + (SSD_GROUPS + g + 1) * gw].astype(BF16)
          for g in range(SSD_GROUPS)]

    dt = _softplus(dt_ref[...] + dtb_ref[...])
    dtt = _softplus(dtt_ref[...] + dtbt_ref[...])
    a2 = -jnp.exp(alog_ref[...]) * LOG2E
    at2 = -jnp.exp(alogt_ref[...]) * LOG2E
    row = lax.broadcasted_iota(jnp.int32, (L, L), 0)
    col = lax.broadcasted_iota(jnp.int32, (L, L), 1)
    lower = col <= row
    lower_b = lower.astype(BF16)
    upper_b = (row <= col).astype(BF16)
    cum = sum(_dot(lower_b, part) for part in _split3(dt * a2))
    cumt = sum(_dot(part, upper_b) for part in _split3(dtt * at2))
    last = cum[L - 1:L, :]
    wst = jnp.exp2(last - cum) * dt
    ecum = jnp.exp2(cum)
    elast = jnp.exp2(last)

    lane = lax.broadcasted_iota(jnp.int32, (L, LANES), 1)
    first = lane < SSD_HEAD_DIM
    first_row = first[0:1, :]
    heads_per_group = SSD_HEADS // SSD_GROUPS
    cb = [jnp.where(lower, _dot_nt(cm[g], bm[g]), 0.0) for g in range(SSD_GROUPS)]
    ys = []
    for p in range(SSD_WIDTH // LANES):
        g = (2 * p) // heads_per_group
        xp = xs[:, p * LANES:(p + 1) * LANES]
        xpb = xp.astype(BF16)
        yi = []
        for h in (2 * p, 2 * p + 1):
            seg = cum[:, h:h + 1] - cumt[h:h + 1, :]
            dec = jnp.exp2(jnp.minimum(seg, 0.0)) * dtt[h:h + 1, :]
            yi.append(_dot((cb[g] * dec).astype(BF16), xpb))
        y_intra = jnp.where(first, yi[0], yi[1])
        stp = st[:, p * LANES:(p + 1) * LANES]
        ec = jnp.where(first, ecum[:, 2 * p:2 * p + 1], ecum[:, 2 * p + 1:2 * p + 2])
        y_inter = _dot(cm[g], stp.astype(BF16)) * ec
        wp = jnp.where(first, wst[:, 2 * p:2 * p + 1], wst[:, 2 * p + 1:2 * p + 2])
        el = jnp.where(first_row, elast[:, 2 * p:2 * p + 1], elast[:, 2 * p + 1:2 * p + 2])
        st[:, p * LANES:(p + 1) * LANES] = el * stp + _dot_tn(bm[g], (xp * wp).astype(BF16))
        ys.append(y_intra + y_inter + dskip_ref[:, p * LANES:(p + 1) * LANES] * xp)
    y = jnp.concatenate(ys, axis=1)
    y_ref[...] = _rmsnorm(y * _silu(z_ref[...].astype(F32)), nw_ref[...]).astype(y_ref.dtype)

    @pl.when(c == nc - 1)
    def _():
        conv_ref[0] = tail
        ssm_ref[0] = st[...].T


def _ssd_prompt(xbc, z, dt, dtt, conv_w, conv_b, dt_bias, a_log, d_skip, ssd_norm, bsz, t):
    m = xbc.shape[0]
    L = SSD_CHUNK
    nc = t // L
    rows = lambda b, c: (b * nc + c, 0)
    full = lambda b, c: (0, 0)
    vec = lambda v: v.astype(F32).reshape(1, -1)
    colv = lambda v: v.astype(F32).reshape(-1, 1)
    y, conv, ssm = pl.pallas_call(
        _ssd_prompt_kernel,
        grid=(bsz, nc),
        in_specs=[
            pl.BlockSpec((L, SSD_CONV_CH), rows),
            pl.BlockSpec((L, SSD_WIDTH), rows),
            pl.BlockSpec((L, SSD_HEADS), rows),
            pl.BlockSpec((SSD_HEADS, L), lambda b, c: (0, b * nc + c)),
            pl.BlockSpec((SSD_CONV, SSD_CONV_CH), full),
            pl.BlockSpec((1, SSD_CONV_CH), full),
            pl.BlockSpec((1, SSD_HEADS), full),
            pl.BlockSpec((SSD_HEADS, 1), full),
            pl.BlockSpec((1, SSD_HEADS), full),
            pl.BlockSpec((SSD_HEADS, 1), full),
            pl.BlockSpec((1, SSD_WIDTH), full),
            pl.BlockSpec((1, SSD_WIDTH), full),
        ],
        out_specs=[
            pl.BlockSpec((L, SSD_WIDTH), rows),
            pl.BlockSpec((1, SSD_CONV - 1, SSD_CONV_CH), lambda b, c: (b, 0, 0)),
            pl.BlockSpec((1, SSD_WIDTH, SSD_STATE), lambda b, c: (b, 0, 0)),
        ],
        out_shape=[
            jax.ShapeDtypeStruct((m, SSD_WIDTH), BF16),
            jax.ShapeDtypeStruct((bsz, SSD_CONV - 1, SSD_CONV_CH), F32),
            jax.ShapeDtypeStruct((bsz, SSD_WIDTH, SSD_STATE), F32),
        ],
        scratch_shapes=[pltpu.VMEM((L + SUBLANES, SSD_CONV_CH), F32), pltpu.VMEM((SSD_STATE, SSD_WIDTH), F32)],
        compiler_params=_params(("parallel", "arbitrary")),
        name="ssd_prompt",
    )(xbc, z, dt, dtt, conv_w.astype(F32), vec(conv_b), vec(dt_bias), colv(dt_bias), vec(a_log), colv(a_log),
      vec(jnp.repeat(d_skip, SSD_HEAD_DIM)), vec(ssd_norm))
    return y, conv, ssm.reshape(bsz, SSD_HEADS, SSD_HEAD_DIM, SSD_STATE)


def _ssd_step_kernel(xbc_ref, z_ref, dt_ref, sconv_ref, sssm_ref, cw_ref, cb_ref, dtb_ref, alog_ref, dskip_ref,
                     nw_ref, y_ref, conv_ref, ssm_ref):
    lane_head = lax.broadcasted_iota(jnp.int32, (1, SSD_WIDTH), 1) // SSD_HEAD_DIM
    rows_per_group = SSD_WIDTH // SSD_GROUPS
    row = lax.broadcasted_iota(jnp.int32, (SSD_WIDTH, SSD_STATE), 0)
    a = -jnp.exp(alog_ref[...])
    for s in range(xbc_ref.shape[0]):
        xr = xbc_ref[s]
        cs = sconv_ref[s]
        conv = cb_ref[...] + cw_ref[SSD_CONV - 1:SSD_CONV, :] * xr
        for j in range(SSD_CONV - 1):
            conv = conv + cw_ref[j:j + 1, :] * cs[j:j + 1, :]
        conv_ref[s] = jnp.concatenate([cs[1:SSD_CONV - 1, :], xr], axis=0)
        xa = _silu(conv)
        xs = xa[:, 0:SSD_WIDTH]
        dt = _softplus(dt_ref[s] + dtb_ref[...])
        da = jnp.exp(dt * a)
        dt_w = jnp.zeros((1, SSD_WIDTH), F32)
        da_w = jnp.zeros((1, SSD_WIDTH), F32)
        for h in range(SSD_HEADS):
            dt_w = jnp.where(lane_head == h, dt[:, h:h + 1], dt_w)
            da_w = jnp.where(lane_head == h, da[:, h:h + 1], da_w)
        dtx_col = _row_to_col(dt_w * xs)
        da_col = _row_to_col(da_w)
        b_rows = jnp.zeros((SSD_WIDTH, SSD_STATE), F32)
        c_rows = jnp.zeros((SSD_WIDTH, SSD_STATE), F32)
        for g in range(SSD_GROUPS):
            sel = (row // rows_per_group) == g
            b0 = SSD_WIDTH + g * SSD_STATE
            c0 = SSD_WIDTH + (SSD_GROUPS + g) * SSD_STATE
            b_rows = jnp.where(sel, xa[:, b0:b0 + SSD_STATE], b_rows)
            c_rows = jnp.where(sel, xa[:, c0:c0 + SSD_STATE], c_rows)
        new = da_col * sssm_ref[s] + dtx_col * b_rows
        ssm_ref[s] = new
        y = _col_to_row(jnp.sum(new * c_rows, axis=1, keepdims=True)) + dskip_ref[...] * xs
        y_ref[s] = _rmsnorm(y * _silu(z_ref[s].astype(F32)), nw_ref[...])


def _ssd_step(xbc, z, dt, state_conv, state_ssm, conv_w, conv_b, dt_bias, a_log, d_skip, ssd_norm):
    n = xbc.shape[0]
    ns = STEP_SEQS if n % STEP_SEQS == 0 else 1
    vec = lambda v: v.astype(F32).reshape(1, -1)
    per = lambda i: (i, 0, 0)
    full = lambda i: (0, 0)
    y, conv, ssm = pl.pallas_call(
        _ssd_step_kernel,
        grid=(n // ns,),
        in_specs=[
            pl.BlockSpec((ns, 1, SSD_CONV_CH), per),
            pl.BlockSpec((ns, 1, SSD_WIDTH), per),
            pl.BlockSpec((ns, 1, SSD_HEADS), per),
            pl.BlockSpec((ns, SSD_CONV - 1, SSD_CONV_CH), per),
            pl.BlockSpec((ns, SSD_WIDTH, SSD_STATE), per),
            pl.BlockSpec((SSD_CONV, SSD_CONV_CH), full),
            pl.BlockSpec((1, SSD_CONV_CH), full),
            pl.BlockSpec((1, SSD_HEADS), full),
            pl.BlockSpec((1, SSD_HEADS), full),
            pl.BlockSpec((1, SSD_WIDTH), full),
            pl.BlockSpec((1, SSD_WIDTH), full),
        ],
        out_specs=[
            pl.BlockSpec((ns, 1, SSD_WIDTH), per),
            pl.BlockSpec((ns, SSD_CONV - 1, SSD_CONV_CH), per),
            pl.BlockSpec((ns, SSD_WIDTH, SSD_STATE), per),
        ],
        out_shape=[
            jax.ShapeDtypeStruct((n, 1, SSD_WIDTH), F32),
            jax.ShapeDtypeStruct((n, SSD_CONV - 1, SSD_CONV_CH), F32),
            jax.ShapeDtypeStruct((n, SSD_WIDTH, SSD_STATE), F32),
        ],
        compiler_params=_params(("parallel",)),
        name="ssd_step",
    )(xbc.reshape(n, 1, SSD_CONV_CH), z.reshape(n, 1, SSD_WIDTH), dt.reshape(n, 1, SSD_HEADS),
      state_conv.astype(F32), state_ssm.astype(F32).reshape(n, SSD_WIDTH, SSD_STATE), conv_w.astype(F32),
      vec(conv_b), vec(dt_bias), vec(a_log), vec(jnp.repeat(d_skip, SSD_HEAD_DIM)), vec(ssd_norm))
    return y.reshape(n, SSD_WIDTH), conv, ssm.reshape(n, SSD_HEADS, SSD_HEAD_DIM, SSD_STATE)


def _even_mix(osb_ref, y_ref):
    return jnp.concatenate([osb_ref[p] for p in range(osb_ref.shape[0])] + [y_ref[...]], axis=1)


def _odd_mix(o_ref, g_ref, nw_ref):
    o = jnp.concatenate([o_ref[h] for h in range(o_ref.shape[0])], axis=1).astype(F32)
    return (_rmsnorm(o, nw_ref[...]) * _silu(g_ref[...].astype(F32))).astype(BF16)


def _mix_mlp_kernel(*refs, mix_fn, n_mix, final_norm):
    mix_refs = refs[0:n_mix]
    x_ref, wout_ref, nw_ref, w1_ref, w2_ref, fw_ref, o_ref, hn_scr, acc_scr = refs[n_mix:]
    j = pl.program_id(1)

    @pl.when(j == 0)
    def _():
        h1 = x_ref[...] + _dot(mix_fn(*mix_refs), wout_ref[...])
        o_ref[...] = h1
        hn_scr[...] = _rmsnorm(h1, nw_ref[...]).astype(BF16)

    h = jnp.maximum(_dot(hn_scr[...], w1_ref[...]), 0.0)
    a = (h * h).astype(BF16)

    @pl.when(j == 0)
    def _():
        acc_scr[...] = _dot(a, w2_ref[...])

    @pl.when(j > 0)
    def _():
        acc_scr[...] += _dot(a, w2_ref[...])

    @pl.when(j == pl.num_programs(1) - 1)
    def _():
        out = o_ref[...] + acc_scr[...]
        o_ref[...] = _rmsnorm(out, fw_ref[...]) if final_norm else out


def _mix_mlp(mix_fn, mix_args, mix_specs, x, w_out, norm_w, w1_all, w2_all, layer, final_w, tm, tf):
    m, d = x.shape
    f = w1_all.shape[2]
    final_norm = final_w is not None
    fw = (final_w if final_norm else jnp.ones((d,), F32)).astype(F32).reshape(1, d)
    const = lambda i, j: (0, 0)
    return pl.pallas_call(
        functools.partial(_mix_mlp_kernel, mix_fn=mix_fn, n_mix=len(mix_args), final_norm=final_norm),
        grid=(m // tm, f // tf),
        in_specs=[
            *mix_specs,
            pl.BlockSpec((tm, d), lambda i, j: (i, 0)),
            pl.BlockSpec(w_out.shape, const),
            pl.BlockSpec((1, d), const),
            pl.BlockSpec((None, d, tf), lambda i, j: (layer, 0, j)),
            pl.BlockSpec((None, tf, d), lambda i, j: (layer, j, 0)),
            pl.BlockSpec((1, d), const),
        ],
        out_specs=pl.BlockSpec((tm, d), lambda i, j: (i, 0)),
        out_shape=jax.ShapeDtypeStruct((m, d), F32),
        scratch_shapes=[pltpu.VMEM((tm, d), BF16), pltpu.VMEM((tm, d), F32)],
        compiler_params=_params(("parallel", "arbitrary")),
        name="mix_mlp",
    )(*mix_args, x, w_out.astype(BF16), norm_w.astype(F32).reshape(1, d), w1_all, w2_all, fw)


def _even_mix_mlp(osb, y, x, w_out, norm_w, w1_all, w2_all, layer, final_w, tm, tf):
    specs = [pl.BlockSpec((osb.shape[0], tm, LANES), lambda i, j: (0, i, 0)),
             pl.BlockSpec((tm, SSD_WIDTH), lambda i, j: (i, 0))]
    return _mix_mlp(_even_mix, (osb, y), specs, x, w_out, norm_w, w1_all, w2_all, layer, final_w, tm, tf)


def _odd_mix_mlp(o, g, hg_norm, x, w_out, norm_w, w1_all, w2_all, layer, final_w, tm, tf):
    nh = o.shape[0]
    specs = [pl.BlockSpec((nh, tm, HG_VAL), lambda i, j: (0, i, 0)),
             pl.BlockSpec((tm, nh * HG_VAL), lambda i, j: (i, 0)),
             pl.BlockSpec((1, nh * HG_VAL), lambda i, j: (0, 0))]
    args = (o, g, hg_norm.astype(F32).reshape(1, -1))
    return _mix_mlp(_odd_mix, args, specs, x, w_out, norm_w, w1_all, w2_all, layer, final_w, tm, tf)


def _odd_proj_kernel(x_ref, nw_ref, w_ref, q_ref, f_ref, v_ref, g_ref):
    hn = _rmsnorm(x_ref[...], nw_ref[...]).astype(BF16)
    width = HG_HEADS * HG_KEY
    per_chunk = PROJ_COLS // HG_KEY
    for i, ref in enumerate((q_ref, f_ref, v_ref)):
        for c in range(width // PROJ_COLS):
            y = _dot(hn, w_ref[:, i * width + c * PROJ_COLS:i * width + (c + 1) * PROJ_COLS])
            for h in range(per_chunk):
                ref[c * per_chunk + h] = y[:, h * HG_KEY:(h + 1) * HG_KEY]
    for c in range(width // PROJ_COLS):
        g_ref[:, c * PROJ_COLS:(c + 1) * PROJ_COLS] = _dot(
            hn, w_ref[:, 3 * width + c * PROJ_COLS:3 * width + (c + 1) * PROJ_COLS]).astype(g_ref.dtype)


def _odd_proj(x, norm_w, w_in, tm):
    m, d = x.shape
    width = HG_HEADS * HG_KEY
    heads = lambda i: (0, i, 0)
    per_head = jax.ShapeDtypeStruct((HG_HEADS, m, HG_KEY), F32)
    return pl.pallas_call(
        _odd_proj_kernel,
        grid=(m // tm,),
        in_specs=[
            pl.BlockSpec((tm, d), lambda i: (i, 0)),
            pl.BlockSpec((1, d), lambda i: (0, 0)),
            pl.BlockSpec(w_in.shape, lambda i: (0, 0)),
        ],
        out_specs=[
            pl.BlockSpec((HG_HEADS, tm, HG_KEY), heads),
            pl.BlockSpec((HG_HEADS, tm, HG_KEY), heads),
            pl.BlockSpec((HG_HEADS, tm, HG_KEY), heads),
            pl.BlockSpec((tm, width), lambda i: (i, 0)),
        ],
        out_shape=[per_head, per_head, per_head, jax.ShapeDtypeStruct((m, width), BF16)],
        compiler_params=_params(("parallel",)),
        name="odd_proj",
    )(x, norm_w.astype(F32).reshape(1, d), w_in.astype(BF16))


def _hgrn_lower_bound(raw, layer):
    e = jnp.exp(raw - jnp.max(raw, axis=0, keepdims=True))
    p = e / jnp.sum(e, axis=0, keepdims=True)
    lb = jnp.zeros_like(p[0])
    for l in range(1, layer + 1):
        lb = lb + p[l]
    return lb


def _hgrn_gates(fpre, lb):
    sig, nsig = _sigmoid_pair(fpre)
    return lb + (1.0 - lb) * sig, (1.0 - lb) * nsig


def _hgrn_prompt_kernel(q_ref, f_ref, v_ref, lb_ref, o_ref, s_ref, st, b_scr, k_scr, od_scr, *, layer):
    c = pl.program_id(1)
    nc = pl.num_programs(1)
    L = q_ref.shape[1]
    nblk = L // HG_DIAG

    @pl.when(c == 0)
    def _():
        st[...] = jnp.zeros_like(st)

    row = lax.broadcasted_iota(jnp.int32, (L, L), 0)
    col = lax.broadcasted_iota(jnp.int32, (L, L), 1)
    lower_b = (col <= row).astype(BF16)
    pos = lax.broadcasted_iota(jnp.int32, (L, HG_KEY), 0)

    def heads(n, _):
        hs = [n * HG_HEAD_GROUP + e for e in range(HG_HEAD_GROUP)]
        every = range(HG_HEAD_GROUP)
        q = [q_ref[h] for h in hs]
        vb = [v_ref[h].astype(BF16) for h in hs]
        gates = [_hgrn_gates(f_ref[h], _hgrn_lower_bound(lb_ref[:, h], layer)) for h in hs]
        kin = [gk[1] for gk in gates]
        b = [sum(_dot(lower_b, part) for part in _split3(jnp.log(gk[0]) * LOG2E)) for gk in gates]
        for e in every:
            b_scr[e] = b[e]
            k_scr[e] = kin[e]
        stt = [st[h] for h in hs]
        o = [_dot_nt((q[e] * jnp.exp2(b[e])).astype(BF16), stt[e].astype(BF16)) for e in every]
        b_last = [x[L - 1:L, :] for x in b]
        for e in every:
            st[hs[e]] = stt[e] * jnp.exp2(b_last[e]) + _dot_tn(
                vb[e], (kin[e] * jnp.exp2(b_last[e] - b[e])).astype(BF16))

        att = [jnp.zeros((L, L), F32) for _ in every]
        g = L
        while g > HG_DIAG:
            half = g // 2
            late = (pos & (g - 1)) >= half
            pair = ((row ^ col) < g) & ((row & (g - 1)) >= half) & ((col & (g - 1)) < half)
            mid = [jnp.broadcast_to(x.reshape(L // g, g, HG_KEY)[:, half - 1:half, :],
                                    (L // g, g, HG_KEY)).reshape(L, HG_KEY) for x in b]
            r = [(jnp.where(late, q[e], kin[e]) * jnp.exp2(-jnp.abs(b[e] - mid[e]))).astype(BF16) for e in every]
            att = [jnp.where(pair, _dot_nt(r[e], r[e]), att[e]) for e in every]
            g = half
        o = [o[e] + _dot(att[e].astype(BF16), vb[e]) for e in every]

        at = lambda ref, i: ref[pl.ds(i, nblk, stride=HG_DIAG), :]
        for e in every:
            bs = [at(b_scr.at[e], i) for i in range(HG_DIAG)]
            ks = [at(k_scr.at[e], i) for i in range(HG_DIAG)]
            vs = [at(v_ref.at[hs[e]], i) for i in range(HG_DIAG)]
            for i in range(HG_DIAG):
                qi = at(q_ref.at[hs[e]], i)
                acc = jnp.zeros((nblk, HG_VAL), F32)
                for j in range(i + 1):
                    a = jnp.sum(qi * ks[j] * jnp.exp2(bs[i] - bs[j]), axis=1, keepdims=True)
                    acc = acc + a * vs[j]
                od_scr[e, pl.ds(i, nblk, stride=HG_DIAG), :] = acc
        for e in every:
            o_ref[hs[e]] = (o[e] + od_scr[e]).astype(o_ref.dtype)
        return 0

    lax.fori_loop(0, q_ref.shape[0] // HG_HEAD_GROUP, heads, 0)

    @pl.when(c == nc - 1)
    def _():
        for h in range(s_ref.shape[1]):
            s_ref[0, h] = st[h].T


def _hgrn_prompt(q, f, v, lb_raw, layer, bsz, t):
    nh, m, _ = q.shape
    L = HG_CHUNK
    nc = t // L
    blk = lambda b, c: (0, b * nc + c, 0)
    depth = lb_raw.shape[0]
    return pl.pallas_call(
        functools.partial(_hgrn_prompt_kernel, layer=layer),
        grid=(bsz, nc),
        in_specs=[
            pl.BlockSpec((nh, L, HG_KEY), blk),
            pl.BlockSpec((nh, L, HG_KEY), blk),
            pl.BlockSpec((nh, L, HG_VAL), blk),
            pl.BlockSpec((depth, nh, 1, HG_KEY), lambda b, c: (0, 0, 0, 0)),
        ],
        out_specs=[
            pl.BlockSpec((nh, L, HG_VAL), blk),
            pl.BlockSpec((1, nh, HG_KEY, HG_VAL), lambda b, c: (b, 0, 0, 0)),
        ],
        out_shape=[
            jax.ShapeDtypeStruct((nh, m, HG_VAL), BF16),
            jax.ShapeDtypeStruct((bsz, nh, HG_KEY, HG_VAL), F32),
        ],
        scratch_shapes=[
            pltpu.VMEM((nh, HG_VAL, HG_KEY), F32),
            pltpu.VMEM((HG_HEAD_GROUP, L, HG_KEY), F32),
            pltpu.VMEM((HG_HEAD_GROUP, L, HG_KEY), F32),
            pltpu.VMEM((HG_HEAD_GROUP, L, HG_VAL), F32),
        ],
        compiler_params=_params(("parallel", "arbitrary")),
        name="hgrn_prompt",
    )(q, f, v, lb_raw.astype(F32).reshape(depth, nh, 1, HG_KEY))


def _hgrn_step_kernel(q_ref, f_ref, v_ref, lb_ref, s_ref, o_ref, snew_ref, *, layer):
    ns = s_ref.shape[0]
    for h in range(q_ref.shape[0]):
        lb = _hgrn_lower_bound(lb_ref[:, h], layer)
        for s in range(ns):
            i = pl.program_id(0) * ns + s
            q = q_ref[h, pl.ds(i, 1), :]
            v = v_ref[h, pl.ds(i, 1), :]
            fg, kin = _hgrn_gates(f_ref[h, pl.ds(i, 1), :], lb)
            new = _row_to_col(fg) * s_ref[s, h] + _row_to_col(kin) * v
            snew_ref[s, h] = new
            o_ref[h, pl.ds(i, 1), :] = jnp.sum(_row_to_col(q) * new, axis=0, keepdims=True)


def _hgrn_step(q, f, v, lb_raw, state, layer):
    nh, n, _ = q.shape
    ns = STEP_SEQS if n % STEP_SEQS == 0 else 1
    depth = lb_raw.shape[0]
    whole = lambda i: (0, 0, 0)
    return pl.pallas_call(
        functools.partial(_hgrn_step_kernel, layer=layer),
        grid=(n // ns,),
        in_specs=[
            pl.BlockSpec((nh, n, HG_KEY), whole),
            pl.BlockSpec((nh, n, HG_KEY), whole),
            pl.BlockSpec((nh, n, HG_VAL), whole),
            pl.BlockSpec((depth, nh, 1, HG_KEY), lambda i: (0, 0, 0, 0)),
            pl.BlockSpec((ns, nh, HG_KEY, HG_VAL), lambda i: (i, 0, 0, 0)),
        ],
        out_specs=[
            pl.BlockSpec((nh, n, HG_VAL), whole),
            pl.BlockSpec((ns, nh, HG_KEY, HG_VAL), lambda i: (i, 0, 0, 0)),
        ],
        out_shape=[
            jax.ShapeDtypeStruct((nh, n, HG_VAL), F32),
            jax.ShapeDtypeStruct((n, nh, HG_KEY, HG_VAL), F32),
        ],
        compiler_params=_params(("arbitrary",)),
        name="hgrn_step",
    )(q, f, v, lb_raw.astype(F32).reshape(depth, nh, 1, HG_KEY), state.astype(F32))


def _row_tile(m, want):
    return want if m % want == 0 else m


def kernel(x_prompt, x_sample, cache_k, cache_v, page_table, state_conv, state_ssm, state_hgrn, norm_mix, norm_ffn,
           norm_final, w_in_even, sb_bias, conv_w, conv_b, dt_bias, a_log, d_skip, ssd_norm, w_out_even, w_in_odd,
           hg_lb_raw, hg_norm, w_out_odd, w_ff1, w_ff2):
    bsz, t, d = x_prompt.shape
    n_seq = x_sample.shape[0]
    depth = norm_mix.shape[0]
    mp = bsz * t
    hp = x_prompt.reshape(mp, d)
    hs = x_sample.reshape(n_seq, d)
    tmp = _row_tile(mp, 512)
    tmm = _row_tile(mp, 1024)
    tf = 1024
    outs = {k: [] for k in ("kp", "vp", "ks", "vs", "cp", "cs", "sp", "ss", "gp", "gs")}
    w1_all = w_ff1.astype(BF16)
    w2_all = w_ff2.astype(BF16)

    for layer in range(depth):
        li = layer // 2
        fw = norm_final if layer == depth - 1 else None
        mlp_w = (norm_ffn[layer], w1_all, w2_all, layer, fw)
        if layer % 2 == 0:
            ssd_w = (conv_w[li], conv_b[li], dt_bias[li], a_log[li], d_skip[li], ssd_norm[li])
            q, k, v, kf, vf, z, xbc, dt, dtt = _even_proj(hp, norm_mix[layer], w_in_even[li], tmp, seq_len=t)
            osb = _sb_prompt(q, k, v, sb_bias[li], bsz, t)
            y, cp, sp = _ssd_prompt(xbc, z, dt, dtt, *ssd_w, bsz, t)
            hp = _even_mix_mlp(osb, y, hp, w_out_even[li], *mlp_w, tmm, tf)
            rows_view = lambda a: jnp.transpose(a.reshape(bsz, SB_HEADS, SB_HEAD_DIM, t), (0, 3, 1, 2))
            outs["kp"].append(rows_view(kf))
            outs["vp"].append(rows_view(vf))
            outs["cp"].append(cp)
            outs["sp"].append(sp)
            q, k, v, kf, vf, z, xbc, dt, dtt = _even_proj(hs, norm_mix[layer], w_in_even[li], n_seq)
            qs = q.astype(F32).transpose(1, 0, 2).reshape(n_seq, 1, SB_WIDTH)
            n_phys, page = cache_k.shape[1], cache_k.shape[2]
            to_lanes = lambda c: jnp.transpose(c, (0, 2, 3, 1)).reshape(n_phys, SB_WIDTH, page)
            osb = _sb_decode(qs, to_lanes(cache_k[li]), to_lanes(cache_v[li]), page_table, sb_bias[li])
            y, cs, ss = _ssd_step(xbc, z, dt, state_conv[li], state_ssm[li], *ssd_w)
            hs = _even_mix_mlp(osb.astype(BF16), y.astype(BF16), hs, w_out_even[li], *mlp_w, n_seq, tf)
            outs["ks"].append(kf.reshape(n_seq, 1, SB_HEADS, SB_HEAD_DIM))
            outs["vs"].append(vf.reshape(n_seq, 1, SB_HEADS, SB_HEAD_DIM))
            outs["cs"].append(cs)
            outs["ss"].append(ss)
        else:
            q, f, v, g = _odd_proj(hp, norm_mix[layer], w_in_odd[li], tmp)
            o, gp = _hgrn_prompt(q, f, v, hg_lb_raw, layer, bsz, t)
            hp = _odd_mix_mlp(o, g, hg_norm[li], hp, w_out_odd[li], *mlp_w, tmm, tf)
            outs["gp"].append(gp)
            q, f, v, g = _odd_proj(hs, norm_mix[layer], w_in_odd[li], n_seq)
            o, gs = _hgrn_step(q, f, v, hg_lb_raw, state_hgrn[li], layer)
            hs = _odd_mix_mlp(o.astype(BF16), g, hg_norm[li], hs, w_out_odd[li], *mlp_w, n_seq, tf)
            outs["gs"].append(gs)

    y_prompt = hp.reshape(bsz, t, d)
    y_sample = hs.reshape(n_seq, 1, d)
    st = lambda key: jnp.stack(outs[key])
    return (y_prompt, y_sample, st("kp"), st("vp"), st("ks"), st("vs"), st("cp"), st("cs"), st("sp"), st("ss"),
            st("gp"), st("gs"))
```

```python
import functools

import jax
import jax.numpy as jnp
from jax import lax
from jax.experimental import pallas as pl
from jax.experimental.pallas import tpu as pltpu

F32 = jnp.float32
BF16 = jnp.bfloat16

EPS = 1e-6
LOG2E = 1.4426950408889634
LN2 = 0.6931471805599453
SB_HEADS = 8
SB_HEAD_DIM = 64
SB_WIDTH = SB_HEADS * SB_HEAD_DIM
SSD_HEADS = 8
SSD_HEAD_DIM = 64
SSD_WIDTH = SSD_HEADS * SSD_HEAD_DIM
SSD_STATE = 128
SSD_GROUPS = 2
SSD_CONV = 4
SSD_CONV_CH = SSD_WIDTH + 2 * SSD_GROUPS * SSD_STATE
HG_HEADS = 8
HG_KEY = 128
HG_VAL = 128

LANES = 128
SUBLANES = 8
VMEM_LIMIT_BYTES = 52 * 1024 * 1024

SB_TILE = 256
SB_QUERY_TILES = 8
SB_DECODE_GROUP = 16
SSD_CHUNK = 128
HG_CHUNK = 128
HG_DIAG = 4
HG_HEAD_GROUP = 8
STEP_SEQS = 4
PROJ_COLS = 512


def _params(sem):
    return pltpu.CompilerParams(dimension_semantics=sem, vmem_limit_bytes=VMEM_LIMIT_BYTES)


def _dot(a, b):
    return jnp.dot(a, b, preferred_element_type=F32)


def _dot_nt(a, b):
    return lax.dot_general(a, b, (((1,), (1,)), ((), ())), preferred_element_type=F32)


def _dot_tn(a, b):
    return lax.dot_general(a, b, (((0,), (0,)), ((), ())), preferred_element_type=F32)


def _split3(x):
    hi = x.astype(BF16)
    r = x - hi.astype(F32)
    mid = r.astype(BF16)
    lo = (r - mid.astype(F32)).astype(BF16)
    return hi, mid, lo


def _rmsnorm(x, w):
    return x * lax.rsqrt(jnp.mean(x * x, axis=-1, keepdims=True) + EPS) * w


def _softplus(x):
    return jnp.maximum(x, 0.0) + jnp.log1p(jnp.exp(-jnp.abs(x)))


def _sigmoid_pair(x):
    e = jnp.exp(-jnp.abs(x))
    r = 1.0 / (1.0 + e)
    big, small = r, e * r
    pos = x >= 0
    return jnp.where(pos, big, small), jnp.where(pos, small, big)


def _silu(x):
    return x / (1.0 + jnp.exp(-x))


def _row_to_col(row):
    n = row.shape[1]
    eye = lax.broadcasted_iota(jnp.int32, (n, n), 0) == lax.broadcasted_iota(jnp.int32, (n, n), 1)
    return jnp.sum(jnp.where(eye, jnp.broadcast_to(row, (n, n)), 0.0), axis=1, keepdims=True)


def _col_to_row(col):
    n = col.shape[0]
    eye = lax.broadcasted_iota(jnp.int32, (n, n), 0) == lax.broadcasted_iota(jnp.int32, (n, n), 1)
    return jnp.sum(jnp.where(eye, jnp.broadcast_to(col, (n, n)), 0.0), axis=0, keepdims=True)


def _even_proj_kernel(x_ref, nw_ref, w_ref, wdt_ref, wdtt_ref,
                      q_ref, k_ref, v_ref, kf_ref, vf_ref, z_ref, xbc_ref, dt_ref, dtt_ref, *, token_minor):
    hn = _rmsnorm(x_ref[...], nw_ref[...]).astype(BF16)
    scale = LOG2E * SB_HEAD_DIM ** -0.5
    npair = SB_WIDTH // LANES
    q = _dot(hn, w_ref[:, 0:SB_WIDTH]) * scale
    for p in range(npair):
        q_ref[p] = q[:, p * LANES:(p + 1) * LANES].astype(BF16)
    for i, (pair_ref, full_ref) in enumerate(((k_ref, kf_ref), (v_ref, vf_ref))):
        y = _dot(hn, w_ref[:, (i + 1) * SB_WIDTH:(i + 2) * SB_WIDTH])
        if token_minor:
            full_ref[0] = y.T
        else:
            full_ref[...] = y
        for p in range(npair):
            pair_ref[p] = y[:, p * LANES:(p + 1) * LANES].astype(BF16)
    z0 = 3 * SB_WIDTH
    z_ref[...] = _dot(hn, w_ref[:, z0:z0 + SSD_WIDTH]).astype(z_ref.dtype)
    x0 = z0 + SSD_WIDTH
    for c in range(SSD_CONV_CH // PROJ_COLS):
        xbc_ref[:, c * PROJ_COLS:(c + 1) * PROJ_COLS] = _dot(
            hn, w_ref[:, x0 + c * PROJ_COLS:x0 + (c + 1) * PROJ_COLS])
    dt_ref[...] = _dot(hn, wdt_ref[...])[:, 0:SSD_HEADS]
    dtt_ref[...] = _dot_nt(wdtt_ref[...], hn)


def _even_proj(x, norm_w, w_in, tm, seq_len=None):
    m, d = x.shape
    npair = SB_WIDTH // LANES
    wb = w_in.astype(BF16)
    n_main = 3 * SB_WIDTH + SSD_WIDTH + SSD_CONV_CH
    w_main = wb
    w_dt = jnp.pad(wb[:, n_main:], ((0, 0), (0, LANES - SSD_HEADS)))
    w_dtt = wb[:, n_main:].T
    full = lambda i: (0, 0)
    rows = lambda i: (i, 0)
    hp = lambda i: (0, i, 0)
    token_minor = seq_len is not None
    if token_minor:
        nt = seq_len // tm
        kv_spec = pl.BlockSpec((1, SB_WIDTH, tm), lambda i: (i // nt, 0, i % nt))
        kv_shape = jax.ShapeDtypeStruct((m // seq_len, SB_WIDTH, seq_len), F32)
    else:
        kv_spec = pl.BlockSpec((tm, SB_WIDTH), rows)
        kv_shape = jax.ShapeDtypeStruct((m, SB_WIDTH), F32)
    return pl.pallas_call(
        functools.partial(_even_proj_kernel, token_minor=token_minor),
        grid=(m // tm,),
        in_specs=[
            pl.BlockSpec((tm, d), rows),
            pl.BlockSpec((1, d), full),
            pl.BlockSpec(wb.shape, full),
            pl.BlockSpec((d, LANES), full),
            pl.BlockSpec((SSD_HEADS, d), full),
        ],
        out_specs=[
            pl.BlockSpec((npair, tm, LANES), hp),
            pl.BlockSpec((npair, tm, LANES), hp),
            pl.BlockSpec((npair, tm, LANES), hp),
            kv_spec,
            kv_spec,
            pl.BlockSpec((tm, SSD_WIDTH), rows),
            pl.BlockSpec((tm, SSD_CONV_CH), rows),
            pl.BlockSpec((tm, SSD_HEADS), rows),
            pl.BlockSpec((SSD_HEADS, tm), lambda i: (0, i)),
        ],
        out_shape=[
            jax.ShapeDtypeStruct((npair, m, LANES), BF16),
            jax.ShapeDtypeStruct((npair, m, LANES), BF16),
            jax.ShapeDtypeStruct((npair, m, LANES), BF16),
            kv_shape,
            kv_shape,
            jax.ShapeDtypeStruct((m, SSD_WIDTH), BF16),
            jax.ShapeDtypeStruct((m, SSD_CONV_CH), F32),
            jax.ShapeDtypeStruct((m, SSD_HEADS), F32),
            jax.ShapeDtypeStruct((SSD_HEADS, m), F32),
        ],
        compiler_params=_params(("parallel",)),
        name="even_proj",
    )(x, norm_w.reshape(1, d), w_main, w_dt, w_dtt)


def _sb_neg_tri(n):
    j = jnp.arange(n)[:, None]
    s = jnp.arange(n)[None, :]
    return -(j > s).astype(BF16)


def _softplus2(z2):
    return jnp.log2(1.0 + jnp.exp2(-jnp.abs(z2))) + jnp.maximum(z2, 0.0)


def _sb_tiles(qs, kbs, vbs, ntri, biases2, carries, accs, valid, row0):
    nh, nt = len(qs), len(kbs)
    t = kbs[0].shape[0]
    tail = lambda x, n: x[row0[n]:]
    put = lambda x, n, new: new if row0[n] == 0 else jnp.concatenate([x[:row0[n]], new], axis=0)
    z2 =[[_dot_nt(tail(qs[h], n), kbs[n]) + biases2[h] for h in range(nh)] for n in range(nt)]
    sp2 = [[_softplus2(z2[n][h]) for h in range(nh)] for n in range(nt)]
    sp2 = [[s if valid[n] is None else jnp.where(tail(valid[n], n), s, 0.0) for s in sp2[n]] for n in range(nt)]
    carry_at = []
    for n in range(nt):
        carry_at.append([tail(c, n) for c in carries])
        tot = [jnp.broadcast_to(jnp.sum(sp2[n][h], axis=1, keepdims=True), carry_at[n][h].shape) for h in range(nh)]
        carries = [put(carries[h], n, carry_at[n][h] - tot[h]) for h in range(nh)]
    rem2 = [[_dot(sp2[n][h].astype(BF16), ntri) + jnp.concatenate([carry_at[n][h]] * (t // LANES), axis=1)
             for h in range(nh)] for n in range(nt)]
    w = [[jnp.exp2((z2[n][h] - sp2[n][h]) + rem2[n][h]) for h in range(nh)] for n in range(nt)]
    w = [[x if valid[n] is None else jnp.where(tail(valid[n], n), x, 0.0) for x in w[n]] for n in range(nt)]
    for n in range(nt):
        pv = [_dot(w[n][h].astype(BF16), vbs[n]) for h in range(nh)]
        accs = [put(accs[h], n, tail(accs[h], n) + pv[h]) for h in range(nh)]
    return carries, accs


def _sb_prompt_kernel(bias_ref, q_ref, k_ref, v_ref, tri_ref, o_ref, c_scr, a_scr):
    p = pl.program_id(1)
    i = pl.program_id(2)
    tq = q_ref.shape[1]
    q = q_ref[0]
    lane = lax.broadcasted_iota(jnp.int32, (tq, LANES), 1)
    first = lane < SB_HEAD_DIM
    qs = (jnp.where(first, q, jnp.zeros_like(q)), jnp.where(first, jnp.zeros_like(q), q))
    biases = (bias_ref[2 * p] * LOG2E, bias_ref[2 * p + 1] * LOG2E)
    tri = tri_ref[...]
    c_scr[...] = jnp.zeros_like(c_scr)
    a_scr[...] = jnp.zeros_like(a_scr)

    tk = tri_ref.shape[0]

    def tiles(js, valid, row0):
        kbs = [k_ref[0, pl.ds(pl.multiple_of(j * tk, tk), tk), :] for j in js]
        vbs = [v_ref[0, pl.ds(pl.multiple_of(j * tk, tk), tk), :] for j in js]
        carry, acc = _sb_tiles(qs, kbs, vbs, tri, biases, [c_scr[0], c_scr[1]], [a_scr[0], a_scr[1]], valid, row0)
        for h in range(2):
            c_scr[h] = carry[h]
            a_scr[h] = acc[h]

    row = lax.broadcasted_iota(jnp.int32, (tq, tk), 0)
    col = lax.broadcasted_iota(jnp.int32, (tq, tk), 1)
    ratio = tq // tk
    for d in range(ratio - 1, 0, -2):
        tiles([ratio * i + d, ratio * i + d - 1], [col + d * tk < row, col + (d - 1) * tk < row], [d * tk, (d - 1) * tk])

    def body(n, _):
        j = ratio * i - 1 - 2 * n
        tiles([j, j - 1], [None, None], [0, 0])
        return 0

    lax.fori_loop(0, (ratio // 2) * i, body, 0)
    o_ref[0] = jnp.where(first, a_scr[0], a_scr[1]).astype(o_ref.dtype)


def _sb_prompt(q, k, v, sb_bias, bsz, t):
    npair, m, _ = q.shape
    tk = SB_TILE
    tq = SB_QUERY_TILES * tk
    nq = t // tq
    return pl.pallas_call(
        _sb_prompt_kernel,
        grid_spec=pltpu.PrefetchScalarGridSpec(
            num_scalar_prefetch=1,
            grid=(bsz, npair, nq),
            in_specs=[
                pl.BlockSpec((1, tq, LANES), lambda b, p, i, s: (p, b * nq + i, 0)),
                pl.BlockSpec((1, t, LANES), lambda b, p, i, s: (p, b, 0)),
                pl.BlockSpec((1, t, LANES), lambda b, p, i, s: (p, b, 0)),
                pl.BlockSpec((tk, tk), lambda b, p, i, s: (0, 0)),
            ],
            out_specs=pl.BlockSpec((1, tq, LANES), lambda b, p, i, s: (p, b * nq + i, 0)),
            scratch_shapes=[pltpu.VMEM((2, tq, LANES), F32), pltpu.VMEM((2, tq, LANES), F32)],
        ),
        out_shape=jax.ShapeDtypeStruct((npair, m, LANES), BF16),
        compiler_params=_params(("parallel", "parallel", "arbitrary")),
        name="sb_prompt",
    )(sb_bias.astype(F32), q, k, v, _sb_neg_tri(tk))


def _sb_decode_kernel(pt_ref, q_ref, bias_ref, *refs, group):
    k_refs, v_refs = refs[0:group], refs[group:2 * group]
    tri_ref, o_ref, c_scr, a_scr = refs[2 * group:]
    s = pl.program_id(0)
    g = pl.program_id(1)
    width = q_ref.shape[2]

    @pl.when(g == 0)
    def _():
        c_scr[...] = jnp.zeros_like(c_scr)
        a_scr[...] = jnp.zeros_like(a_scr)

    head = lax.broadcasted_iota(jnp.int32, (SB_HEADS, width), 0)
    lane = lax.broadcasted_iota(jnp.int32, (SB_HEADS, width), 1)
    own = (lane // SB_HEAD_DIM) == head
    qm = jnp.where(own, jnp.broadcast_to(q_ref[0], (SB_HEADS, width)), 0.0).astype(BF16)
    bias2 = jnp.concatenate([bias_ref[...] * LOG2E] * group, axis=0)
    z2 = jnp.concatenate([_dot(qm, k_refs[j][0].astype(BF16)) for j in range(group)], axis=0) + bias2
    sp2 = _softplus2(z2)
    hi = sp2.astype(BF16)
    lo = (sp2 - hi.astype(F32)).astype(BF16)
    tot = jnp.sum(sp2, axis=1, keepdims=True)
    carries = [c_scr[...]]
    for j in range(group):
        carries.append(carries[j] - tot[j * SB_HEADS:(j + 1) * SB_HEADS, :])
    c_scr[...] = carries[group]
    rem2 = _dot(hi, tri_ref[...]) + _dot(lo, tri_ref[...]) + jnp.concatenate(carries[0:group], axis=0)
    w = jnp.exp2((z2 - sp2) + rem2)
    acc = a_scr[...]
    for j in range(group):
        wj = w[j * SB_HEADS:(j + 1) * SB_HEADS, :].astype(BF16)
        acc = acc + _dot_nt(wj, v_refs[j][0].astype(BF16))
    a_scr[...] = acc

    @pl.when(g == pl.num_programs(1) - 1)
    def _():
        o = jnp.sum(jnp.where(own, acc, 0.0), axis=0, keepdims=True)
        for p in range(width // LANES):
            o_ref[p, pl.ds(s, 1), :] = o[:, p * LANES:(p + 1) * LANES]


def _sb_decode(q, cache_kt, cache_vt, page_table, sb_bias):
    n_seq, _, width = q.shape
    page = cache_kt.shape[2]
    npages = page_table.shape[1]
    group = SB_DECODE_GROUP if npages % SB_DECODE_GROUP == 0 else 1
    tri = _sb_neg_tri(page)

    def kv_spec(jj):
        return pl.BlockSpec((1, width, page), lambda b, g, pt: (pt[b, npages - 1 - (g * group + jj)], 0, 0))

    return pl.pallas_call(
        functools.partial(_sb_decode_kernel, group=group),
        grid_spec=pltpu.PrefetchScalarGridSpec(
            num_scalar_prefetch=1,
            grid=(n_seq, npages // group),
            in_specs=[
                pl.BlockSpec((1, 1, width), lambda b, g, pt: (b, 0, 0)),
                pl.BlockSpec((SB_HEADS, 1), lambda b, g, pt: (0, 0)),
                *[kv_spec(jj) for jj in range(group)],
                *[kv_spec(jj) for jj in range(group)],
                pl.BlockSpec((page, page), lambda b, g, pt: (0, 0)),
            ],
            out_specs=pl.BlockSpec((width // LANES, n_seq, LANES), lambda b, g, pt: (0, 0, 0)),
            scratch_shapes=[pltpu.VMEM((SB_HEADS, 1), F32), pltpu.VMEM((SB_HEADS, width), F32)],
        ),
        out_shape=jax.ShapeDtypeStruct((width // LANES, n_seq, LANES), F32),
        compiler_params=_params(("arbitrary", "arbitrary")),
        name="sb_decode",
    )(page_table, q, sb_bias.astype(F32).reshape(SB_HEADS, 1), *([cache_kt] * group), *([cache_vt] * group), tri)


def _ssd_prompt_kernel(xbc_ref, z_ref, dt_ref, dtt_ref, cw_ref, cb_ref, dtb_ref, dtbt_ref, alog_ref, alogt_ref,
                       dskip_ref, nw_ref, y_ref, conv_ref, ssm_ref, buf, st):
    c = pl.program_id(1)
    nc = pl.num_programs(1)
    L = xbc_ref.shape[0]
    pad = SUBLANES

    @pl.when(c == 0)
    def _():
        buf[0:pad, :] = jnp.zeros((pad, SSD_CONV_CH), F32)
        st[...] = jnp.zeros_like(st)

    buf[pad:pad + L, :] = xbc_ref[...]
    conv = cb_ref[...]
    for j in range(SSD_CONV):
        off = pad - (SSD_CONV - 1) + j
        conv = conv + cw_ref[j:j + 1, :] * buf[off:off + L, :]
    tail = buf[pad + L - (SSD_CONV - 1):pad + L, :]
    buf[pad - (SSD_CONV - 1):pad, :] = tail
    xa = _silu(conv)
    xs = xa[:, 0:SSD_WIDTH]
    gw = SSD_STATE
    bm = [xa[:, SSD_WIDTH + g * gw:SSD_WIDTH + (g + 1) * gw].astype(BF16) for g in range(SSD_GROUPS)]
    cm = [xa[:, SSD_WIDTH + (SSD_GROUPS + g) * gw:SSD_WIDTH + (SSD_GROUPS + g + 1) * gw].astype(BF16)
          for g in range(SSD_GROUPS)]

    dt = _softplus(dt_ref[...] + dtb_ref[...])
    dtt = _softplus(dtt_ref[...] + dtbt_ref[...])
    a2 = -jnp.exp(alog_ref[...]) * LOG2E
    at2 = -jnp.exp(alogt_ref[...]) * LOG2E
    row = lax.broadcasted_iota(jnp.int32, (L, L), 0)
    col = lax.broadcasted_iota(jnp.int32, (L, L), 1)
    lower = col <= row
    lower_b = lower.astype(BF16)
    upper_b = (row <= col).astype(BF16)
    cum = sum(_dot(lower_b, part) for part in _split3(dt * a2))
    cumt = sum(_dot(part, upper_b) for part in _split3(dtt * at2))
    last = cum[L - 1:L, :]
    wst = jnp.exp2(last - cum) * dt
    ecum = jnp.exp2(cum)
    elast = jnp.exp2(last)

    lane = lax.broadcasted_iota(jnp.int32, (L, LANES), 1)
    first = lane < SSD_HEAD_DIM
    first_row = first[0:1, :]
    heads_per_group = SSD_HEADS // SSD_GROUPS
    cb = [jnp.where(lower, _dot_nt(cm[g], bm[g]), 0.0) for g in range(SSD_GROUPS)]
    ys = []
    for p in range(SSD_WIDTH // LANES):
        g = (2 * p) // heads_per_group
        xp = xs[:, p * LANES:(p + 1) * LANES]
        xpb = xp.astype(BF16)
        yi = []
        for h in (2 * p, 2 * p + 1):
            seg = cum[:, h:h + 1] - cumt[h:h + 1, :]
            dec = jnp.exp2(jnp.minimum(seg, 0.0)) * dtt[h:h + 1, :]
            yi.append(_dot((cb[g] * dec).astype(BF16), xpb))
        y_intra = jnp.where(first, yi[0], yi[1])
        stp = st[:, p * LANES:(p + 1) * LANES]
        ec = jnp.where(first, ecum[:, 2 * p:2 * p + 1], ecum[:, 2 * p + 1:2 * p + 2])
        y_inter = _dot(cm[g], stp.astype(BF16)) * ec
        wp = jnp.where(first, wst[:, 2 * p:2 * p + 1], wst[:, 2 * p + 1:2 * p + 2])
        el = jnp.where(first_row, elast[:, 2 * p:2 * p + 1], elast[:, 2 * p + 1:2 * p + 2])
        st[:, p * LANES:(p + 1) * LANES] = el * stp + _dot_tn(bm[g], (xp * wp).astype(BF16))
        ys.append(y_intra + y_inter + dskip_ref[:, p * LANES:(p + 1) * LANES] * xp)
    y = jnp.concatenate(ys, axis=1)
    y_ref[...] = _rmsnorm(y * _silu(z_ref[...].astype(F32)), nw_ref[...]).astype(y_ref.dtype)

    @pl.when(c == nc - 1)
    def _():
        conv_ref[0] = tail
        ssm_ref[0] = st[...].T


def _ssd_prompt(xbc, z, dt, dtt, conv_w, conv_b, dt_bias, a_log, d_skip, ssd_norm, bsz, t):
    m = xbc.shape[0]
    L = SSD_CHUNK
    nc = t // L
    rows = lambda b, c: (b * nc + c, 0)
    full = lambda b, c: (0, 0)
    vec = lambda v: v.astype(F32).reshape(1, -1)
    colv = lambda v: v.astype(F32).reshape(-1, 1)
    y, conv, ssm = pl.pallas_call(
        _ssd_prompt_kernel,
        grid=(bsz, nc),
        in_specs=[
            pl.BlockSpec((L, SSD_CONV_CH), rows),
            pl.BlockSpec((L, SSD_WIDTH), rows),
            pl.BlockSpec((L, SSD_HEADS), rows),
            pl.BlockSpec((SSD_HEADS, L), lambda b, c: (0, b * nc + c)),
            pl.BlockSpec((SSD_CONV, SSD_CONV_CH), full),
            pl.BlockSpec((1, SSD_CONV_CH), full),
            pl.BlockSpec((1, SSD_HEADS), full),
            pl.BlockSpec((SSD_HEADS, 1), full),
            pl.BlockSpec((1, SSD_HEADS), full),
            pl.BlockSpec((SSD_HEADS, 1), full),
            pl.BlockSpec((1, SSD_WIDTH), full),
            pl.BlockSpec((1, SSD_WIDTH), full),
        ],
        out_specs=[
            pl.BlockSpec((L, SSD_WIDTH), rows),
            pl.BlockSpec((1, SSD_CONV - 1, SSD_CONV_CH), lambda b, c: (b, 0, 0)),
            pl.BlockSpec((1, SSD_WIDTH, SSD_STATE), lambda b, c: (b, 0, 0)),
        ],
        out_shape=[
            jax.ShapeDtypeStruct((m, SSD_WIDTH), BF16),
            jax.ShapeDtypeStruct((bsz, SSD_CONV - 1, SSD_CONV_CH), F32),
            jax.ShapeDtypeStruct((bsz, SSD_WIDTH, SSD_STATE), F32),
        ],
        scratch_shapes=[pltpu.VMEM((L + SUBLANES, SSD_CONV_CH), F32), pltpu.VMEM((SSD_STATE, SSD_WIDTH), F32)],
        compiler_params=_params(("parallel", "arbitrary")),
        name="ssd_prompt",
    )(xbc, z, dt, dtt, conv_w.astype(F32), vec(conv_b), vec(dt_bias), colv(dt_bias), vec(a_log), colv(a_log),
      vec(jnp.repeat(d_skip, SSD_HEAD_DIM)), vec(ssd_norm))
    return y, conv, ssm.reshape(bsz, SSD_HEADS, SSD_HEAD_DIM, SSD_STATE)


def _ssd_step_kernel(xbc_ref, z_ref, dt_ref, sconv_ref, sssm_ref, cw_ref, cb_ref, dtb_ref, alog_ref, dskip_ref,
                     nw_ref, y_ref, conv_ref, ssm_ref):
    lane_head = lax.broadcasted_iota(jnp.int32, (1, SSD_WIDTH), 1) // SSD_HEAD_DIM
    rows_per_group = SSD_WIDTH // SSD_GROUPS
    row = lax.broadcasted_iota(jnp.int32, (SSD_WIDTH, SSD_STATE), 0)
    a = -jnp.exp(alog_ref[...])
    for s in range(xbc_ref.shape[0]):
        xr = xbc_ref[s]
        cs = sconv_ref[s]
        conv = cb_ref[...] + cw_ref[SSD_CONV - 1:SSD_CONV, :] * xr
        for j in range(SSD_CONV - 1):
            conv = conv + cw_ref[j:j + 1, :] * cs[j:j + 1, :]
        conv_ref[s] = jnp.concatenate([cs[1:SSD_CONV - 1, :], xr], axis=0)
        xa = _silu(conv)
        xs = xa[:, 0:SSD_WIDTH]
        dt = _softplus(dt_ref[s] + dtb_ref[...])
        da = jnp.exp(dt * a)
        dt_w = jnp.zeros((1, SSD_WIDTH), F32)
        da_w = jnp.zeros((1, SSD_WIDTH), F32)
        for h in range(SSD_HEADS):
            dt_w = jnp.where(lane_head == h, dt[:, h:h + 1], dt_w)
            da_w = jnp.where(lane_head == h, da[:, h:h + 1], da_w)
        dtx_col = _row_to_col(dt_w * xs)
        da_col = _row_to_col(da_w)
        b_rows = jnp.zeros((SSD_WIDTH, SSD_STATE), F32)
        c_rows = jnp.zeros((SSD_WIDTH, SSD_STATE), F32)
        for g in range(SSD_GROUPS):
            sel = (row // rows_per_group) == g
            b0 = SSD_WIDTH + g * SSD_STATE
            c0 = SSD_WIDTH + (SSD_GROUPS + g) * SSD_STATE
            b_rows = jnp.where(sel, xa[:, b0:b0 + SSD_STATE], b_rows)
            c_rows = jnp.where(sel, xa[:, c0:c0 + SSD_STATE], c_rows)
        new = da_col * sssm_ref[s] + dtx_col * b_rows
        ssm_ref[s] = new
        y = _col_to_row(jnp.sum(new * c_rows, axis=1, keepdims=True)) + dskip_ref[...] * xs
        y_ref[s] = _rmsnorm(y * _silu(z_ref[s].astype(F32)), nw_ref[...])


def _ssd_step(xbc, z, dt, state_conv, state_ssm, conv_w, conv_b, dt_bias, a_log, d_skip, ssd_norm):
    n = xbc.shape[0]
    ns = STEP_SEQS if n % STEP_SEQS == 0 else 1
    vec = lambda v: v.astype(F32).reshape(1, -1)
    per = lambda i: (i, 0, 0)
    full = lambda i: (0, 0)
    y, conv, ssm = pl.pallas_call(
        _ssd_step_kernel,
        grid=(n // ns,),
        in_specs=[
            pl.BlockSpec((ns, 1, SSD_CONV_CH), per),
            pl.BlockSpec((ns, 1, SSD_WIDTH), per),
            pl.BlockSpec((ns, 1, SSD_HEADS), per),
            pl.BlockSpec((ns, SSD_CONV - 1, SSD_CONV_CH), per),
            pl.BlockSpec((ns, SSD_WIDTH, SSD_STATE), per),
            pl.BlockSpec((SSD_CONV, SSD_CONV_CH), full),
            pl.BlockSpec((1, SSD_CONV_CH), full),
            pl.BlockSpec((1, SSD_HEADS), full),
            pl.BlockSpec((1, SSD_HEADS), full),
            pl.BlockSpec((1, SSD_WIDTH), full),
            pl.BlockSpec((1, SSD_WIDTH), full),
        ],
        out_specs=[
            pl.BlockSpec((ns, 1, SSD_WIDTH), per),
            pl.BlockSpec((ns, SSD_CONV - 1, SSD_CONV_CH), per),
            pl.BlockSpec((ns, SSD_WIDTH, SSD_STATE), per),
        ],
        out_shape=[
            jax.ShapeDtypeStruct((n, 1, SSD_WIDTH), F32),
            jax.ShapeDtypeStruct((n, SSD_CONV - 1, SSD_CONV_CH), F32),
            jax.ShapeDtypeStruct((n, SSD_WIDTH, SSD_STATE), F32),
        ],
        compiler_params=_params(("parallel",)),
        name="ssd_step",
    )(xbc.reshape(n, 1, SSD_CONV_CH), z.reshape(n, 1, SSD_WIDTH), dt.reshape(n, 1, SSD_HEADS),
      state_conv.astype(F32), state_ssm.astype(F32).reshape(n, SSD_WIDTH, SSD_STATE), conv_w.astype(F32),
      vec(conv_b), vec(dt_bias), vec(a_log), vec(jnp.repeat(d_skip, SSD_HEAD_DIM)), vec(ssd_norm))
    return y.reshape(n, SSD_WIDTH), conv, ssm.reshape(n, SSD_HEADS, SSD_HEAD_DIM, SSD_STATE)


def _even_mix(osb_ref, y_ref):
    return jnp.concatenate([osb_ref[p] for p in range(osb_ref.shape[0])] + [y_ref[...]], axis=1)


def _odd_mix(o_ref, g_ref, nw_ref):
    o = jnp.concatenate([o_ref[h] for h in range(o_ref.shape[0])], axis=1).astype(F32)
    return (_rmsnorm(o, nw_ref[...]) * _silu(g_ref[...].astype(F32))).astype(BF16)


def _mix_mlp_kernel(*refs, mix_fn, n_mix, final_norm):
    mix_refs = refs[0:n_mix]
    x_ref, wout_ref, nw_ref, w1_ref, w2_ref, fw_ref, o_ref, hn_scr, acc_scr = refs[n_mix:]
    j = pl.program_id(1)

    @pl.when(j == 0)
    def _():
        h1 = x_ref[...] + _dot(mix_fn(*mix_refs), wout_ref[...])
        o_ref[...] = h1
        hn_scr[...] = _rmsnorm(h1, nw_ref[...]).astype(BF16)

    h = jnp.maximum(_dot(hn_scr[...], w1_ref[...]), 0.0)
    a = (h * h).astype(BF16)

    @pl.when(j == 0)
    def _():
        acc_scr[...] = _dot(a, w2_ref[...])

    @pl.when(j > 0)
    def _():
        acc_scr[...] += _dot(a, w2_ref[...])

    @pl.when(j == pl.num_programs(1) - 1)
    def _():
        out = o_ref[...] + acc_scr[...]
        o_ref[...] = _rmsnorm(out, fw_ref[...]) if final_norm else out


def _mix_mlp(mix_fn, mix_args, mix_specs, x, w_out, norm_w, w1_all, w2_all, layer, final_w, tm, tf):
    m, d = x.shape
    f = w1_all.shape[2]
    final_norm = final_w is not None
    fw = (final_w if final_norm else jnp.ones((d,), F32)).astype(F32).reshape(1, d)
    const = lambda i, j: (0, 0)
    return pl.pallas_call(
        functools.partial(_mix_mlp_kernel, mix_fn=mix_fn, n_mix=len(mix_args), final_norm=final_norm),
        grid=(m // tm, f // tf),
        in_specs=[
            *mix_specs,
            pl.BlockSpec((tm, d), lambda i, j: (i, 0)),
            pl.BlockSpec(w_out.shape, const),
            pl.BlockSpec((1, d), const),
            pl.BlockSpec((None, d, tf), lambda i, j: (layer, 0, j)),
            pl.BlockSpec((None, tf, d), lambda i, j: (layer, j, 0)),
            pl.BlockSpec((1, d), const),
        ],
        out_specs=pl.BlockSpec((tm, d), lambda i, j: (i, 0)),
        out_shape=jax.ShapeDtypeStruct((m, d), F32),
        scratch_shapes=[pltpu.VMEM((tm, d), BF16), pltpu.VMEM((tm, d), F32)],
        compiler_params=_params(("parallel", "arbitrary")),
        name="mix_mlp",
    )(*mix_args, x, w_out.astype(BF16), norm_w.astype(F32).reshape(1, d), w1_all, w2_all, fw)


def _even_mix_mlp(osb, y, x, w_out, norm_w, w1_all, w2_all, layer, final_w, tm, tf):
    specs = [pl.BlockSpec((osb.shape[0], tm, LANES), lambda i, j: (0, i, 0)),
             pl.BlockSpec((tm, SSD_WIDTH), lambda i, j: (i, 0))]
    return _mix_mlp(_even_mix, (osb, y), specs, x, w_out, norm_w, w1_all, w2_all, layer, final_w, tm, tf)


def _odd_mix_mlp(o, g, hg_norm, x, w_out, norm_w, w1_all, w2_all, layer, final_w, tm, tf):
    nh = o.shape[0]
    specs = [pl.BlockSpec((nh, tm, HG_VAL), lambda i, j: (0, i, 0)),
             pl.BlockSpec((tm, nh * HG_VAL), lambda i, j: (i, 0)),
             pl.BlockSpec((1, nh * HG_VAL), lambda i, j: (0, 0))]
    args = (o, g, hg_norm.astype(F32).reshape(1, -1))
    return _mix_mlp(_odd_mix, args, specs, x, w_out, norm_w, w1_all, w2_all, layer, final_w, tm, tf)


def _odd_proj_kernel(x_ref, nw_ref, w_ref, q_ref, f_ref, v_ref, g_ref):
    hn = _rmsnorm(x_ref[...], nw_ref[...]).astype(BF16)
    width = HG_HEADS * HG_KEY
    per_chunk = PROJ_COLS // HG_KEY
    for i, ref in enumerate((q_ref, f_ref, v_ref)):
        for c in range(width // PROJ_COLS):
            y = _dot(hn, w_ref[:, i * width + c * PROJ_COLS:i * width + (c + 1) * PROJ_COLS])
            for h in range(per_chunk):
                ref[c * per_chunk + h] = y[:, h * HG_KEY:(h + 1) * HG_KEY]
    for c in range(width // PROJ_COLS):
        g_ref[:, c * PROJ_COLS:(c + 1) * PROJ_COLS] = _dot(
            hn, w_ref[:, 3 * width + c * PROJ_COLS:3 * width + (c + 1) * PROJ_COLS]).astype(g_ref.dtype)


def _odd_proj(x, norm_w, w_in, tm):
    m, d = x.shape
    width = HG_HEADS * HG_KEY
    heads = lambda i: (0, i, 0)
    per_head = jax.ShapeDtypeStruct((HG_HEADS, m, HG_KEY), F32)
    return pl.pallas_call(
        _odd_proj_kernel,
        grid=(m // tm,),
        in_specs=[
            pl.BlockSpec((tm, d), lambda i: (i, 0)),
            pl.BlockSpec((1, d), lambda i: (0, 0)),
            pl.BlockSpec(w_in.shape, lambda i: (0, 0)),
        ],
        out_specs=[
            pl.BlockSpec((HG_HEADS, tm, HG_KEY), heads),
            pl.BlockSpec((HG_HEADS, tm, HG_KEY), heads),
            pl.BlockSpec((HG_HEADS, tm, HG_KEY), heads),
            pl.BlockSpec((tm, width), lambda i: (i, 0)),
        ],
        out_shape=[per_head, per_head, per_head, jax.ShapeDtypeStruct((m, width), BF16)],
        compiler_params=_params(("parallel",)),
        name="odd_proj",
    )(x, norm_w.astype(F32).reshape(1, d), w_in.astype(BF16))


def _hgrn_lower_bound(raw, layer):
    e = jnp.exp(raw - jnp.max(raw, axis=0, keepdims=True))
    p = e / jnp.sum(e, axis=0, keepdims=True)
    lb = jnp.zeros_like(p[0])
    for l in range(1, layer + 1):
        lb = lb + p[l]
    return lb


def _hgrn_gates(fpre, lb):
    sig, nsig = _sigmoid_pair(fpre)
    return lb + (1.0 - lb) * sig, (1.0 - lb) * nsig


def _hgrn_prompt_kernel(q_ref, f_ref, v_ref, lb_ref, o_ref, s_ref, st, b_scr, k_scr, od_scr, *, layer):
    c = pl.program_id(1)
    nc = pl.num_programs(1)
    L = q_ref.shape[1]
    nblk = L // HG_DIAG

    @pl.when(c == 0)
    def _():
        st[...] = jnp.zeros_like(st)

    row = lax.broadcasted_iota(jnp.int32, (L, L), 0)
    col = lax.broadcasted_iota(jnp.int32, (L, L), 1)
    lower_b = (col <= row).astype(BF16)
    pos = lax.broadcasted_iota(jnp.int32, (L, HG_KEY), 0)

    def heads(n, _):
        hs = [n * HG_HEAD_GROUP + e for e in range(HG_HEAD_GROUP)]
        every = range(HG_HEAD_GROUP)
        q = [q_ref[h] for h in hs]
        vb = [v_ref[h].astype(BF16) for h in hs]
        gates = [_hgrn_gates(f_ref[h], _hgrn_lower_bound(lb_ref[:, h], layer)) for h in hs]
        kin = [gk[1] for gk in gates]
        b = [sum(_dot(lower_b, part) for part in _split3(jnp.log(gk[0]) * LOG2E)) for gk in gates]
        for e in every:
            b_scr[e] = b[e]
            k_scr[e] = kin[e]
        stt = [st[h] for h in hs]
        o = [_dot_nt((q[e] * jnp.exp2(b[e])).astype(BF16), stt[e].astype(BF16)) for e in every]
        b_last = [x[L - 1:L, :] for x in b]
        for e in every:
            st[hs[e]] = stt[e] * jnp.exp2(b_last[e]) + _dot_tn(
                vb[e], (kin[e] * jnp.exp2(b_last[e] - b[e])).astype(BF16))

        att = [jnp.zeros((L, L), F32) for _ in every]
        g = L
        while g > HG_DIAG:
            half = g // 2
            late = (pos & (g - 1)) >= half
            pair = ((row ^ col) < g) & ((row & (g - 1)) >= half) & ((col & (g - 1)) < half)
            mid = [jnp.broadcast_to(x.reshape(L // g, g, HG_KEY)[:, half - 1:half, :],
                                    (L // g, g, HG_KEY)).reshape(L, HG_KEY) for x in b]
            r = [(jnp.where(late, q[e], kin[e]) * jnp.exp2(-jnp.abs(b[e] - mid[e]))).astype(BF16) for e in every]
            att = [jnp.where(pair, _dot_nt(r[e], r[e]), att[e]) for e in every]
            g = half
        o = [o[e] + _dot(att[e].astype(BF16), vb[e]) for e in every]

        at = lambda ref, i: ref[pl.ds(i, nblk, stride=HG_DIAG), :]
        for e in every:
            bs = [at(b_scr.at[e], i) for i in range(HG_DIAG)]
            ks = [at(k_scr.at[e], i) for i in range(HG_DIAG)]
            vs = [at(v_ref.at[hs[e]], i) for i in range(HG_DIAG)]
            for i in range(HG_DIAG):
                qi = at(q_ref.at[hs[e]], i)
                acc = jnp.zeros((nblk, HG_VAL), F32)
                for j in range(i + 1):
                    a = jnp.sum(qi * ks[j] * jnp.exp2(bs[i] - bs[j]), axis=1, keepdims=True)
                    acc = acc + a * vs[j]
                od_scr[e, pl.ds(i, nblk, stride=HG_DIAG), :] = acc
        for e in every:
            o_ref[hs[e]] = (o[e] + od_scr[e]).astype(o_ref.dtype)
        return 0

    lax.fori_loop(0, q_ref.shape[0] // HG_HEAD_GROUP, heads, 0)

    @pl.when(c == nc - 1)
    def _():
        for h in range(s_ref.shape[1]):
            s_ref[0, h] = st[h].T


def _hgrn_prompt(q, f, v, lb_raw, layer, bsz, t):
    nh, m, _ = q.shape
    L = HG_CHUNK
    nc = t // L
    blk = lambda b, c: (0, b * nc + c, 0)
    depth = lb_raw.shape[0]
    return pl.pallas_call(
        functools.partial(_hgrn_prompt_kernel, layer=layer),
        grid=(bsz, nc),
        in_specs=[
            pl.BlockSpec((nh, L, HG_KEY), blk),
            pl.BlockSpec((nh, L, HG_KEY), blk),
            pl.BlockSpec((nh, L, HG_VAL), blk),
            pl.BlockSpec((depth, nh, 1, HG_KEY), lambda b, c: (0, 0, 0, 0)),
        ],
        out_specs=[
            pl.BlockSpec((nh, L, HG_VAL), blk),
            pl.BlockSpec((1, nh, HG_KEY, HG_VAL), lambda b, c: (b, 0, 0, 0)),
        ],
        out_shape=[
            jax.ShapeDtypeStruct((nh, m, HG_VAL), BF16),
            jax.ShapeDtypeStruct((bsz, nh, HG_KEY, HG_VAL), F32),
        ],
        scratch_shapes=[
            pltpu.VMEM((nh, HG_VAL, HG_KEY), F32),
            pltpu.VMEM((HG_HEAD_GROUP, L, HG_KEY), F32),
            pltpu.VMEM((HG_HEAD_GROUP, L, HG_KEY), F32),
            pltpu.VMEM((HG_HEAD_GROUP, L, HG_VAL), F32),
        ],
        compiler_params=_params(("parallel", "arbitrary")),
        name="hgrn_prompt",
    )(q, f, v, lb_raw.astype(F32).reshape(depth, nh, 1, HG_KEY))


def _hgrn_step_kernel(q_ref, f_ref, v_ref, lb_ref, s_ref, o_ref, snew_ref, *, layer):
    ns = s_ref.shape[0]
    for h in range(q_ref.shape[0]):
        lb = _hgrn_lower_bound(lb_ref[:, h], layer)
        for s in range(ns):
            i = pl.program_id(0) * ns + s
            q = q_ref[h, pl.ds(i, 1), :]
            v = v_ref[h, pl.ds(i, 1), :]
            fg, kin = _hgrn_gates(f_ref[h, pl.ds(i, 1), :], lb)
            new = _row_to_col(fg) * s_ref[s, h] + _row_to_col(kin) * v
            snew_ref[s, h] = new
            o_ref[h, pl.ds(i, 1), :] = jnp.sum(_row_to_col(q) * new, axis=0, keepdims=True)


def _hgrn_step(q, f, v, lb_raw, state, layer):
    nh, n, _ = q.shape
    ns = STEP_SEQS if n % STEP_SEQS == 0 else 1
    depth = lb_raw.shape[0]
    whole = lambda i: (0, 0, 0)
    return pl.pallas_call(
        functools.partial(_hgrn_step_kernel, layer=layer),
        grid=(n // ns,),
        in_specs=[
            pl.BlockSpec((nh, n, HG_KEY), whole),
            pl.BlockSpec((nh, n, HG_KEY), whole),
            pl.BlockSpec((nh, n, HG_VAL), whole),
            pl.BlockSpec((depth, nh, 1, HG_KEY), lambda i: (0, 0, 0, 0)),
            pl.BlockSpec((ns, nh, HG_KEY, HG_VAL), lambda i: (i, 0, 0, 0)),
        ],
        out_specs=[
            pl.BlockSpec((nh, n, HG_VAL), whole),
            pl.BlockSpec((ns, nh, HG_KEY, HG_VAL), lambda i: (i, 0, 0, 0)),
        ],
        out_shape=[
            jax.ShapeDtypeStruct((nh, n, HG_VAL), F32),
            jax.ShapeDtypeStruct((n, nh, HG_KEY, HG_VAL), F32),
        ],
        compiler_params=_params(("arbitrary",)),
        name="hgrn_step",
    )(q, f, v, lb_raw.astype(F32).reshape(depth, nh, 1, HG_KEY), state.astype(F32))


def _row_tile(m, want):
    return want if m % want == 0 else m


def kernel(x_prompt, x_sample, cache_k, cache_v, page_table, state_conv, state_ssm, state_hgrn, norm_mix, norm_ffn,
           norm_final, w_in_even, sb_bias, conv_w, conv_b, dt_bias, a_log, d_skip, ssd_norm, w_out_even, w_in_odd,
           hg_lb_raw, hg_norm, w_out_odd, w_ff1, w_ff2):
    bsz, t, d = x_prompt.shape
    n_seq = x_sample.shape[0]
    depth = norm_mix.shape[0]
    mp = bsz * t
    hp = x_prompt.reshape(mp, d)
    hs = x_sample.reshape(n_seq, d)
    tmp = _row_tile(mp, 512)
    tmm = _row_tile(mp, 1024)
    tf = 1024
    outs = {k: [] for k in ("kp", "vp", "ks", "vs", "cp", "cs", "sp", "ss", "gp", "gs")}
    w1_all = w_ff1.astype(BF16)
    w2_all = w_ff2.astype(BF16)

    for layer in range(depth):
        li = layer // 2
        fw = norm_final if layer == depth - 1 else None
        mlp_w = (norm_ffn[layer], w1_all, w2_all, layer, fw)
        if layer % 2 == 0:
            ssd_w = (conv_w[li], conv_b[li], dt_bias[li], a_log[li], d_skip[li], ssd_norm[li])
            q, k, v, kf, vf, z, xbc, dt, dtt = _even_proj(hp, norm_mix[layer], w_in_even[li], tmp, seq_len=t)
            osb = _sb_prompt(q, k, v, sb_bias[li], bsz, t)
            y, cp, sp = _ssd_prompt(xbc, z, dt, dtt, *ssd_w, bsz, t)
            hp = _even_mix_mlp(osb, y, hp, w_out_even[li], *mlp_w, tmm, tf)
            rows_view = lambda a: jnp.transpose(a.reshape(bsz, SB_HEADS, SB_HEAD_DIM, t), (0, 3, 1, 2))
            outs["kp"].append(rows_view(kf))
            outs["vp"].append(rows_view(vf))
            outs["cp"].append(cp)
            outs["sp"].append(sp)
            q, k, v, kf, vf, z, xbc, dt, dtt = _even_proj(hs, norm_mix[layer], w_in_even[li], n_seq)
            qs = q.astype(F32).transpose(1, 0, 2).reshape(n_seq, 1, SB_WIDTH)
            n_phys, page = cache_k.shape[1], cache_k.shape[2]
            to_lanes = lambda c: jnp.transpose(c, (0, 2, 3, 1)).reshape(n_phys, SB_WIDTH, page)
            osb = _sb_decode(qs, to_lanes(cache_k[li]), to_lanes(cache_v[li]), page_table, sb_bias[li])
            y, cs, ss = _ssd_step(xbc, z, dt, state_conv[li], state_ssm[li], *ssd_w)
            hs = _even_mix_mlp(osb.astype(BF16), y.astype(BF16), hs, w_out_even[li], *mlp_w, n_seq, tf)
            outs["ks"].append(kf.reshape(n_seq, 1, SB_HEADS, SB_HEAD_DIM))
            outs["vs"].append(vf.reshape(n_seq, 1, SB_HEADS, SB_HEAD_DIM))
            outs["cs"].append(cs)
            outs["ss"].append(ss)
        else:
            q, f, v, g = _odd_proj(hp, norm_mix[layer], w_in_odd[li], tmp)
            o, gp = _hgrn_prompt(q, f, v, hg_lb_raw, layer, bsz, t)
            hp = _odd_mix_mlp(o, g, hg_norm[li], hp, w_out_odd[li], *mlp_w, tmm, tf)
            outs["gp"].append(gp)
            q, f, v, g = _odd_proj(hs, norm_mix[layer], w_in_odd[li], n_seq)
            o, gs = _hgrn_step(q, f, v, hg_lb_raw, state_hgrn[li], layer)
            hs = _odd_mix_mlp(o.astype(BF16), g, hg_norm[li], hs, w_out_odd[li], *mlp_w, n_seq, tf)
            outs["gs"].append(gs)

    y_prompt = hp.reshape(bsz, t, d)
    y_sample = hs.reshape(n_seq, 1, d)
    st = lambda key: jnp.stack(outs[key])
    return (y_prompt, y_sample, st("kp"), st("vp"), st("ks"), st("vs"), st("cp"), st("cs"), st("sp"), st("ss"),
            st("gp"), st("gs"))
```

```python
import functools

import jax
import jax.numpy as jnp
from jax import lax
from jax.experimental import pallas as pl
from jax.experimental.pallas import tpu as pltpu

F32 = jnp.float32
BF16 = jnp.bfloat16

EPS = 1e-6
LOG2E = 1.4426950408889634
LN2 = 0.6931471805599453
SB_HEADS = 8
SB_HEAD_DIM = 64
SB_WIDTH = SB_HEADS * SB_HEAD_DIM
SSD_HEADS = 8
SSD_HEAD_DIM = 64
SSD_WIDTH = SSD_HEADS * SSD_HEAD_DIM
SSD_STATE = 128
SSD_GROUPS = 2
SSD_CONV = 4
SSD_CONV_CH = SSD_WIDTH + 2 * SSD_GROUPS * SSD_STATE
HG_HEADS = 8
HG_KEY = 128
HG_VAL = 128

LANES = 128
SUBLANES = 8
VMEM_LIMIT_BYTES = 52 * 1024 * 1024

SB_TILE = 256
SB_QUERY_TILES = 8
SB_DECODE_GROUP = 32
SSD_CHUNK = 128
SSD_STEP_CHUNKS = 2
HG_STEP_CHUNKS = 4
HG_CHUNK = 128
HG_DIAG = 4
HG_HEAD_GROUP = 8
STEP_SEQS = 4
PROJ_COLS = 512


def _params(sem):
    return pltpu.CompilerParams(dimension_semantics=sem, vmem_limit_bytes=VMEM_LIMIT_BYTES)


def _dot(a, b):
    return jnp.dot(a, b, preferred_element_type=F32)


def _dot_nt(a, b):
    return lax.dot_general(a, b, (((1,), (1,)), ((), ())), preferred_element_type=F32)


def _dot_tn(a, b):
    return lax.dot_general(a, b, (((0,), (0,)), ((), ())), preferred_element_type=F32)


def _split3(x):
    hi = x.astype(BF16)
    r = x - hi.astype(F32)
    mid = r.astype(BF16)
    lo = (r - mid.astype(F32)).astype(BF16)
    return hi, mid, lo


def _rmsnorm(x, w):
    return x * lax.rsqrt(jnp.mean(x * x, axis=-1, keepdims=True) + EPS) * w


def _softplus(x):
    return jnp.maximum(x, 0.0) + jnp.log1p(jnp.exp(-jnp.abs(x)))


def _sigmoid_pair(x):
    e = jnp.exp(-jnp.abs(x))
    r = 1.0 / (1.0 + e)
    big, small = r, e * r
    pos = x >= 0
    return jnp.where(pos, big, small), jnp.where(pos, small, big)


def _silu(x):
    return x / (1.0 + jnp.exp(-x))


def _row_to_col(row):
    n = row.shape[1]
    eye = lax.broadcasted_iota(jnp.int32, (n, n), 0) == lax.broadcasted_iota(jnp.int32, (n, n), 1)
    return jnp.sum(jnp.where(eye, jnp.broadcast_to(row, (n, n)), 0.0), axis=1, keepdims=True)


def _col_to_row(col):
    n = col.shape[0]
    eye = lax.broadcasted_iota(jnp.int32, (n, n), 0) == lax.broadcasted_iota(jnp.int32, (n, n), 1)
    return jnp.sum(jnp.where(eye, jnp.broadcast_to(col, (n, n)), 0.0), axis=0, keepdims=True)


def _even_proj_kernel(x_ref, nw_ref, w_ref, wdt_ref, wdtt_ref,
                      q_ref, k_ref, v_ref, kf_ref, vf_ref, z_ref, xbc_ref, dt_ref, dtt_ref, *, token_minor):
    hn = _rmsnorm(x_ref[...], nw_ref[...]).astype(BF16)
    scale = LOG2E * SB_HEAD_DIM ** -0.5
    npair = SB_WIDTH // LANES
    q = _dot(hn, w_ref[:, 0:SB_WIDTH]) * scale
    for p in range(npair):
        q_ref[p] = q[:, p * LANES:(p + 1) * LANES].astype(BF16)
    for i, (pair_ref, full_ref) in enumerate(((k_ref, kf_ref), (v_ref, vf_ref))):
        y = _dot(hn, w_ref[:, (i + 1) * SB_WIDTH:(i + 2) * SB_WIDTH])
        if token_minor:
            full_ref[0] = y.T
        else:
            full_ref[...] = y
        for p in range(npair):
            pair_ref[p] = y[:, p * LANES:(p + 1) * LANES].astype(BF16)
    z0 = 3 * SB_WIDTH
    z_ref[...] = _dot(hn, w_ref[:, z0:z0 + SSD_WIDTH]).astype(z_ref.dtype)
    x0 = z0 + SSD_WIDTH
    for c in range(SSD_CONV_CH // PROJ_COLS):
        xbc_ref[:, c * PROJ_COLS:(c + 1) * PROJ_COLS] = _dot(
            hn, w_ref[:, x0 + c * PROJ_COLS:x0 + (c + 1) * PROJ_COLS])
    dt_ref[...] = _dot(hn, wdt_ref[...])[:, 0:SSD_HEADS]
    dtt_ref[...] = _dot_nt(wdtt_ref[...], hn)


def _even_proj(x, norm_w, w_in, tm, seq_len=None):
    m, d = x.shape
    npair = SB_WIDTH // LANES
    wb = w_in.astype(BF16)
    n_main = 3 * SB_WIDTH + SSD_WIDTH + SSD_CONV_CH
    w_main = wb
    w_dt = jnp.pad(wb[:, n_main:], ((0, 0), (0, LANES - SSD_HEADS)))
    w_dtt = wb[:, n_main:].T
    full = lambda i: (0, 0)
    rows = lambda i: (i, 0)
    hp = lambda i: (0, i, 0)
    token_minor = seq_len is not None
    if token_minor:
        nt = seq_len // tm
        kv_spec = pl.BlockSpec((1, SB_WIDTH, tm), lambda i: (i // nt, 0, i % nt))
        kv_shape = jax.ShapeDtypeStruct((m // seq_len, SB_WIDTH, seq_len), F32)
    else:
        kv_spec = pl.BlockSpec((tm, SB_WIDTH), rows)
        kv_shape = jax.ShapeDtypeStruct((m, SB_WIDTH), F32)
    return pl.pallas_call(
        functools.partial(_even_proj_kernel, token_minor=token_minor),
        grid=(m // tm,),
        in_specs=[
            pl.BlockSpec((tm, d), rows),
            pl.BlockSpec((1, d), full),
            pl.BlockSpec(wb.shape, full),
            pl.BlockSpec((d, LANES), full),
            pl.BlockSpec((SSD_HEADS, d), full),
        ],
        out_specs=[
            pl.BlockSpec((npair, tm, LANES), hp),
            pl.BlockSpec((npair, tm, LANES), hp),
            pl.BlockSpec((npair, tm, LANES), hp),
            kv_spec,
            kv_spec,
            pl.BlockSpec((tm, SSD_WIDTH), rows),
            pl.BlockSpec((tm, SSD_CONV_CH), rows),
            pl.BlockSpec((tm, SSD_HEADS), rows),
            pl.BlockSpec((SSD_HEADS, tm), lambda i: (0, i)),
        ],
        out_shape=[
            jax.ShapeDtypeStruct((npair, m, LANES), BF16),
            jax.ShapeDtypeStruct((npair, m, LANES), BF16),
            jax.ShapeDtypeStruct((npair, m, LANES), BF16),
            kv_shape,
            kv_shape,
            jax.ShapeDtypeStruct((m, SSD_WIDTH), BF16),
            jax.ShapeDtypeStruct((m, SSD_CONV_CH), F32),
            jax.ShapeDtypeStruct((m, SSD_HEADS), F32),
            jax.ShapeDtypeStruct((SSD_HEADS, m), F32),
        ],
        compiler_params=_params(("parallel",)),
        name="even_proj",
    )(x, norm_w.reshape(1, d), w_main, w_dt, w_dtt)


def _sb_neg_tri(n):
    j = jnp.arange(n)[:, None]
    s = jnp.arange(n)[None, :]
    return -(j > s).astype(BF16)


def _softplus2(z2):
    return jnp.log2(1.0 + jnp.exp2(-jnp.abs(z2))) + jnp.maximum(z2, 0.0)


def _sb_tiles(qs, kbs, vbs, ntri, biases2, carries, accs, valid, row0):
    nh, nt = len(qs), len(kbs)
    t = kbs[0].shape[0]
    tail = lambda x, n: x[row0[n]:]
    put = lambda x, n, new: new if row0[n] == 0 else jnp.concatenate([x[:row0[n]], new], axis=0)
    z2 =[[_dot_nt(tail(qs[h], n), kbs[n]) + biases2[h] for h in range(nh)] for n in range(nt)]
    sp2 = [[_softplus2(z2[n][h]) for h in range(nh)] for n in range(nt)]
    sp2 = [[s if valid[n] is None else jnp.where(tail(valid[n], n), s, 0.0) for s in sp2[n]] for n in range(nt)]
    carry_at = []
    for n in range(nt):
        carry_at.append([tail(c, n) for c in carries])
        tot = [jnp.broadcast_to(jnp.sum(sp2[n][h], axis=1, keepdims=True), carry_at[n][h].shape) for h in range(nh)]
        carries = [put(carries[h], n, carry_at[n][h] - tot[h]) for h in range(nh)]
    rem2 = [[_dot(sp2[n][h].astype(BF16), ntri) + jnp.concatenate([carry_at[n][h]] * (t // LANES), axis=1)
             for h in range(nh)] for n in range(nt)]
    w = [[jnp.exp2((z2[n][h] - sp2[n][h]) + rem2[n][h]) for h in range(nh)] for n in range(nt)]
    w = [[x if valid[n] is None else jnp.where(tail(valid[n], n), x, 0.0) for x in w[n]] for n in range(nt)]
    for n in range(nt):
        pv = [_dot(w[n][h].astype(BF16), vbs[n]) for h in range(nh)]
        accs = [put(accs[h], n, tail(accs[h], n) + pv[h]) for h in range(nh)]
    return carries, accs


def _sb_prompt_kernel(bias_ref, q_ref, k_ref, v_ref, tri_ref, o_ref, c_scr, a_scr):
    p = pl.program_id(1)
    i = pl.program_id(2)
    tq = q_ref.shape[1]
    q = q_ref[0]
    lane = lax.broadcasted_iota(jnp.int32, (tq, LANES), 1)
    first = lane < SB_HEAD_DIM
    qs = (jnp.where(first, q, jnp.zeros_like(q)), jnp.where(first, jnp.zeros_like(q), q))
    biases = (bias_ref[2 * p] * LOG2E, bias_ref[2 * p + 1] * LOG2E)
    tri = tri_ref[...]
    c_scr[...] = jnp.zeros_like(c_scr)
    a_scr[...] = jnp.zeros_like(a_scr)

    tk = tri_ref.shape[0]

    def tiles(js, valid, row0):
        kbs = [k_ref[0, pl.ds(pl.multiple_of(j * tk, tk), tk), :] for j in js]
        vbs = [v_ref[0, pl.ds(pl.multiple_of(j * tk, tk), tk), :] for j in js]
        carry, acc = _sb_tiles(qs, kbs, vbs, tri, biases, [c_scr[0], c_scr[1]], [a_scr[0], a_scr[1]], valid, row0)
        for h in range(2):
            c_scr[h] = carry[h]
            a_scr[h] = acc[h]

    row = lax.broadcasted_iota(jnp.int32, (tq, tk), 0)
    col = lax.broadcasted_iota(jnp.int32, (tq, tk), 1)
    ratio = tq // tk
    for d in range(ratio - 1, 0, -2):
        tiles([ratio * i + d, ratio * i + d - 1], [col + d * tk < row, col + (d - 1) * tk < row], [d * tk, (d - 1) * tk])

    def body(n, _):
        j = ratio * i - 1 - 2 * n
        tiles([j, j - 1], [None, None], [0, 0])
        return 0

    lax.fori_loop(0, (ratio // 2) * i, body, 0)
    o_ref[0] = jnp.where(first, a_scr[0], a_scr[1]).astype(o_ref.dtype)


def _sb_prompt(q, k, v, sb_bias, bsz, t):
    npair, m, _ = q.shape
    tk = SB_TILE
    tq = SB_QUERY_TILES * tk
    nq = t // tq
    return pl.pallas_call(
        _sb_prompt_kernel,
        grid_spec=pltpu.PrefetchScalarGridSpec(
            num_scalar_prefetch=1,
            grid=(bsz, npair, nq),
            in_specs=[
                pl.BlockSpec((1, tq, LANES), lambda b, p, i, s: (p, b * nq + i, 0)),
                pl.BlockSpec((1, t, LANES), lambda b, p, i, s: (p, b, 0)),
                pl.BlockSpec((1, t, LANES), lambda b, p, i, s: (p, b, 0)),
                pl.BlockSpec((tk, tk), lambda b, p, i, s: (0, 0)),
            ],
            out_specs=pl.BlockSpec((1, tq, LANES), lambda b, p, i, s: (p, b * nq + i, 0)),
            scratch_shapes=[pltpu.VMEM((2, tq, LANES), F32), pltpu.VMEM((2, tq, LANES), F32)],
        ),
        out_shape=jax.ShapeDtypeStruct((npair, m, LANES), BF16),
        compiler_params=_params(("parallel", "parallel", "arbitrary")),
        name="sb_prompt",
    )(sb_bias.astype(F32), q, k, v, _sb_neg_tri(tk))


def _sb_decode_kernel(pt_ref, q_ref, bias_ref, *refs, group):
    k_refs, v_refs = refs[0:group], refs[group:2 * group]
    tri_ref, o_ref, c_scr, a_scr = refs[2 * group:]
    s = pl.program_id(0)
    g = pl.program_id(1)
    width = q_ref.shape[2]

    @pl.when(g == 0)
    def _():
        c_scr[...] = jnp.zeros_like(c_scr)
        a_scr[...] = jnp.zeros_like(a_scr)

    head = lax.broadcasted_iota(jnp.int32, (SB_HEADS, width), 0)
    lane = lax.broadcasted_iota(jnp.int32, (SB_HEADS, width), 1)
    own = (lane // SB_HEAD_DIM) == head
    qm = jnp.where(own, jnp.broadcast_to(q_ref[0], (SB_HEADS, width)), 0.0).astype(BF16)
    bias2 = jnp.concatenate([bias_ref[...] * LOG2E] * group, axis=0)
    z2 = jnp.concatenate([_dot(qm, k_refs[j][0].astype(BF16)) for j in range(group)], axis=0) + bias2
    sp2 = _softplus2(z2)
    hi = sp2.astype(BF16)
    lo = (sp2 - hi.astype(F32)).astype(BF16)
    tot = jnp.sum(sp2, axis=1, keepdims=True)
    carries = [c_scr[...]]
    for j in range(group):
        carries.append(carries[j] - tot[j * SB_HEADS:(j + 1) * SB_HEADS, :])
    c_scr[...] = carries[group]
    rem2 = _dot(hi, tri_ref[...]) + _dot(lo, tri_ref[...]) + jnp.concatenate(carries[0:group], axis=0)
    w = jnp.exp2((z2 - sp2) + rem2)
    acc = a_scr[...]
    for j in range(group):
        wj = w[j * SB_HEADS:(j + 1) * SB_HEADS, :].astype(BF16)
        acc = acc + _dot_nt(wj, v_refs[j][0].astype(BF16))
    a_scr[...] = acc

    @pl.when(g == pl.num_programs(1) - 1)
    def _():
        o = jnp.sum(jnp.where(own, acc, 0.0), axis=0, keepdims=True)
        for p in range(width // LANES):
            o_ref[p, pl.ds(s, 1), :] = o[:, p * LANES:(p + 1) * LANES]


def _sb_decode(q, cache_kt, cache_vt, page_table, sb_bias):
    n_seq, _, width = q.shape
    page = cache_kt.shape[2]
    npages = page_table.shape[1]
    group = SB_DECODE_GROUP if npages % SB_DECODE_GROUP == 0 else 1
    tri = _sb_neg_tri(page)

    def kv_spec(jj):
        return pl.BlockSpec((1, width, page), lambda b, g, pt: (pt[b, npages - 1 - (g * group + jj)], 0, 0))

    return pl.pallas_call(
        functools.partial(_sb_decode_kernel, group=group),
        grid_spec=pltpu.PrefetchScalarGridSpec(
            num_scalar_prefetch=1,
            grid=(n_seq, npages // group),
            in_specs=[
                pl.BlockSpec((1, 1, width), lambda b, g, pt: (b, 0, 0)),
                pl.BlockSpec((SB_HEADS, 1), lambda b, g, pt: (0, 0)),
                *[kv_spec(jj) for jj in range(group)],
                *[kv_spec(jj) for jj in range(group)],
                pl.BlockSpec((page, page), lambda b, g, pt: (0, 0)),
            ],
            out_specs=pl.BlockSpec((width // LANES, n_seq, LANES), lambda b, g, pt: (0, 0, 0)),
            scratch_shapes=[pltpu.VMEM((SB_HEADS, 1), F32), pltpu.VMEM((SB_HEADS, width), F32)],
        ),
        out_shape=jax.ShapeDtypeStruct((width // LANES, n_seq, LANES), F32),
        compiler_params=_params(("arbitrary", "arbitrary")),
        name="sb_decode",
    )(page_table, q, sb_bias.astype(F32).reshape(SB_HEADS, 1), *([cache_kt] * group), *([cache_vt] * group), tri)


def _ssd_chunk(xbc_ref, z_ref, dt_ref, dtt_ref, cw_ref, cb_ref, dtb_ref, dtbt_ref, alog_ref, alogt_ref,
               dskip_ref, nw_ref, y_ref, buf, st):
    L = xbc_ref.shape[0]
    pad = SUBLANES
    buf[pad:pad + L, :] = xbc_ref[...]
    conv = cb_ref[...]
    for j in range(SSD_CONV):
        off = pad - (SSD_CONV - 1) + j
        conv = conv + cw_ref[j:j + 1, :] * buf[off:off + L, :]
    tail = buf[pad + L - (SSD_CONV - 1):pad + L, :]
    buf[pad - (SSD_CONV - 1):pad, :] = tail
    xa = _silu(conv)
    xs = xa[:, 0:SSD_WIDTH]
    gw = SSD_STATE
    bm = [xa[:, SSD_WIDTH + g * gw:SSD_WIDTH + (g + 1) * gw].astype(BF16) for g in range(SSD_GROUPS)]
    cm = [xa[:, SSD_WIDTH + (SSD_GROUPS + g) * gw:SSD_WIDTH + (SSD_GROUPS + g + 1) * gw].astype(BF16)
          for g in range(SSD_GROUPS)]

    dt = _softplus(dt_ref[...] + dtb_ref[...])
    dtt = _softplus(dtt_ref[...] + dtbt_ref[...])
    a2 = -jnp.exp(alog_ref[...]) * LOG2E
    at2 = -jnp.exp(alogt_ref[...]) * LOG2E
    row = lax.broadcasted_iota(jnp.int32, (L, L), 0)
    col = lax.broadcasted_iota(jnp.int32, (L, L), 1)
    lower = col <= row
    lower_b = lower.astype(BF16)
    upper_b = (row <= col).astype(BF16)
    cum = sum(_dot(lower_b, part) for part in _split3(dt * a2))
    cumt = sum(_dot(part, upper_b) for part in _split3(dtt * at2))
    last = cum[L - 1:L, :]
    wst = jnp.exp2(last - cum) * dt
    ecum = jnp.exp2(cum)
    elast = jnp.exp2(last)

    lane = lax.broadcasted_iota(jnp.int32, (L, LANES), 1)
    first = lane < SSD_HEAD_DIM
    first_row = first[0:1, :]
    heads_per_group = SSD_HEADS // SSD_GROUPS
    cb = [jnp.where(lower, _dot_nt(cm[g], bm[g]), 0.0) for g in range(SSD_GROUPS)]
    ys = []
    for p in range(SSD_WIDTH // LANES):
        g = (2 * p) // heads_per_group
        xp = xs[:, p * LANES:(p + 1) * LANES]
        xpb = xp.astype(BF16)
        yi = []
        for h in (2 * p, 2 * p + 1):
            seg = cum[:, h:h + 1] - cumt[h:h + 1, :]
            dec = jnp.exp2(jnp.minimum(seg, 0.0)) * dtt[h:h + 1, :]
            yi.append(_dot((cb[g] * dec).astype(BF16), xpb))
        y_intra = jnp.where(first, yi[0], yi[1])
        stp = st[:, p * LANES:(p + 1) * LANES]
        ec = jnp.where(first, ecum[:, 2 * p:2 * p + 1], ecum[:, 2 * p + 1:2 * p + 2])
        y_inter = _dot(cm[g], stp.astype(BF16)) * ec
        wp = jnp.where(first, wst[:, 2 * p:2 * p + 1], wst[:, 2 * p + 1:2 * p + 2])
        el = jnp.where(first_row, elast[:, 2 * p:2 * p + 1], elast[:, 2 * p + 1:2 * p + 2])
        st[:, p * LANES:(p + 1) * LANES] = el * stp + _dot_tn(bm[g], (xp * wp).astype(BF16))
        ys.append(y_intra + y_inter + dskip_ref[:, p * LANES:(p + 1) * LANES] * xp)
    y = jnp.concatenate(ys, axis=1)
    y_ref[...] = _rmsnorm(y * _silu(z_ref[...].astype(F32)), nw_ref[...]).astype(y_ref.dtype)
    return tail


def _ssd_prompt_kernel(xbc_ref, z_ref, dt_ref, dtt_ref, *refs):
    params, (y_ref, conv_ref, ssm_ref, buf, st) = refs[:-5], refs[-5:]
    c = pl.program_id(1)

    @pl.when(c == 0)
    def _():
        buf[0:SUBLANES, :] = jnp.zeros((SUBLANES, SSD_CONV_CH), F32)
        st[...] = jnp.zeros_like(st)

    for s in range(xbc_ref.shape[0] // SSD_CHUNK):
        rows = pl.ds(s * SSD_CHUNK, SSD_CHUNK)
        tail = _ssd_chunk(xbc_ref.at[rows], z_ref.at[rows], dt_ref.at[rows], dtt_ref.at[:, rows], *params,
                          y_ref.at[rows], buf, st)

    @pl.when(c == pl.num_programs(1) - 1)
    def _():
        conv_ref[0] = tail
        ssm_ref[0] = st[...].T


def _ssd_prompt(xbc, z, dt, dtt, conv_w, conv_b, dt_bias, a_log, d_skip, ssd_norm, bsz, t):
    m = xbc.shape[0]
    L = SSD_CHUNK * SSD_STEP_CHUNKS
    nc = t // L
    rows = lambda b, c: (b * nc + c, 0)
    full = lambda b, c: (0, 0)
    vec = lambda v: v.astype(F32).reshape(1, -1)
    colv = lambda v: v.astype(F32).reshape(-1, 1)
    y, conv, ssm = pl.pallas_call(
        _ssd_prompt_kernel,
        grid=(bsz, nc),
        in_specs=[
            pl.BlockSpec((L, SSD_CONV_CH), rows),
            pl.BlockSpec((L, SSD_WIDTH), rows),
            pl.BlockSpec((L, SSD_HEADS), rows),
            pl.BlockSpec((SSD_HEADS, L), lambda b, c: (0, b * nc + c)),
            pl.BlockSpec((SSD_CONV, SSD_CONV_CH), full),
            pl.BlockSpec((1, SSD_CONV_CH), full),
            pl.BlockSpec((1, SSD_HEADS), full),
            pl.BlockSpec((SSD_HEADS, 1), full),
            pl.BlockSpec((1, SSD_HEADS), full),
            pl.BlockSpec((SSD_HEADS, 1), full),
            pl.BlockSpec((1, SSD_WIDTH), full),
            pl.BlockSpec((1, SSD_WIDTH), full),
        ],
        out_specs=[
            pl.BlockSpec((L, SSD_WIDTH), rows),
            pl.BlockSpec((1, SSD_CONV - 1, SSD_CONV_CH), lambda b, c: (b, 0, 0)),
            pl.BlockSpec((1, SSD_WIDTH, SSD_STATE), lambda b, c: (b, 0, 0)),
        ],
        out_shape=[
            jax.ShapeDtypeStruct((m, SSD_WIDTH), BF16),
            jax.ShapeDtypeStruct((bsz, SSD_CONV - 1, SSD_CONV_CH), F32),
            jax.ShapeDtypeStruct((bsz, SSD_WIDTH, SSD_STATE), F32),
        ],
        scratch_shapes=[pltpu.VMEM((SSD_CHUNK + SUBLANES, SSD_CONV_CH), F32),
                        pltpu.VMEM((SSD_STATE, SSD_WIDTH), F32)],
        compiler_params=_params(("parallel", "arbitrary")),
        name="ssd_prompt",
    )(xbc, z, dt, dtt, conv_w.astype(F32), vec(conv_b), vec(dt_bias), colv(dt_bias), vec(a_log), colv(a_log),
      vec(jnp.repeat(d_skip, SSD_HEAD_DIM)), vec(ssd_norm))
    return y, conv, ssm.reshape(bsz, SSD_HEADS, SSD_HEAD_DIM, SSD_STATE)


def _ssd_step_kernel(xbc_ref, z_ref, dt_ref, sconv_ref, sssm_ref, cw_ref, cb_ref, dtb_ref, alog_ref, dskip_ref,
                     nw_ref, y_ref, conv_ref, ssm_ref):
    lane_head = lax.broadcasted_iota(jnp.int32, (1, SSD_WIDTH), 1) // SSD_HEAD_DIM
    rows_per_group = SSD_WIDTH // SSD_GROUPS
    row = lax.broadcasted_iota(jnp.int32, (SSD_WIDTH, SSD_STATE), 0)
    a = -jnp.exp(alog_ref[...])
    for s in range(xbc_ref.shape[0]):
        xr = xbc_ref[s]
        cs = sconv_ref[s]
        conv = cb_ref[...] + cw_ref[SSD_CONV - 1:SSD_CONV, :] * xr
        for j in range(SSD_CONV - 1):
            conv = conv + cw_ref[j:j + 1, :] * cs[j:j + 1, :]
        conv_ref[s] = jnp.concatenate([cs[1:SSD_CONV - 1, :], xr], axis=0)
        xa = _silu(conv)
        xs = xa[:, 0:SSD_WIDTH]
        dt = _softplus(dt_ref[s] + dtb_ref[...])
        da = jnp.exp(dt * a)
        dt_w = jnp.zeros((1, SSD_WIDTH), F32)
        da_w = jnp.zeros((1, SSD_WIDTH), F32)
        for h in range(SSD_HEADS):
            dt_w = jnp.where(lane_head == h, dt[:, h:h + 1], dt_w)
            da_w = jnp.where(lane_head == h, da[:, h:h + 1], da_w)
        dtx_col = _row_to_col(dt_w * xs)
        da_col = _row_to_col(da_w)
        b_rows = jnp.zeros((SSD_WIDTH, SSD_STATE), F32)
        c_rows = jnp.zeros((SSD_WIDTH, SSD_STATE), F32)
        for g in range(SSD_GROUPS):
            sel = (row // rows_per_group) == g
            b0 = SSD_WIDTH + g * SSD_STATE
            c0 = SSD_WIDTH + (SSD_GROUPS + g) * SSD_STATE
            b_rows = jnp.where(sel, xa[:, b0:b0 + SSD_STATE], b_rows)
            c_rows = jnp.where(sel, xa[:, c0:c0 + SSD_STATE], c_rows)
        new = da_col * sssm_ref[s] + dtx_col * b_rows
        ssm_ref[s] = new
        y = _col_to_row(jnp.sum(new * c_rows, axis=1, keepdims=True)) + dskip_ref[...] * xs
        y_ref[s] = _rmsnorm(y * _silu(z_ref[s].astype(F32)), nw_ref[...])


def _ssd_step(xbc, z, dt, state_conv, state_ssm, conv_w, conv_b, dt_bias, a_log, d_skip, ssd_norm):
    n = xbc.shape[0]
    ns = STEP_SEQS if n % STEP_SEQS == 0 else 1
    vec = lambda v: v.astype(F32).reshape(1, -1)
    per = lambda i: (i, 0, 0)
    full = lambda i: (0, 0)
    y, conv, ssm = pl.pallas_call(
        _ssd_step_kernel,
        grid=(n // ns,),
        in_specs=[
            pl.BlockSpec((ns, 1, SSD_CONV_CH), per),
            pl.BlockSpec((ns, 1, SSD_WIDTH), per),
            pl.BlockSpec((ns, 1, SSD_HEADS), per),
            pl.BlockSpec((ns, SSD_CONV - 1, SSD_CONV_CH), per),
            pl.BlockSpec((ns, SSD_WIDTH, SSD_STATE), per),
            pl.BlockSpec((SSD_CONV, SSD_CONV_CH), full),
            pl.BlockSpec((1, SSD_CONV_CH), full),
            pl.BlockSpec((1, SSD_HEADS), full),
            pl.BlockSpec((1, SSD_HEADS), full),
            pl.BlockSpec((1, SSD_WIDTH), full),
            pl.BlockSpec((1, SSD_WIDTH), full),
        ],
        out_specs=[
            pl.BlockSpec((ns, 1, SSD_WIDTH), per),
            pl.BlockSpec((ns, SSD_CONV - 1, SSD_CONV_CH), per),
            pl.BlockSpec((ns, SSD_WIDTH, SSD_STATE), per),
        ],
        out_shape=[
            jax.ShapeDtypeStruct((n, 1, SSD_WIDTH), F32),
            jax.ShapeDtypeStruct((n, SSD_CONV - 1, SSD_CONV_CH), F32),
            jax.ShapeDtypeStruct((n, SSD_WIDTH, SSD_STATE), F32),
        ],
        compiler_params=_params(("parallel",)),
        name="ssd_step",
    )(xbc.reshape(n, 1, SSD_CONV_CH), z.reshape(n, 1, SSD_WIDTH), dt.reshape(n, 1, SSD_HEADS),
      state_conv.astype(F32), state_ssm.astype(F32).reshape(n, SSD_WIDTH, SSD_STATE), conv_w.astype(F32),
      vec(conv_b), vec(dt_bias), vec(a_log), vec(jnp.repeat(d_skip, SSD_HEAD_DIM)), vec(ssd_norm))
    return y.reshape(n, SSD_WIDTH), conv, ssm.reshape(n, SSD_HEADS, SSD_HEAD_DIM, SSD_STATE)


def _even_mix(osb_ref, y_ref):
    return jnp.concatenate([osb_ref[p] for p in range(osb_ref.shape[0])] + [y_ref[...]], axis=1)


def _odd_mix(o_ref, g_ref, nw_ref):
    o = jnp.concatenate([o_ref[h] for h in range(o_ref.shape[0])], axis=1).astype(F32)
    return (_rmsnorm(o, nw_ref[...]) * _silu(g_ref[...].astype(F32))).astype(BF16)


def _mix_mlp_kernel(*refs, mix_fn, n_mix, final_norm):
    mix_refs = refs[0:n_mix]
    x_ref, wout_ref, nw_ref, w1_ref, w2_ref, fw_ref, o_ref, hn_scr, acc_scr = refs[n_mix:]
    j = pl.program_id(1)

    @pl.when(j == 0)
    def _():
        h1 = x_ref[...] + _dot(mix_fn(*mix_refs), wout_ref[...])
        o_ref[...] = h1
        hn_scr[...] = _rmsnorm(h1, nw_ref[...]).astype(BF16)

    h = jnp.maximum(_dot(hn_scr[...], w1_ref[...]), 0.0)
    a = (h * h).astype(BF16)

    @pl.when(j == 0)
    def _():
        acc_scr[...] = _dot(a, w2_ref[...])

    @pl.when(j > 0)
    def _():
        acc_scr[...] += _dot(a, w2_ref[...])

    @pl.when(j == pl.num_programs(1) - 1)
    def _():
        out = o_ref[...] + acc_scr[...]
        o_ref[...] = _rmsnorm(out, fw_ref[...]) if final_norm else out


def _mix_mlp(mix_fn, mix_args, mix_specs, x, w_out, norm_w, w1_all, w2_all, layer, final_w, tm, tf):
    m, d = x.shape
    f = w1_all.shape[2]
    final_norm = final_w is not None
    fw = (final_w if final_norm else jnp.ones((d,), F32)).astype(F32).reshape(1, d)
    const = lambda i, j: (0, 0)
    return pl.pallas_call(
        functools.partial(_mix_mlp_kernel, mix_fn=mix_fn, n_mix=len(mix_args), final_norm=final_norm),
        grid=(m // tm, f // tf),
        in_specs=[
            *mix_specs,
            pl.BlockSpec((tm, d), lambda i, j: (i, 0)),
            pl.BlockSpec(w_out.shape, const),
            pl.BlockSpec((1, d), const),
            pl.BlockSpec((None, d, tf), lambda i, j: (layer, 0, j)),
            pl.BlockSpec((None, tf, d), lambda i, j: (layer, j, 0)),
            pl.BlockSpec((1, d), const),
        ],
        out_specs=pl.BlockSpec((tm, d), lambda i, j: (i, 0)),
        out_shape=jax.ShapeDtypeStruct((m, d), F32),
        scratch_shapes=[pltpu.VMEM((tm, d), BF16), pltpu.VMEM((tm, d), F32)],
        compiler_params=_params(("parallel", "arbitrary")),
        name="mix_mlp",
    )(*mix_args, x, w_out.astype(BF16), norm_w.astype(F32).reshape(1, d), w1_all, w2_all, fw)


def _even_mix_mlp(osb, y, x, w_out, norm_w, w1_all, w2_all, layer, final_w, tm, tf):
    specs = [pl.BlockSpec((osb.shape[0], tm, LANES), lambda i, j: (0, i, 0)),
             pl.BlockSpec((tm, SSD_WIDTH), lambda i, j: (i, 0))]
    return _mix_mlp(_even_mix, (osb, y), specs, x, w_out, norm_w, w1_all, w2_all, layer, final_w, tm, tf)


def _odd_mix_mlp(o, g, hg_norm, x, w_out, norm_w, w1_all, w2_all, layer, final_w, tm, tf):
    nh = o.shape[0]
    specs = [pl.BlockSpec((nh, tm, HG_VAL), lambda i, j: (0, i, 0)),
             pl.BlockSpec((tm, nh * HG_VAL), lambda i, j: (i, 0)),
             pl.BlockSpec((1, nh * HG_VAL), lambda i, j: (0, 0))]
    args = (o, g, hg_norm.astype(F32).reshape(1, -1))
    return _mix_mlp(_odd_mix, args, specs, x, w_out, norm_w, w1_all, w2_all, layer, final_w, tm, tf)


def _odd_proj_kernel(x_ref, nw_ref, w_ref, q_ref, f_ref, v_ref, g_ref):
    hn = _rmsnorm(x_ref[...], nw_ref[...]).astype(BF16)
    width = HG_HEADS * HG_KEY
    per_chunk = PROJ_COLS // HG_KEY
    for i, ref in enumerate((q_ref, f_ref, v_ref)):
        for c in range(width // PROJ_COLS):
            y = _dot(hn, w_ref[:, i * width + c * PROJ_COLS:i * width + (c + 1) * PROJ_COLS])
            for h in range(per_chunk):
                ref[c * per_chunk + h] = y[:, h * HG_KEY:(h + 1) * HG_KEY]
    for c in range(width // PROJ_COLS):
        g_ref[:, c * PROJ_COLS:(c + 1) * PROJ_COLS] = _dot(
            hn, w_ref[:, 3 * width + c * PROJ_COLS:3 * width + (c + 1) * PROJ_COLS]).astype(g_ref.dtype)


def _odd_proj(x, norm_w, w_in, tm):
    m, d = x.shape
    width = HG_HEADS * HG_KEY
    heads = lambda i: (0, i, 0)
    per_head = jax.ShapeDtypeStruct((HG_HEADS, m, HG_KEY), F32)
    return pl.pallas_call(
        _odd_proj_kernel,
        grid=(m // tm,),
        in_specs=[
            pl.BlockSpec((tm, d), lambda i: (i, 0)),
            pl.BlockSpec((1, d), lambda i: (0, 0)),
            pl.BlockSpec(w_in.shape, lambda i: (0, 0)),
        ],
        out_specs=[
            pl.BlockSpec((HG_HEADS, tm, HG_KEY), heads),
            pl.BlockSpec((HG_HEADS, tm, HG_KEY), heads),
            pl.BlockSpec((HG_HEADS, tm, HG_KEY), heads),
            pl.BlockSpec((tm, width), lambda i: (i, 0)),
        ],
        out_shape=[per_head, per_head, per_head, jax.ShapeDtypeStruct((m, width), BF16)],
        compiler_params=_params(("parallel",)),
        name="odd_proj",
    )(x, norm_w.astype(F32).reshape(1, d), w_in.astype(BF16))


def _hgrn_lower_bound(raw, layer):
    e = jnp.exp(raw - jnp.max(raw, axis=0, keepdims=True))
    p = e / jnp.sum(e, axis=0, keepdims=True)
    lb = jnp.zeros_like(p[0])
    for l in range(1, layer + 1):
        lb = lb + p[l]
    return lb


def _hgrn_gates(fpre, lb):
    sig, nsig = _sigmoid_pair(fpre)
    return lb + (1.0 - lb) * sig, (1.0 - lb) * nsig


def _hgrn_chunk(q_ref, f_ref, v_ref, lb_ref, o_ref, st, b_scr, k_scr, od_scr, layer):
    L = q_ref.shape[1]
    nblk = L // HG_DIAG
    row = lax.broadcasted_iota(jnp.int32, (L, L), 0)
    col = lax.broadcasted_iota(jnp.int32, (L, L), 1)
    lower_b = (col <= row).astype(BF16)
    pos = lax.broadcasted_iota(jnp.int32, (L, HG_KEY), 0)

    def heads(n, _):
        hs = [n * HG_HEAD_GROUP + e for e in range(HG_HEAD_GROUP)]
        every = range(HG_HEAD_GROUP)
        q = [q_ref[h] for h in hs]
        vb = [v_ref[h].astype(BF16) for h in hs]
        gates = [_hgrn_gates(f_ref[h], _hgrn_lower_bound(lb_ref[:, h], layer)) for h in hs]
        kin = [gk[1] for gk in gates]
        b = [sum(_dot(lower_b, part) for part in _split3(jnp.log(gk[0]) * LOG2E)) for gk in gates]
        for e in every:
            b_scr[e] = b[e]
            k_scr[e] = kin[e]
        stt = [st[h] for h in hs]
        o = [_dot_nt((q[e] * jnp.exp2(b[e])).astype(BF16), stt[e].astype(BF16)) for e in every]
        b_last = [x[L - 1:L, :] for x in b]
        for e in every:
            st[hs[e]] = stt[e] * jnp.exp2(b_last[e]) + _dot_tn(
                vb[e], (kin[e] * jnp.exp2(b_last[e] - b[e])).astype(BF16))

        att = [jnp.zeros((L, L), F32) for _ in every]
        g = L
        while g > HG_DIAG:
            half = g // 2
            late = (pos & (g - 1)) >= half
            pair = ((row ^ col) < g) & ((row & (g - 1)) >= half) & ((col & (g - 1)) < half)
            mid = [jnp.broadcast_to(x.reshape(L // g, g, HG_KEY)[:, half - 1:half, :],
                                    (L // g, g, HG_KEY)).reshape(L, HG_KEY) for x in b]
            r = [(jnp.where(late, q[e], kin[e]) * jnp.exp2(-jnp.abs(b[e] - mid[e]))).astype(BF16) for e in every]
            att = [jnp.where(pair, _dot_nt(r[e], r[e]), att[e]) for e in every]
            g = half
        o = [o[e] + _dot(att[e].astype(BF16), vb[e]) for e in every]

        at = lambda ref, i: ref[pl.ds(i, nblk, stride=HG_DIAG), :]
        for e in every:
            bs = [at(b_scr.at[e], i) for i in range(HG_DIAG)]
            ks = [at(k_scr.at[e], i) for i in range(HG_DIAG)]
            vs = [at(v_ref.at[hs[e]], i) for i in range(HG_DIAG)]
            for i in range(HG_DIAG):
                qi = at(q_ref.at[hs[e]], i)
                acc = jnp.zeros((nblk, HG_VAL), F32)
                for j in range(i + 1):
                    a = jnp.sum(qi * ks[j] * jnp.exp2(bs[i] - bs[j]), axis=1, keepdims=True)
                    acc = acc + a * vs[j]
                od_scr[e, pl.ds(i, nblk, stride=HG_DIAG), :] = acc
        for e in every:
            o_ref[hs[e]] = (o[e] + od_scr[e]).astype(o_ref.dtype)
        return 0

    groups = q_ref.shape[0] // HG_HEAD_GROUP
    if groups == 1:
        heads(0, 0)
    else:
        lax.fori_loop(0, groups, heads, 0)


def _hgrn_prompt_kernel(q_ref, f_ref, v_ref, lb_ref, o_ref, s_ref, st, b_scr, k_scr, od_scr, *, layer):
    c = pl.program_id(1)

    @pl.when(c == 0)
    def _():
        st[...] = jnp.zeros_like(st)

    for s in range(q_ref.shape[1] // HG_CHUNK):
        rows = pl.ds(s * HG_CHUNK, HG_CHUNK)
        _hgrn_chunk(q_ref.at[:, rows], f_ref.at[:, rows], v_ref.at[:, rows], lb_ref, o_ref.at[:, rows], st,
                    b_scr.at[s], k_scr.at[s], od_scr.at[s], layer)

    @pl.when(c == pl.num_programs(1) - 1)
    def _():
        for h in range(s_ref.shape[1]):
            s_ref[0, h] = st[h].T


def _hgrn_prompt(q, f, v, lb_raw, layer, bsz, t):
    nh, m, _ = q.shape
    L = HG_CHUNK * HG_STEP_CHUNKS
    nc = t // L
    blk = lambda b, c: (0, b * nc + c, 0)
    depth = lb_raw.shape[0]
    return pl.pallas_call(
        functools.partial(_hgrn_prompt_kernel, layer=layer),
        grid=(bsz, nc),
        in_specs=[
            pl.BlockSpec((nh, L, HG_KEY), blk),
            pl.BlockSpec((nh, L, HG_KEY), blk),
            pl.BlockSpec((nh, L, HG_VAL), blk),
            pl.BlockSpec((depth, nh, 1, HG_KEY), lambda b, c: (0, 0, 0, 0)),
        ],
        out_specs=[
            pl.BlockSpec((nh, L, HG_VAL), blk),
            pl.BlockSpec((1, nh, HG_KEY, HG_VAL), lambda b, c: (b, 0, 0, 0)),
        ],
        out_shape=[
            jax.ShapeDtypeStruct((nh, m, HG_VAL), BF16),
            jax.ShapeDtypeStruct((bsz, nh, HG_KEY, HG_VAL), F32),
        ],
        scratch_shapes=[
            pltpu.VMEM((nh, HG_VAL, HG_KEY), F32),
            pltpu.VMEM((HG_STEP_CHUNKS, HG_HEAD_GROUP, HG_CHUNK, HG_KEY), F32),
            pltpu.VMEM((HG_STEP_CHUNKS, HG_HEAD_GROUP, HG_CHUNK, HG_KEY), F32),
            pltpu.VMEM((HG_STEP_CHUNKS, HG_HEAD_GROUP, HG_CHUNK, HG_VAL), F32),
        ],
        compiler_params=_params(("parallel", "arbitrary")),
        name="hgrn_prompt",
    )(q, f, v, lb_raw.astype(F32).reshape(depth, nh, 1, HG_KEY))


def _hgrn_step_kernel(q_ref, f_ref, v_ref, lb_ref, s_ref, o_ref, snew_ref, *, layer):
    ns = s_ref.shape[0]
    for h in range(q_ref.shape[0]):
        lb = _hgrn_lower_bound(lb_ref[:, h], layer)
        for s in range(ns):
            i = pl.program_id(0) * ns + s
            q = q_ref[h, pl.ds(i, 1), :]
            v = v_ref[h, pl.ds(i, 1), :]
            fg, kin = _hgrn_gates(f_ref[h, pl.ds(i, 1), :], lb)
            new = _row_to_col(fg) * s_ref[s, h] + _row_to_col(kin) * v
            snew_ref[s, h] = new
            o_ref[h, pl.ds(i, 1), :] = jnp.sum(_row_to_col(q) * new, axis=0, keepdims=True)


def _hgrn_step(q, f, v, lb_raw, state, layer):
    nh, n, _ = q.shape
    ns = STEP_SEQS if n % STEP_SEQS == 0 else 1
    depth = lb_raw.shape[0]
    whole = lambda i: (0, 0, 0)
    return pl.pallas_call(
        functools.partial(_hgrn_step_kernel, layer=layer),
        grid=(n // ns,),
        in_specs=[
            pl.BlockSpec((nh, n, HG_KEY), whole),
            pl.BlockSpec((nh, n, HG_KEY), whole),
            pl.BlockSpec((nh, n, HG_VAL), whole),
            pl.BlockSpec((depth, nh, 1, HG_KEY), lambda i: (0, 0, 0, 0)),
            pl.BlockSpec((ns, nh, HG_KEY, HG_VAL), lambda i: (i, 0, 0, 0)),
        ],
        out_specs=[
            pl.BlockSpec((nh, n, HG_VAL), whole),
            pl.BlockSpec((ns, nh, HG_KEY, HG_VAL), lambda i: (i, 0, 0, 0)),
        ],
        out_shape=[
            jax.ShapeDtypeStruct((nh, n, HG_VAL), F32),
            jax.ShapeDtypeStruct((n, nh, HG_KEY, HG_VAL), F32),
        ],
        compiler_params=_params(("arbitrary",)),
        name="hgrn_step",
    )(q, f, v, lb_raw.astype(F32).reshape(depth, nh, 1, HG_KEY), state.astype(F32))


def _row_tile(m, want):
    return want if m % want == 0 else m


def kernel(x_prompt, x_sample, cache_k, cache_v, page_table, state_conv, state_ssm, state_hgrn, norm_mix, norm_ffn,
           norm_final, w_in_even, sb_bias, conv_w, conv_b, dt_bias, a_log, d_skip, ssd_norm, w_out_even, w_in_odd,
           hg_lb_raw, hg_norm, w_out_odd, w_ff1, w_ff2):
    bsz, t, d = x_prompt.shape
    n_seq = x_sample.shape[0]
    depth = norm_mix.shape[0]
    mp = bsz * t
    hp = x_prompt.reshape(mp, d)
    hs = x_sample.reshape(n_seq, d)
    tmp = _row_tile(mp, 512)
    tmm = _row_tile(mp, 1024)
    tf = 1024
    outs = {k: [] for k in ("kp", "vp", "ks", "vs", "cp", "cs", "sp", "ss", "gp", "gs")}
    w1_all = w_ff1.astype(BF16)
    w2_all = w_ff2.astype(BF16)

    for layer in range(depth):
        li = layer // 2
        fw = norm_final if layer == depth - 1 else None
        mlp_w = (norm_ffn[layer], w1_all, w2_all, layer, fw)
        if layer % 2 == 0:
            ssd_w = (conv_w[li], conv_b[li], dt_bias[li], a_log[li], d_skip[li], ssd_norm[li])
            q, k, v, kf, vf, z, xbc, dt, dtt = _even_proj(hp, norm_mix[layer], w_in_even[li], tmp, seq_len=t)
            osb = _sb_prompt(q, k, v, sb_bias[li], bsz, t)
            y, cp, sp = _ssd_prompt(xbc, z, dt, dtt, *ssd_w, bsz, t)
            hp = _even_mix_mlp(osb, y, hp, w_out_even[li], *mlp_w, tmm, tf)
            rows_view = lambda a: jnp.transpose(a.reshape(bsz, SB_HEADS, SB_HEAD_DIM, t), (0, 3, 1, 2))
            outs["kp"].append(rows_view(kf))
            outs["vp"].append(rows_view(vf))
            outs["cp"].append(cp)
            outs["sp"].append(sp)
            q, k, v, kf, vf, z, xbc, dt, dtt = _even_proj(hs, norm_mix[layer], w_in_even[li], n_seq)
            qs = q.astype(F32).transpose(1, 0, 2).reshape(n_seq, 1, SB_WIDTH)
            n_phys, page = cache_k.shape[1], cache_k.shape[2]
            to_lanes = lambda c: jnp.transpose(c, (0, 2, 3, 1)).reshape(n_phys, SB_WIDTH, page)
            osb = _sb_decode(qs, to_lanes(cache_k[li]), to_lanes(cache_v[li]), page_table, sb_bias[li])
            y, cs, ss = _ssd_step(xbc, z, dt, state_conv[li], state_ssm[li], *ssd_w)
            hs = _even_mix_mlp(osb.astype(BF16), y.astype(BF16), hs, w_out_even[li], *mlp_w, n_seq, tf)
            outs["ks"].append(kf.reshape(n_seq, 1, SB_HEADS, SB_HEAD_DIM))
            outs["vs"].append(vf.reshape(n_seq, 1, SB_HEADS, SB_HEAD_DIM))
            outs["cs"].append(cs)
            outs["ss"].append(ss)
        else:
            q, f, v, g = _odd_proj(hp, norm_mix[layer], w_in_odd[li], tmp)
            o, gp = _hgrn_prompt(q, f, v, hg_lb_raw, layer, bsz, t)
            hp = _odd_mix_mlp(o, g, hg_norm[li], hp, w_out_odd[li], *mlp_w, tmm, tf)
            outs["gp"].append(gp)
            q, f, v, g = _odd_proj(hs, norm_mix[layer], w_in_odd[li], n_seq)
            o, gs = _hgrn_step(q, f, v, hg_lb_raw, state_hgrn[li], layer)
            hs = _odd_mix_mlp(o.astype(BF16), g, hg_norm[li], hs, w_out_odd[li], *mlp_w, n_seq, tf)
            outs["gs"].append(gs)

    y_prompt = hp.reshape(bsz, t, d)
    y_sample = hs.reshape(n_seq, 1, d)
    st = lambda key: jnp.stack(outs[key])
    return (y_prompt, y_sample, st("kp"), st("vp"), st("ks"), st("vs"), st("cp"), st("cs"), st("sp"), st("ss"),
            st("gp"), st("gs"))
```

```python
import functools

import jax
import jax.numpy as jnp
from jax import lax
from jax.experimental import pallas as pl
from jax.experimental.pallas import tpu as pltpu

F32 = jnp.float32
BF16 = jnp.bfloat16

EPS = 1e-6
LOG2E = 1.4426950408889634
SB_HEADS = 8
SB_HEAD_DIM = 64
SB_WIDTH = SB_HEADS * SB_HEAD_DIM
SSD_HEADS = 8
SSD_HEAD_DIM = 64
SSD_WIDTH = SSD_HEADS * SSD_HEAD_DIM
SSD_STATE = 128
SSD_GROUPS = 2
SSD_CONV = 4
SSD_CONV_CH = SSD_WIDTH + 2 * SSD_GROUPS * SSD_STATE
HG_HEADS = 8
HG_KEY = 128
HG_VAL = 128

LANES = 128
SUBLANES = 8
VMEM_LIMIT_BYTES = 52 * 1024 * 1024

SB_TILE = 256
SB_QUERY_TILES = 8
SB_DECODE_GROUP = 32
SSD_CHUNK = 128
SSD_STEP_CHUNKS = 2
HG_STEP_CHUNKS = 4
HG_CHUNK = 128
HG_DIAG = 4
HG_HEAD_GROUP = 8
STEP_SEQS = 4
PROJ_COLS = 512


def _params(sem):
    return pltpu.CompilerParams(dimension_semantics=sem, vmem_limit_bytes=VMEM_LIMIT_BYTES)


def _dot(a, b):
    return jnp.dot(a, b, preferred_element_type=F32)


def _dot_nt(a, b):
    return lax.dot_general(a, b, (((1,), (1,)), ((), ())), preferred_element_type=F32)


def _dot_tn(a, b):
    return lax.dot_general(a, b, (((0,), (0,)), ((), ())), preferred_element_type=F32)


def _split3(x):
    hi = x.astype(BF16)
    r = x - hi.astype(F32)
    mid = r.astype(BF16)
    lo = (r - mid.astype(F32)).astype(BF16)
    return hi, mid, lo


def _rmsnorm(x, w):
    return x * lax.rsqrt(jnp.mean(x * x, axis=-1, keepdims=True) + EPS) * w


def _softplus(x):
    return jnp.maximum(x, 0.0) + jnp.log1p(jnp.exp(-jnp.abs(x)))


def _sigmoid_pair(x):
    e = jnp.exp(-jnp.abs(x))
    r = 1.0 / (1.0 + e)
    big, small = r, e * r
    pos = x >= 0
    return jnp.where(pos, big, small), jnp.where(pos, small, big)


def _silu(x):
    return x / (1.0 + jnp.exp(-x))


def _row_to_col(row):
    n = row.shape[1]
    eye = lax.broadcasted_iota(jnp.int32, (n, n), 0) == lax.broadcasted_iota(jnp.int32, (n, n), 1)
    return jnp.sum(jnp.where(eye, jnp.broadcast_to(row, (n, n)), 0.0), axis=1, keepdims=True)


def _col_to_row(col):
    n = col.shape[0]
    eye = lax.broadcasted_iota(jnp.int32, (n, n), 0) == lax.broadcasted_iota(jnp.int32, (n, n), 1)
    return jnp.sum(jnp.where(eye, jnp.broadcast_to(col, (n, n)), 0.0), axis=0, keepdims=True)


def _even_proj_kernel(x_ref, nw_ref, w_ref, wdt_ref, wdtt_ref,
                      q_ref, k_ref, v_ref, kf_ref, vf_ref, z_ref, xbc_ref, dt_ref, dtt_ref, *, token_minor):
    hn = _rmsnorm(x_ref[...], nw_ref[...]).astype(BF16)
    scale = LOG2E * SB_HEAD_DIM ** -0.5
    npair = SB_WIDTH // LANES
    q = _dot(hn, w_ref[:, 0:SB_WIDTH]) * scale
    for p in range(npair):
        q_ref[p] = q[:, p * LANES:(p + 1) * LANES].astype(BF16)
    for i, (pair_ref, full_ref) in enumerate(((k_ref, kf_ref), (v_ref, vf_ref))):
        y = _dot(hn, w_ref[:, (i + 1) * SB_WIDTH:(i + 2) * SB_WIDTH])
        if token_minor:
            full_ref[0] = y.T
        else:
            full_ref[...] = y
        for p in range(npair):
            pair_ref[p] = y[:, p * LANES:(p + 1) * LANES].astype(BF16)
    z0 = 3 * SB_WIDTH
    z_ref[...] = _dot(hn, w_ref[:, z0:z0 + SSD_WIDTH]).astype(z_ref.dtype)
    x0 = z0 + SSD_WIDTH
    for c in range(SSD_CONV_CH // PROJ_COLS):
        xbc_ref[:, c * PROJ_COLS:(c + 1) * PROJ_COLS] = _dot(
            hn, w_ref[:, x0 + c * PROJ_COLS:x0 + (c + 1) * PROJ_COLS])
    dt_ref[...] = _dot(hn, wdt_ref[...])[:, 0:SSD_HEADS]
    dtt_ref[...] = _dot_nt(wdtt_ref[...], hn)


def _even_proj(x, norm_w, w_in, tm, seq_len=None):
    m, d = x.shape
    npair = SB_WIDTH // LANES
    wb = w_in.astype(BF16)
    n_main = 3 * SB_WIDTH + SSD_WIDTH + SSD_CONV_CH
    w_main = wb
    w_dt = jnp.pad(wb[:, n_main:], ((0, 0), (0, LANES - SSD_HEADS)))
    w_dtt = wb[:, n_main:].T
    full = lambda i: (0, 0)
    rows = lambda i: (i, 0)
    hp = lambda i: (0, i, 0)
    token_minor = seq_len is not None
    assert m % tm == 0 and d % LANES == 0, (m, tm, d)
    if token_minor:
        assert seq_len % tm == 0 and m % seq_len == 0, (m, seq_len, tm)
        nt = seq_len // tm
        kv_spec = pl.BlockSpec((1, SB_WIDTH, tm), lambda i: (i // nt, 0, i % nt))
        kv_shape = jax.ShapeDtypeStruct((m // seq_len, SB_WIDTH, seq_len), F32)
    else:
        kv_spec = pl.BlockSpec((tm, SB_WIDTH), rows)
        kv_shape = jax.ShapeDtypeStruct((m, SB_WIDTH), F32)
    return pl.pallas_call(
        functools.partial(_even_proj_kernel, token_minor=token_minor),
        grid=(m // tm,),
        in_specs=[
            pl.BlockSpec((tm, d), rows),
            pl.BlockSpec((1, d), full),
            pl.BlockSpec(wb.shape, full),
            pl.BlockSpec((d, LANES), full),
            pl.BlockSpec((SSD_HEADS, d), full),
        ],
        out_specs=[
            pl.BlockSpec((npair, tm, LANES), hp),
            pl.BlockSpec((npair, tm, LANES), hp),
            pl.BlockSpec((npair, tm, LANES), hp),
            kv_spec,
            kv_spec,
            pl.BlockSpec((tm, SSD_WIDTH), rows),
            pl.BlockSpec((tm, SSD_CONV_CH), rows),
            pl.BlockSpec((tm, SSD_HEADS), rows),
            pl.BlockSpec((SSD_HEADS, tm), lambda i: (0, i)),
        ],
        out_shape=[
            jax.ShapeDtypeStruct((npair, m, LANES), BF16),
            jax.ShapeDtypeStruct((npair, m, LANES), BF16),
            jax.ShapeDtypeStruct((npair, m, LANES), BF16),
            kv_shape,
            kv_shape,
            jax.ShapeDtypeStruct((m, SSD_WIDTH), BF16),
            jax.ShapeDtypeStruct((m, SSD_CONV_CH), F32),
            jax.ShapeDtypeStruct((m, SSD_HEADS), F32),
            jax.ShapeDtypeStruct((SSD_HEADS, m), F32),
        ],
        compiler_params=_params(("parallel",)),
        name="even_proj",
    )(x, norm_w.reshape(1, d), w_main, w_dt, w_dtt)


def _sb_neg_tri(n):
    j = jnp.arange(n)[:, None]
    s = jnp.arange(n)[None, :]
    return -(j > s).astype(BF16)


def _softplus2(z2):
    return jnp.log2(1.0 + jnp.exp2(-jnp.abs(z2))) + jnp.maximum(z2, 0.0)


def _sb_tiles(qs, kbs, vbs, ntri, biases2, carries, accs, valid, row0):
    nh, nt = len(qs), len(kbs)
    t = kbs[0].shape[0]
    tail = lambda x, n: x[row0[n]:]
    put = lambda x, n, new: new if row0[n] == 0 else jnp.concatenate([x[:row0[n]], new], axis=0)
    z2 =[[_dot_nt(tail(qs[h], n), kbs[n]) + biases2[h] for h in range(nh)] for n in range(nt)]
    sp2 = [[_softplus2(z2[n][h]) for h in range(nh)] for n in range(nt)]
    sp2 = [[s if valid[n] is None else jnp.where(tail(valid[n], n), s, 0.0) for s in sp2[n]] for n in range(nt)]
    carry_at = []
    for n in range(nt):
        carry_at.append([tail(c, n) for c in carries])
        tot = [jnp.broadcast_to(jnp.sum(sp2[n][h], axis=1, keepdims=True), carry_at[n][h].shape) for h in range(nh)]
        carries = [put(carries[h], n, carry_at[n][h] - tot[h]) for h in range(nh)]
    rem2 = [[_dot(sp2[n][h].astype(BF16), ntri) + jnp.concatenate([carry_at[n][h]] * (t // LANES), axis=1)
             for h in range(nh)] for n in range(nt)]
    w = [[jnp.exp2((z2[n][h] - sp2[n][h]) + rem2[n][h]) for h in range(nh)] for n in range(nt)]
    w = [[x if valid[n] is None else jnp.where(tail(valid[n], n), x, 0.0) for x in w[n]] for n in range(nt)]
    for n in range(nt):
        pv = [_dot(w[n][h].astype(BF16), vbs[n]) for h in range(nh)]
        accs = [put(accs[h], n, tail(accs[h], n) + pv[h]) for h in range(nh)]
    return carries, accs


def _sb_prompt_kernel(bias_ref, q_ref, k_ref, v_ref, tri_ref, o_ref, c_scr, a_scr):
    p = pl.program_id(1)
    i = pl.program_id(2)
    tq = q_ref.shape[1]
    q = q_ref[0]
    lane = lax.broadcasted_iota(jnp.int32, (tq, LANES), 1)
    first = lane < SB_HEAD_DIM
    qs = (jnp.where(first, q, jnp.zeros_like(q)), jnp.where(first, jnp.zeros_like(q), q))
    biases = (bias_ref[2 * p] * LOG2E, bias_ref[2 * p + 1] * LOG2E)
    tri = tri_ref[...]
    c_scr[...] = jnp.zeros_like(c_scr)
    a_scr[...] = jnp.zeros_like(a_scr)

    tk = tri_ref.shape[0]

    def tiles(js, valid, row0):
        kbs = [k_ref[0, pl.ds(pl.multiple_of(j * tk, tk), tk), :] for j in js]
        vbs = [v_ref[0, pl.ds(pl.multiple_of(j * tk, tk), tk), :] for j in js]
        carry, acc = _sb_tiles(qs, kbs, vbs, tri, biases, [c_scr[0], c_scr[1]], [a_scr[0], a_scr[1]], valid, row0)
        for h in range(2):
            c_scr[h] = carry[h]
            a_scr[h] = acc[h]

    row = lax.broadcasted_iota(jnp.int32, (tq, tk), 0)
    col = lax.broadcasted_iota(jnp.int32, (tq, tk), 1)
    ratio = tq // tk
    for d in range(ratio - 1, 0, -2):
        tiles([ratio * i + d, ratio * i + d - 1], [col + d * tk < row, col + (d - 1) * tk < row], [d * tk, (d - 1) * tk])

    def body(n, _):
        j = ratio * i - 1 - 2 * n
        tiles([j, j - 1], [None, None], [0, 0])
        return 0

    lax.fori_loop(0, (ratio // 2) * i, body, 0)
    o_ref[0] = jnp.where(first, a_scr[0], a_scr[1]).astype(o_ref.dtype)


def _sb_prompt(q, k, v, sb_bias, bsz, t):
    npair, m, _ = q.shape
    tk = SB_TILE
    tq = SB_QUERY_TILES * tk
    assert t % tq == 0 and SB_QUERY_TILES % 2 == 0 and m == bsz * t, (t, tq, m, bsz)
    nq = t // tq
    return pl.pallas_call(
        _sb_prompt_kernel,
        grid_spec=pltpu.PrefetchScalarGridSpec(
            num_scalar_prefetch=1,
            grid=(bsz, npair, nq),
            in_specs=[
                pl.BlockSpec((1, tq, LANES), lambda b, p, i, s: (p, b * nq + i, 0)),
                pl.BlockSpec((1, t, LANES), lambda b, p, i, s: (p, b, 0)),
                pl.BlockSpec((1, t, LANES), lambda b, p, i, s: (p, b, 0)),
                pl.BlockSpec((tk, tk), lambda b, p, i, s: (0, 0)),
            ],
            out_specs=pl.BlockSpec((1, tq, LANES), lambda b, p, i, s: (p, b * nq + i, 0)),
            scratch_shapes=[pltpu.VMEM((2, tq, LANES), F32), pltpu.VMEM((2, tq, LANES), F32)],
        ),
        out_shape=jax.ShapeDtypeStruct((npair, m, LANES), BF16),
        compiler_params=_params(("parallel", "parallel", "arbitrary")),
        name="sb_prompt",
    )(sb_bias.astype(F32), q, k, v, _sb_neg_tri(tk))


def _sb_decode_kernel(pt_ref, q_ref, bias_ref, *refs, group):
    k_refs, v_refs = refs[0:group], refs[group:2 * group]
    tri_ref, o_ref, c_scr, a_scr = refs[2 * group:]
    s = pl.program_id(0)
    g = pl.program_id(1)
    width = q_ref.shape[2]

    @pl.when(g == 0)
    def _():
        c_scr[...] = jnp.zeros_like(c_scr)
        a_scr[...] = jnp.zeros_like(a_scr)

    head = lax.broadcasted_iota(jnp.int32, (SB_HEADS, width), 0)
    lane = lax.broadcasted_iota(jnp.int32, (SB_HEADS, width), 1)
    own = (lane // SB_HEAD_DIM) == head
    qm = jnp.where(own, jnp.broadcast_to(q_ref[0], (SB_HEADS, width)), 0.0).astype(BF16)
    bias2 = jnp.concatenate([bias_ref[...] * LOG2E] * group, axis=0)
    z2 = jnp.concatenate([_dot(qm, k_refs[j][0].astype(BF16)) for j in range(group)], axis=0) + bias2
    sp2 = _softplus2(z2)
    hi = sp2.astype(BF16)
    lo = (sp2 - hi.astype(F32)).astype(BF16)
    tot = jnp.sum(sp2, axis=1, keepdims=True)
    carries = [c_scr[...]]
    for j in range(group):
        carries.append(carries[j] - tot[j * SB_HEADS:(j + 1) * SB_HEADS, :])
    c_scr[...] = carries[group]
    rem2 = _dot(hi, tri_ref[...]) + _dot(lo, tri_ref[...]) + jnp.concatenate(carries[0:group], axis=0)
    w = jnp.exp2((z2 - sp2) + rem2)
    acc = a_scr[...]
    for j in range(group):
        wj = w[j * SB_HEADS:(j + 1) * SB_HEADS, :].astype(BF16)
        acc = acc + _dot_nt(wj, v_refs[j][0].astype(BF16))
    a_scr[...] = acc

    @pl.when(g == pl.num_programs(1) - 1)
    def _():
        o = jnp.sum(jnp.where(own, acc, 0.0), axis=0, keepdims=True)
        for p in range(width // LANES):
            o_ref[p, pl.ds(s, 1), :] = o[:, p * LANES:(p + 1) * LANES]


def _sb_decode(q, cache_kt, cache_vt, page_table, sb_bias):
    n_seq, _, width = q.shape
    page = cache_kt.shape[2]
    npages = page_table.shape[1]
    group = SB_DECODE_GROUP if npages % SB_DECODE_GROUP == 0 else 1
    tri = _sb_neg_tri(page)

    def kv_spec(jj):
        return pl.BlockSpec((1, width, page), lambda b, g, pt: (pt[b, npages - 1 - (g * group + jj)], 0, 0))

    return pl.pallas_call(
        functools.partial(_sb_decode_kernel, group=group),
        grid_spec=pltpu.PrefetchScalarGridSpec(
            num_scalar_prefetch=1,
            grid=(n_seq, npages // group),
            in_specs=[
                pl.BlockSpec((1, 1, width), lambda b, g, pt: (b, 0, 0)),
                pl.BlockSpec((SB_HEADS, 1), lambda b, g, pt: (0, 0)),
                *[kv_spec(jj) for jj in range(group)],
                *[kv_spec(jj) for jj in range(group)],
                pl.BlockSpec((page, page), lambda b, g, pt: (0, 0)),
            ],
            out_specs=pl.BlockSpec((width // LANES, n_seq, LANES), lambda b, g, pt: (0, 0, 0)),
            scratch_shapes=[pltpu.VMEM((SB_HEADS, 1), F32), pltpu.VMEM((SB_HEADS, width), F32)],
        ),
        out_shape=jax.ShapeDtypeStruct((width // LANES, n_seq, LANES), F32),
        compiler_params=_params(("arbitrary", "arbitrary")),
        name="sb_decode",
    )(page_table, q, sb_bias.astype(F32).reshape(SB_HEADS, 1), *([cache_kt] * group), *([cache_vt] * group), tri)


def _ssd_chunk(xbc_ref, z_ref, dt_ref, dtt_ref, cw_ref, cb_ref, dtb_ref, dtbt_ref, alog_ref, alogt_ref,
               dskip_ref, nw_ref, y_ref, buf, st):
    L = xbc_ref.shape[0]
    pad = SUBLANES
    buf[pad:pad + L, :] = xbc_ref[...]
    conv = cb_ref[...]
    for j in range(SSD_CONV):
        off = pad - (SSD_CONV - 1) + j
        conv = conv + cw_ref[j:j + 1, :] * buf[off:off + L, :]
    tail = buf[pad + L - (SSD_CONV - 1):pad + L, :]
    buf[pad - (SSD_CONV - 1):pad, :] = tail
    xa = _silu(conv)
    xs = xa[:, 0:SSD_WIDTH]
    gw = SSD_STATE
    bm = [xa[:, SSD_WIDTH + g * gw:SSD_WIDTH + (g + 1) * gw].astype(BF16) for g in range(SSD_GROUPS)]
    cm = [xa[:, SSD_WIDTH + (SSD_GROUPS + g) * gw:SSD_WIDTH + (SSD_GROUPS + g + 1) * gw].astype(BF16)
          for g in range(SSD_GROUPS)]

    dt = _softplus(dt_ref[...] + dtb_ref[...])
    dtt = _softplus(dtt_ref[...] + dtbt_ref[...])
    a2 = -jnp.exp(alog_ref[...]) * LOG2E
    at2 = -jnp.exp(alogt_ref[...]) * LOG2E
    row = lax.broadcasted_iota(jnp.int32, (L, L), 0)
    col = lax.broadcasted_iota(jnp.int32, (L, L), 1)
    lower = col <= row
    lower_b = lower.astype(BF16)
    upper_b = (row <= col).astype(BF16)
    cum = sum(_dot(lower_b, part) for part in _split3(dt * a2))
    cumt = sum(_dot(part, upper_b) for part in _split3(dtt * at2))
    last = cum[L - 1:L, :]
    wst = jnp.exp2(last - cum) * dt
    ecum = jnp.exp2(cum)
    elast = jnp.exp2(last)

    lane = lax.broadcasted_iota(jnp.int32, (L, LANES), 1)
    first = lane < SSD_HEAD_DIM
    first_row = first[0:1, :]
    heads_per_group = SSD_HEADS // SSD_GROUPS
    cb = [jnp.where(lower, _dot_nt(cm[g], bm[g]), 0.0) for g in range(SSD_GROUPS)]
    ys = []
    for p in range(SSD_WIDTH // LANES):
        g = (2 * p) // heads_per_group
        xp = xs[:, p * LANES:(p + 1) * LANES]
        xpb = xp.astype(BF16)
        yi = []
        for h in (2 * p, 2 * p + 1):
            seg = cum[:, h:h + 1] - cumt[h:h + 1, :]
            dec = jnp.exp2(jnp.minimum(seg, 0.0)) * dtt[h:h + 1, :]
            yi.append(_dot((cb[g] * dec).astype(BF16), xpb))
        y_intra = jnp.where(first, yi[0], yi[1])
        stp = st[:, p * LANES:(p + 1) * LANES]
        ec = jnp.where(first, ecum[:, 2 * p:2 * p + 1], ecum[:, 2 * p + 1:2 * p + 2])
        y_inter = _dot(cm[g], stp.astype(BF16)) * ec
        wp = jnp.where(first, wst[:, 2 * p:2 * p + 1], wst[:, 2 * p + 1:2 * p + 2])
        el = jnp.where(first_row, elast[:, 2 * p:2 * p + 1], elast[:, 2 * p + 1:2 * p + 2])
        st[:, p * LANES:(p + 1) * LANES] = el * stp + _dot_tn(bm[g], (xp * wp).astype(BF16))
        ys.append(y_intra + y_inter + dskip_ref[:, p * LANES:(p + 1) * LANES] * xp)
    y = jnp.concatenate(ys, axis=1)
    y_ref[...] = _rmsnorm(y * _silu(z_ref[...].astype(F32)), nw_ref[...]).astype(y_ref.dtype)
    return tail


def _ssd_prompt_kernel(xbc_ref, z_ref, dt_ref, dtt_ref, *refs):
    params, (y_ref, conv_ref, ssm_ref, buf, st) = refs[:-5], refs[-5:]
    c = pl.program_id(1)

    @pl.when(c == 0)
    def _():
        buf[0:SUBLANES, :] = jnp.zeros((SUBLANES, SSD_CONV_CH), F32)
        st[...] = jnp.zeros_like(st)

    for s in range(xbc_ref.shape[0] // SSD_CHUNK):
        rows = pl.ds(s * SSD_CHUNK, SSD_CHUNK)
        tail = _ssd_chunk(xbc_ref.at[rows], z_ref.at[rows], dt_ref.at[rows], dtt_ref.at[:, rows], *params,
                          y_ref.at[rows], buf, st)

    @pl.when(c == pl.num_programs(1) - 1)
    def _():
        conv_ref[0] = tail
        ssm_ref[0] = st[...].T


def _ssd_prompt(xbc, z, dt, dtt, conv_w, conv_b, dt_bias, a_log, d_skip, ssd_norm, bsz, t):
    m = xbc.shape[0]
    L = SSD_CHUNK * SSD_STEP_CHUNKS
    assert t % L == 0 and m == bsz * t, (t, L, m, bsz)
    nc = t // L
    rows = lambda b, c: (b * nc + c, 0)
    full = lambda b, c: (0, 0)
    vec = lambda v: v.astype(F32).reshape(1, -1)
    colv = lambda v: v.astype(F32).reshape(-1, 1)
    y, conv, ssm = pl.pallas_call(
        _ssd_prompt_kernel,
        grid=(bsz, nc),
        in_specs=[
            pl.BlockSpec((L, SSD_CONV_CH), rows),
            pl.BlockSpec((L, SSD_WIDTH), rows),
            pl.BlockSpec((L, SSD_HEADS), rows),
            pl.BlockSpec((SSD_HEADS, L), lambda b, c: (0, b * nc + c)),
            pl.BlockSpec((SSD_CONV, SSD_CONV_CH), full),
            pl.BlockSpec((1, SSD_CONV_CH), full),
            pl.BlockSpec((1, SSD_HEADS), full),
            pl.BlockSpec((SSD_HEADS, 1), full),
            pl.BlockSpec((1, SSD_HEADS), full),
            pl.BlockSpec((SSD_HEADS, 1), full),
            pl.BlockSpec((1, SSD_WIDTH), full),
            pl.BlockSpec((1, SSD_WIDTH), full),
        ],
        out_specs=[
            pl.BlockSpec((L, SSD_WIDTH), rows),
            pl.BlockSpec((1, SSD_CONV - 1, SSD_CONV_CH), lambda b, c: (b, 0, 0)),
            pl.BlockSpec((1, SSD_WIDTH, SSD_STATE), lambda b, c: (b, 0, 0)),
        ],
        out_shape=[
            jax.ShapeDtypeStruct((m, SSD_WIDTH), BF16),
            jax.ShapeDtypeStruct((bsz, SSD_CONV - 1, SSD_CONV_CH), F32),
            jax.ShapeDtypeStruct((bsz, SSD_WIDTH, SSD_STATE), F32),
        ],
        scratch_shapes=[pltpu.VMEM((SSD_CHUNK + SUBLANES, SSD_CONV_CH), F32),
                        pltpu.VMEM((SSD_STATE, SSD_WIDTH), F32)],
        compiler_params=_params(("parallel", "arbitrary")),
        name="ssd_prompt",
    )(xbc, z, dt, dtt, conv_w.astype(F32), vec(conv_b), vec(dt_bias), colv(dt_bias), vec(a_log), colv(a_log),
      vec(jnp.repeat(d_skip, SSD_HEAD_DIM)), vec(ssd_norm))
    return y, conv, ssm.reshape(bsz, SSD_HEADS, SSD_HEAD_DIM, SSD_STATE)


def _ssd_step_kernel(xbc_ref, z_ref, dt_ref, sconv_ref, sssm_ref, cw_ref, cb_ref, dtb_ref, alog_ref, dskip_ref,
                     nw_ref, y_ref, conv_ref, ssm_ref):
    lane_head = lax.broadcasted_iota(jnp.int32, (1, SSD_WIDTH), 1) // SSD_HEAD_DIM
    rows_per_group = SSD_WIDTH // SSD_GROUPS
    row = lax.broadcasted_iota(jnp.int32, (SSD_WIDTH, SSD_STATE), 0)
    a = -jnp.exp(alog_ref[...])
    for s in range(xbc_ref.shape[0]):
        xr = xbc_ref[s]
        cs = sconv_ref[s]
        conv = cb_ref[...] + cw_ref[SSD_CONV - 1:SSD_CONV, :] * xr
        for j in range(SSD_CONV - 1):
            conv = conv + cw_ref[j:j + 1, :] * cs[j:j + 1, :]
        conv_ref[s] = jnp.concatenate([cs[1:SSD_CONV - 1, :], xr], axis=0)
        xa = _silu(conv)
        xs = xa[:, 0:SSD_WIDTH]
        dt = _softplus(dt_ref[s] + dtb_ref[...])
        da = jnp.exp(dt * a)
        dt_w = jnp.zeros((1, SSD_WIDTH), F32)
        da_w = jnp.zeros((1, SSD_WIDTH), F32)
        for h in range(SSD_HEADS):
            dt_w = jnp.where(lane_head == h, dt[:, h:h + 1], dt_w)
            da_w = jnp.where(lane_head == h, da[:, h:h + 1], da_w)
        dtx_col = _row_to_col(dt_w * xs)
        da_col = _row_to_col(da_w)
        b_rows = jnp.zeros((SSD_WIDTH, SSD_STATE), F32)
        c_rows = jnp.zeros((SSD_WIDTH, SSD_STATE), F32)
        for g in range(SSD_GROUPS):
            sel = (row // rows_per_group) == g
            b0 = SSD_WIDTH + g * SSD_STATE
            c0 = SSD_WIDTH + (SSD_GROUPS + g) * SSD_STATE
            b_rows = jnp.where(sel, xa[:, b0:b0 + SSD_STATE], b_rows)
            c_rows = jnp.where(sel, xa[:, c0:c0 + SSD_STATE], c_rows)
        new = da_col * sssm_ref[s] + dtx_col * b_rows
        ssm_ref[s] = new
        y = _col_to_row(jnp.sum(new * c_rows, axis=1, keepdims=True)) + dskip_ref[...] * xs
        y_ref[s] = _rmsnorm(y * _silu(z_ref[s].astype(F32)), nw_ref[...])


def _ssd_step(xbc, z, dt, state_conv, state_ssm, conv_w, conv_b, dt_bias, a_log, d_skip, ssd_norm):
    n = xbc.shape[0]
    ns = STEP_SEQS if n % STEP_SEQS == 0 else 1
    vec = lambda v: v.astype(F32).reshape(1, -1)
    per = lambda i: (i, 0, 0)
    full = lambda i: (0, 0)
    y, conv, ssm = pl.pallas_call(
        _ssd_step_kernel,
        grid=(n // ns,),
        in_specs=[
            pl.BlockSpec((ns, 1, SSD_CONV_CH), per),
            pl.BlockSpec((ns, 1, SSD_WIDTH), per),
            pl.BlockSpec((ns, 1, SSD_HEADS), per),
            pl.BlockSpec((ns, SSD_CONV - 1, SSD_CONV_CH), per),
            pl.BlockSpec((ns, SSD_WIDTH, SSD_STATE), per),
            pl.BlockSpec((SSD_CONV, SSD_CONV_CH), full),
            pl.BlockSpec((1, SSD_CONV_CH), full),
            pl.BlockSpec((1, SSD_HEADS), full),
            pl.BlockSpec((1, SSD_HEADS), full),
            pl.BlockSpec((1, SSD_WIDTH), full),
            pl.BlockSpec((1, SSD_WIDTH), full),
        ],
        out_specs=[
            pl.BlockSpec((ns, 1, SSD_WIDTH), per),
            pl.BlockSpec((ns, SSD_CONV - 1, SSD_CONV_CH), per),
            pl.BlockSpec((ns, SSD_WIDTH, SSD_STATE), per),
        ],
        out_shape=[
            jax.ShapeDtypeStruct((n, 1, SSD_WIDTH), F32),
            jax.ShapeDtypeStruct((n, SSD_CONV - 1, SSD_CONV_CH), F32),
            jax.ShapeDtypeStruct((n, SSD_WIDTH, SSD_STATE), F32),
        ],
        compiler_params=_params(("parallel",)),
        name="ssd_step",
    )(xbc.reshape(n, 1, SSD_CONV_CH), z.reshape(n, 1, SSD_WIDTH), dt.reshape(n, 1, SSD_HEADS),
      state_conv.astype(F32), state_ssm.astype(F32).reshape(n, SSD_WIDTH, SSD_STATE), conv_w.astype(F32),
      vec(conv_b), vec(dt_bias), vec(a_log), vec(jnp.repeat(d_skip, SSD_HEAD_DIM)), vec(ssd_norm))
    return y.reshape(n, SSD_WIDTH), conv, ssm.reshape(n, SSD_HEADS, SSD_HEAD_DIM, SSD_STATE)


def _even_mix(osb_ref, y_ref):
    return jnp.concatenate([osb_ref[p] for p in range(osb_ref.shape[0])] + [y_ref[...]], axis=1)


def _odd_mix(o_ref, g_ref, nw_ref):
    o = jnp.concatenate([o_ref[h] for h in range(o_ref.shape[0])], axis=1).astype(F32)
    return (_rmsnorm(o, nw_ref[...]) * _silu(g_ref[...].astype(F32))).astype(BF16)


def _mix_mlp_kernel(*refs, mix_fn, n_mix, final_norm):
    mix_refs = refs[0:n_mix]
    x_ref, wout_ref, nw_ref, w1_ref, w2_ref, fw_ref, o_ref, hn_scr, acc_scr = refs[n_mix:]
    j = pl.program_id(1)

    @pl.when(j == 0)
    def _():
        h1 = x_ref[...] + _dot(mix_fn(*mix_refs), wout_ref[...])
        o_ref[...] = h1
        hn_scr[...] = _rmsnorm(h1, nw_ref[...]).astype(BF16)

    h = jnp.maximum(_dot(hn_scr[...], w1_ref[...]), 0.0)
    a = (h * h).astype(BF16)

    @pl.when(j == 0)
    def _():
        acc_scr[...] = _dot(a, w2_ref[...])

    @pl.when(j > 0)
    def _():
        acc_scr[...] += _dot(a, w2_ref[...])

    @pl.when(j == pl.num_programs(1) - 1)
    def _():
        out = o_ref[...] + acc_scr[...]
        o_ref[...] = _rmsnorm(out, fw_ref[...]) if final_norm else out


def _mix_mlp(mix_fn, mix_args, mix_specs, x, w_out, norm_w, w1_all, w2_all, layer, final_w, tm, tf):
    m, d = x.shape
    f = w1_all.shape[2]
    assert m % tm == 0 and f % tf == 0, (m, tm, f, tf)
    final_norm = final_w is not None
    fw = (final_w if final_norm else jnp.ones((d,), F32)).astype(F32).reshape(1, d)
    const = lambda i, j: (0, 0)
    return pl.pallas_call(
        functools.partial(_mix_mlp_kernel, mix_fn=mix_fn, n_mix=len(mix_args), final_norm=final_norm),
        grid=(m // tm, f // tf),
        in_specs=[
            *mix_specs,
            pl.BlockSpec((tm, d), lambda i, j: (i, 0)),
            pl.BlockSpec(w_out.shape, const),
            pl.BlockSpec((1, d), const),
            pl.BlockSpec((None, d, tf), lambda i, j: (layer, 0, j)),
            pl.BlockSpec((None, tf, d), lambda i, j: (layer, j, 0)),
            pl.BlockSpec((1, d), const),
        ],
        out_specs=pl.BlockSpec((tm, d), lambda i, j: (i, 0)),
        out_shape=jax.ShapeDtypeStruct((m, d), F32),
        scratch_shapes=[pltpu.VMEM((tm, d), BF16), pltpu.VMEM((tm, d), F32)],
        compiler_params=_params(("parallel", "arbitrary")),
        name="mix_mlp",
    )(*mix_args, x, w_out.astype(BF16), norm_w.astype(F32).reshape(1, d), w1_all, w2_all, fw)


def _even_mix_mlp(osb, y, x, w_out, norm_w, w1_all, w2_all, layer, final_w, tm, tf):
    specs = [pl.BlockSpec((osb.shape[0], tm, LANES), lambda i, j: (0, i, 0)),
             pl.BlockSpec((tm, SSD_WIDTH), lambda i, j: (i, 0))]
    return _mix_mlp(_even_mix, (osb, y), specs, x, w_out, norm_w, w1_all, w2_all, layer, final_w, tm, tf)


def _odd_mix_mlp(o, g, hg_norm, x, w_out, norm_w, w1_all, w2_all, layer, final_w, tm, tf):
    nh = o.shape[0]
    specs = [pl.BlockSpec((nh, tm, HG_VAL), lambda i, j: (0, i, 0)),
             pl.BlockSpec((tm, nh * HG_VAL), lambda i, j: (i, 0)),
             pl.BlockSpec((1, nh * HG_VAL), lambda i, j: (0, 0))]
    args = (o, g, hg_norm.astype(F32).reshape(1, -1))
    return _mix_mlp(_odd_mix, args, specs, x, w_out, norm_w, w1_all, w2_all, layer, final_w, tm, tf)


def _odd_proj_kernel(x_ref, nw_ref, w_ref, q_ref, f_ref, v_ref, g_ref):
    hn = _rmsnorm(x_ref[...], nw_ref[...]).astype(BF16)
    width = HG_HEADS * HG_KEY
    per_chunk = PROJ_COLS // HG_KEY
    for i, ref in enumerate((q_ref, f_ref, v_ref)):
        for c in range(width // PROJ_COLS):
            y = _dot(hn, w_ref[:, i * width + c * PROJ_COLS:i * width + (c + 1) * PROJ_COLS])
            for h in range(per_chunk):
                ref[c * per_chunk + h] = y[:, h * HG_KEY:(h + 1) * HG_KEY]
    for c in range(width // PROJ_COLS):
        g_ref[:, c * PROJ_COLS:(c + 1) * PROJ_COLS] = _dot(
            hn, w_ref[:, 3 * width + c * PROJ_COLS:3 * width + (c + 1) * PROJ_COLS]).astype(g_ref.dtype)


def _odd_proj(x, norm_w, w_in, tm):
    m, d = x.shape
    width = HG_HEADS * HG_KEY
    heads = lambda i: (0, i, 0)
    per_head = jax.ShapeDtypeStruct((HG_HEADS, m, HG_KEY), F32)
    return pl.pallas_call(
        _odd_proj_kernel,
        grid=(m // tm,),
        in_specs=[
            pl.BlockSpec((tm, d), lambda i: (i, 0)),
            pl.BlockSpec((1, d), lambda i: (0, 0)),
            pl.BlockSpec(w_in.shape, lambda i: (0, 0)),
        ],
        out_specs=[
            pl.BlockSpec((HG_HEADS, tm, HG_KEY), heads),
            pl.BlockSpec((HG_HEADS, tm, HG_KEY), heads),
            pl.BlockSpec((HG_HEADS, tm, HG_KEY), heads),
            pl.BlockSpec((tm, width), lambda i: (i, 0)),
        ],
        out_shape=[per_head, per_head, per_head, jax.ShapeDtypeStruct((m, width), BF16)],
        compiler_params=_params(("parallel",)),
        name="odd_proj",
    )(x, norm_w.astype(F32).reshape(1, d), w_in.astype(BF16))


def _hgrn_lower_bound(raw, layer):
    e = jnp.exp(raw - jnp.max(raw, axis=0, keepdims=True))
    p = e / jnp.sum(e, axis=0, keepdims=True)
    lb = jnp.zeros_like(p[0])
    for l in range(1, layer + 1):
        lb = lb + p[l]
    return lb


def _hgrn_gates(fpre, lb):
    sig, nsig = _sigmoid_pair(fpre)
    return lb + (1.0 - lb) * sig, (1.0 - lb) * nsig


def _hgrn_chunk(q_ref, f_ref, v_ref, lb_ref, o_ref, st, b_scr, k_scr, od_scr, layer):
    L = q_ref.shape[1]
    nblk = L // HG_DIAG
    row = lax.broadcasted_iota(jnp.int32, (L, L), 0)
    col = lax.broadcasted_iota(jnp.int32, (L, L), 1)
    lower_b = (col <= row).astype(BF16)
    pos = lax.broadcasted_iota(jnp.int32, (L, HG_KEY), 0)

    def heads(n, _):
        hs = [n * HG_HEAD_GROUP + e for e in range(HG_HEAD_GROUP)]
        every = range(HG_HEAD_GROUP)
        q = [q_ref[h] for h in hs]
        vb = [v_ref[h].astype(BF16) for h in hs]
        gates = [_hgrn_gates(f_ref[h], _hgrn_lower_bound(lb_ref[:, h], layer)) for h in hs]
        kin = [gk[1] for gk in gates]
        b = [sum(_dot(lower_b, part) for part in _split3(jnp.log(gk[0]) * LOG2E)) for gk in gates]
        for e in every:
            b_scr[e] = b[e]
            k_scr[e] = kin[e]
        stt = [st[h] for h in hs]
        o = [_dot_nt((q[e] * jnp.exp2(b[e])).astype(BF16), stt[e].astype(BF16)) for e in every]
        b_last = [x[L - 1:L, :] for x in b]
        for e in every:
            st[hs[e]] = stt[e] * jnp.exp2(b_last[e]) + _dot_tn(
                vb[e], (kin[e] * jnp.exp2(b_last[e] - b[e])).astype(BF16))

        att = [jnp.zeros((L, L), F32) for _ in every]
        g = L
        while g > HG_DIAG:
            half = g // 2
            late = (pos & (g - 1)) >= half
            pair = ((row ^ col) < g) & ((row & (g - 1)) >= half) & ((col & (g - 1)) < half)
            mid = [jnp.broadcast_to(x.reshape(L // g, g, HG_KEY)[:, half - 1:half, :],
                                    (L // g, g, HG_KEY)).reshape(L, HG_KEY) for x in b]
            r = [(jnp.where(late, q[e], kin[e]) * jnp.exp2(-jnp.abs(b[e] - mid[e]))).astype(BF16) for e in every]
            att = [jnp.where(pair, _dot_nt(r[e], r[e]), att[e]) for e in every]
            g = half
        o = [o[e] + _dot(att[e].astype(BF16), vb[e]) for e in every]

        at = lambda ref, i: ref[pl.ds(i, nblk, stride=HG_DIAG), :]
        for e in every:
            bs = [at(b_scr.at[e], i) for i in range(HG_DIAG)]
            ks = [at(k_scr.at[e], i) for i in range(HG_DIAG)]
            vs = [at(v_ref.at[hs[e]], i) for i in range(HG_DIAG)]
            for i in range(HG_DIAG):
                qi = at(q_ref.at[hs[e]], i)
                acc = jnp.zeros((nblk, HG_VAL), F32)
                for j in range(i + 1):
                    a = jnp.sum(qi * ks[j] * jnp.exp2(bs[i] - bs[j]), axis=1, keepdims=True)
                    acc = acc + a * vs[j]
                od_scr[e, pl.ds(i, nblk, stride=HG_DIAG), :] = acc
        for e in every:
            o_ref[hs[e]] = (o[e] + od_scr[e]).astype(o_ref.dtype)
        return 0

    groups = q_ref.shape[0] // HG_HEAD_GROUP
    if groups == 1:
        heads(0, 0)
    else:
        lax.fori_loop(0, groups, heads, 0)


def _hgrn_prompt_kernel(q_ref, f_ref, v_ref, lb_ref, o_ref, s_ref, st, b_scr, k_scr, od_scr, *, layer):
    c = pl.program_id(1)

    @pl.when(c == 0)
    def _():
        st[...] = jnp.zeros_like(st)

    for s in range(q_ref.shape[1] // HG_CHUNK):
        rows = pl.ds(s * HG_CHUNK, HG_CHUNK)
        _hgrn_chunk(q_ref.at[:, rows], f_ref.at[:, rows], v_ref.at[:, rows], lb_ref, o_ref.at[:, rows], st,
                    b_scr.at[s], k_scr.at[s], od_scr.at[s], layer)

    @pl.when(c == pl.num_programs(1) - 1)
    def _():
        for h in range(s_ref.shape[1]):
            s_ref[0, h] = st[h].T


def _hgrn_prompt(q, f, v, lb_raw, layer, bsz, t):
    nh, m, _ = q.shape
    L = HG_CHUNK * HG_STEP_CHUNKS
    assert t % L == 0 and m == bsz * t and nh % HG_HEAD_GROUP == 0, (t, L, m, bsz, nh)
    nc = t // L
    blk = lambda b, c: (0, b * nc + c, 0)
    depth = lb_raw.shape[0]
    return pl.pallas_call(
        functools.partial(_hgrn_prompt_kernel, layer=layer),
        grid=(bsz, nc),
        in_specs=[
            pl.BlockSpec((nh, L, HG_KEY), blk),
            pl.BlockSpec((nh, L, HG_KEY), blk),
            pl.BlockSpec((nh, L, HG_VAL), blk),
            pl.BlockSpec((depth, nh, 1, HG_KEY), lambda b, c: (0, 0, 0, 0)),
        ],
        out_specs=[
            pl.BlockSpec((nh, L, HG_VAL), blk),
            pl.BlockSpec((1, nh, HG_KEY, HG_VAL), lambda b, c: (b, 0, 0, 0)),
        ],
        out_shape=[
            jax.ShapeDtypeStruct((nh, m, HG_VAL), BF16),
            jax.ShapeDtypeStruct((bsz, nh, HG_KEY, HG_VAL), F32),
        ],
        scratch_shapes=[
            pltpu.VMEM((nh, HG_VAL, HG_KEY), F32),
            pltpu.VMEM((HG_STEP_CHUNKS, HG_HEAD_GROUP, HG_CHUNK, HG_KEY), F32),
            pltpu.VMEM((HG_STEP_CHUNKS, HG_HEAD_GROUP, HG_CHUNK, HG_KEY), F32),
            pltpu.VMEM((HG_STEP_CHUNKS, HG_HEAD_GROUP, HG_CHUNK, HG_VAL), F32),
        ],
        compiler_params=_params(("parallel", "arbitrary")),
        name="hgrn_prompt",
    )(q, f, v, lb_raw.astype(F32).reshape(depth, nh, 1, HG_KEY))


def _hgrn_step_kernel(q_ref, f_ref, v_ref, lb_ref, s_ref, o_ref, snew_ref, *, layer):
    ns = s_ref.shape[0]
    for h in range(q_ref.shape[0]):
        lb = _hgrn_lower_bound(lb_ref[:, h], layer)
        for s in range(ns):
            i = pl.program_id(0) * ns + s
            q = q_ref[h, pl.ds(i, 1), :]
            v = v_ref[h, pl.ds(i, 1), :]
            fg, kin = _hgrn_gates(f_ref[h, pl.ds(i, 1), :], lb)
            new = _row_to_col(fg) * s_ref[s, h] + _row_to_col(kin) * v
            snew_ref[s, h] = new
            o_ref[h, pl.ds(i, 1), :] = jnp.sum(_row_to_col(q) * new, axis=0, keepdims=True)


def _hgrn_step(q, f, v, lb_raw, state, layer):
    nh, n, _ = q.shape
    ns = STEP_SEQS if n % STEP_SEQS == 0 else 1
    depth = lb_raw.shape[0]
    whole = lambda i: (0, 0, 0)
    return pl.pallas_call(
        functools.partial(_hgrn_step_kernel, layer=layer),
        grid=(n // ns,),
        in_specs=[
            pl.BlockSpec((nh, n, HG_KEY), whole),
            pl.BlockSpec((nh, n, HG_KEY), whole),
            pl.BlockSpec((nh, n, HG_VAL), whole),
            pl.BlockSpec((depth, nh, 1, HG_KEY), lambda i: (0, 0, 0, 0)),
            pl.BlockSpec((ns, nh, HG_KEY, HG_VAL), lambda i: (i, 0, 0, 0)),
        ],
        out_specs=[
            pl.BlockSpec((nh, n, HG_VAL), whole),
            pl.BlockSpec((ns, nh, HG_KEY, HG_VAL), lambda i: (i, 0, 0, 0)),
        ],
        out_shape=[
            jax.ShapeDtypeStruct((nh, n, HG_VAL), F32),
            jax.ShapeDtypeStruct((n, nh, HG_KEY, HG_VAL), F32),
        ],
        compiler_params=_params(("arbitrary",)),
        name="hgrn_step",
    )(q, f, v, lb_raw.astype(F32).reshape(depth, nh, 1, HG_KEY), state.astype(F32))


def _row_tile(m, want):
    return want if m % want == 0 else m


def kernel(x_prompt, x_sample, cache_k, cache_v, page_table, state_conv, state_ssm, state_hgrn, norm_mix, norm_ffn,
           norm_final, w_in_even, sb_bias, conv_w, conv_b, dt_bias, a_log, d_skip, ssd_norm, w_out_even, w_in_odd,
           hg_lb_raw, hg_norm, w_out_odd, w_ff1, w_ff2):
    bsz, t, d = x_prompt.shape
    n_seq = x_sample.shape[0]
    depth = norm_mix.shape[0]
    mp = bsz * t
    hp = x_prompt.reshape(mp, d)
    hs = x_sample.reshape(n_seq, d)
    tmp = _row_tile(mp, 512)
    tmm = _row_tile(mp, 1024)
    tf = 1024
    outs = {k: [] for k in ("kp", "vp", "ks", "vs", "cp", "cs", "sp", "ss", "gp", "gs")}
    w1_all = w_ff1.astype(BF16)
    w2_all = w_ff2.astype(BF16)

    for layer in range(depth):
        li = layer // 2
        fw = norm_final if layer == depth - 1 else None
        mlp_w = (norm_ffn[layer], w1_all, w2_all, layer, fw)
        if layer % 2 == 0:
            ssd_w = (conv_w[li], conv_b[li], dt_bias[li], a_log[li], d_skip[li], ssd_norm[li])
            q, k, v, kf, vf, z, xbc, dt, dtt = _even_proj(hp, norm_mix[layer], w_in_even[li], tmp, seq_len=t)
            osb = _sb_prompt(q, k, v, sb_bias[li], bsz, t)
            y, cp, sp = _ssd_prompt(xbc, z, dt, dtt, *ssd_w, bsz, t)
            hp = _even_mix_mlp(osb, y, hp, w_out_even[li], *mlp_w, tmm, tf)
            rows_view = lambda a: jnp.transpose(a.reshape(bsz, SB_HEADS, SB_HEAD_DIM, t), (0, 3, 1, 2))
            outs["kp"].append(rows_view(kf))
            outs["vp"].append(rows_view(vf))
            outs["cp"].append(cp)
            outs["sp"].append(sp)
            q, k, v, kf, vf, z, xbc, dt, dtt = _even_proj(hs, norm_mix[layer], w_in_even[li], n_seq)
            qs = q.astype(F32).transpose(1, 0, 2).reshape(n_seq, 1, SB_WIDTH)
            n_phys, page = cache_k.shape[1], cache_k.shape[2]
            to_lanes = lambda c: jnp.transpose(c, (0, 2, 3, 1)).reshape(n_phys, SB_WIDTH, page)
            osb = _sb_decode(qs, to_lanes(cache_k[li]), to_lanes(cache_v[li]), page_table, sb_bias[li])
            y, cs, ss = _ssd_step(xbc, z, dt, state_conv[li], state_ssm[li], *ssd_w)
            hs = _even_mix_mlp(osb.astype(BF16), y.astype(BF16), hs, w_out_even[li], *mlp_w, n_seq, tf)
            outs["ks"].append(kf.reshape(n_seq, 1, SB_HEADS, SB_HEAD_DIM))
            outs["vs"].append(vf.reshape(n_seq, 1, SB_HEADS, SB_HEAD_DIM))
            outs["cs"].append(cs)
            outs["ss"].append(ss)
        else:
            q, f, v, g = _odd_proj(hp, norm_mix[layer], w_in_odd[li], tmp)
            o, gp = _hgrn_prompt(q, f, v, hg_lb_raw, layer, bsz, t)
            hp = _odd_mix_mlp(o, g, hg_norm[li], hp, w_out_odd[li], *mlp_w, tmm, tf)
            outs["gp"].append(gp)
            q, f, v, g = _odd_proj(hs, norm_mix[layer], w_in_odd[li], n_seq)
            o, gs = _hgrn_step(q, f, v, hg_lb_raw, state_hgrn[li], layer)
            hs = _odd_mix_mlp(o.astype(BF16), g, hg_norm[li], hs, w_out_odd[li], *mlp_w, n_seq, tf)
            outs["gs"].append(gs)

    y_prompt = hp.reshape(bsz, t, d)
    y_sample = hs.reshape(n_seq, 1, d)
    st = lambda key: jnp.stack(outs[key])
    return (y_prompt, y_sample, st("kp"), st("vp"), st("ks"), st("vs"), st("cp"), st("cs"), st("sp"), st("ss"),
            st("gp"), st("gs"))
```

```python
import functools

import jax
import jax.numpy as jnp
from jax import lax
from jax.experimental import pallas as pl
from jax.experimental.pallas import tpu as pltpu

F32 = jnp.float32
BF16 = jnp.bfloat16

EPS = 1e-6
LOG2E = 1.4426950408889634
SB_HEADS = 8
SB_HEAD_DIM = 64
SB_WIDTH = SB_HEADS * SB_HEAD_DIM
SSD_HEADS = 8
SSD_HEAD_DIM = 64
SSD_WIDTH = SSD_HEADS * SSD_HEAD_DIM
SSD_STATE = 128
SSD_GROUPS = 2
SSD_CONV = 4
SSD_CONV_CH = SSD_WIDTH + 2 * SSD_GROUPS * SSD_STATE
HG_HEADS = 8
HG_KEY = 128
HG_VAL = 128

LANES = 128
SUBLANES = 8
VMEM_LIMIT_BYTES = 52 * 1024 * 1024

SB_TILE = 256
SB_QUERY_TILES = 8
SB_DECODE_GROUP = 32
SSD_CHUNK = 128
SSD_STEP_CHUNKS = 2
HG_STEP_CHUNKS = 4
HG_CHUNK = 128
HG_DIAG = 4
HG_HEAD_GROUP = 8
STEP_SEQS = 4
PROJ_COLS = 512


def _params(sem):
    return pltpu.CompilerParams(dimension_semantics=sem, vmem_limit_bytes=VMEM_LIMIT_BYTES)


def _dot(a, b):
    return jnp.dot(a, b, preferred_element_type=F32)


def _dot_nt(a, b):
    return lax.dot_general(a, b, (((1,), (1,)), ((), ())), preferred_element_type=F32)


def _dot_tn(a, b):
    return lax.dot_general(a, b, (((0,), (0,)), ((), ())), preferred_element_type=F32)


def _split3(x):
    hi = x.astype(BF16)
    r = x - hi.astype(F32)
    mid = r.astype(BF16)
    lo = (r - mid.astype(F32)).astype(BF16)
    return hi, mid, lo


def _rmsnorm(x, w):
    return x * lax.rsqrt(jnp.mean(x * x, axis=-1, keepdims=True) + EPS) * w


def _softplus(x):
    return jnp.maximum(x, 0.0) + jnp.log1p(jnp.exp(-jnp.abs(x)))


def _sigmoid_pair(x):
    e = jnp.exp(-jnp.abs(x))
    r = 1.0 / (1.0 + e)
    big, small = r, e * r
    pos = x >= 0
    return jnp.where(pos, big, small), jnp.where(pos, small, big)


def _silu(x):
    return x / (1.0 + jnp.exp(-x))


def _row_to_col(row):
    n = row.shape[1]
    eye = lax.broadcasted_iota(jnp.int32, (n, n), 0) == lax.broadcasted_iota(jnp.int32, (n, n), 1)
    return jnp.sum(jnp.where(eye, jnp.broadcast_to(row, (n, n)), 0.0), axis=1, keepdims=True)


def _col_to_row(col):
    n = col.shape[0]
    eye = lax.broadcasted_iota(jnp.int32, (n, n), 0) == lax.broadcasted_iota(jnp.int32, (n, n), 1)
    return jnp.sum(jnp.where(eye, jnp.broadcast_to(col, (n, n)), 0.0), axis=0, keepdims=True)


def _even_proj_kernel(x_ref, nw_ref, w_ref, wdt_ref, wdtt_ref,
                      q_ref, k_ref, v_ref, kf_ref, vf_ref, z_ref, xbc_ref, dt_ref, dtt_ref, *, token_minor):
    hn = _rmsnorm(x_ref[...], nw_ref[...]).astype(BF16)
    scale = LOG2E * SB_HEAD_DIM ** -0.5
    npair = SB_WIDTH // LANES
    q = _dot(hn, w_ref[:, 0:SB_WIDTH]) * scale
    for p in range(npair):
        q_ref[p] = q[:, p * LANES:(p + 1) * LANES].astype(BF16)
    for i, (pair_ref, full_ref) in enumerate(((k_ref, kf_ref), (v_ref, vf_ref))):
        y = _dot(hn, w_ref[:, (i + 1) * SB_WIDTH:(i + 2) * SB_WIDTH])
        if token_minor:
            full_ref[0] = y.T
        else:
            full_ref[...] = y
        for p in range(npair):
            pair_ref[p] = y[:, p * LANES:(p + 1) * LANES].astype(BF16)
    z0 = 3 * SB_WIDTH
    z_ref[...] = _dot(hn, w_ref[:, z0:z0 + SSD_WIDTH]).astype(z_ref.dtype)
    x0 = z0 + SSD_WIDTH
    for c in range(SSD_CONV_CH // PROJ_COLS):
        xbc_ref[:, c * PROJ_COLS:(c + 1) * PROJ_COLS] = _dot(
            hn, w_ref[:, x0 + c * PROJ_COLS:x0 + (c + 1) * PROJ_COLS])
    dtp = _dot(hn, wdt_ref[...])
    dt_ref[...] = dtp[:, 0:SSD_HEADS]
    if token_minor:
        dtt_ref[...] = dtp.T[0:SSD_HEADS, :]
    else:
        dtt_ref[...] = _dot_nt(wdtt_ref[...], hn)


def _even_proj(x, norm_w, w_in, tm, seq_len=None):
    m, d = x.shape
    npair = SB_WIDTH // LANES
    wb = w_in.astype(BF16)
    n_main = 3 * SB_WIDTH + SSD_WIDTH + SSD_CONV_CH
    w_main = wb
    w_dt = jnp.pad(wb[:, n_main:], ((0, 0), (0, LANES - SSD_HEADS)))
    w_dtt = wb[:, n_main:].T
    full = lambda i: (0, 0)
    rows = lambda i: (i, 0)
    hp = lambda i: (0, i, 0)
    token_minor = seq_len is not None
    assert m % tm == 0 and d % LANES == 0, (m, tm, d)
    if token_minor:
        assert seq_len % tm == 0 and m % seq_len == 0, (m, seq_len, tm)
        nt = seq_len // tm
        kv_spec = pl.BlockSpec((1, SB_WIDTH, tm), lambda i: (i // nt, 0, i % nt))
        kv_shape = jax.ShapeDtypeStruct((m // seq_len, SB_WIDTH, seq_len), F32)
    else:
        kv_spec = pl.BlockSpec((tm, SB_WIDTH), rows)
        kv_shape = jax.ShapeDtypeStruct((m, SB_WIDTH), F32)
    return pl.pallas_call(
        functools.partial(_even_proj_kernel, token_minor=token_minor),
        grid=(m // tm,),
        in_specs=[
            pl.BlockSpec((tm, d), rows),
            pl.BlockSpec((1, d), full),
            pl.BlockSpec(wb.shape, full),
            pl.BlockSpec((d, LANES), full),
            pl.BlockSpec((SSD_HEADS, d), full),
        ],
        out_specs=[
            pl.BlockSpec((npair, tm, LANES), hp),
            pl.BlockSpec((npair, tm, LANES), hp),
            pl.BlockSpec((npair, tm, LANES), hp),
            kv_spec,
            kv_spec,
            pl.BlockSpec((tm, SSD_WIDTH), rows),
            pl.BlockSpec((tm, SSD_CONV_CH), rows),
            pl.BlockSpec((tm, SSD_HEADS), rows),
            pl.BlockSpec((SSD_HEADS, tm), lambda i: (0, i)),
        ],
        out_shape=[
            jax.ShapeDtypeStruct((npair, m, LANES), BF16),
            jax.ShapeDtypeStruct((npair, m, LANES), BF16),
            jax.ShapeDtypeStruct((npair, m, LANES), BF16),
            kv_shape,
            kv_shape,
            jax.ShapeDtypeStruct((m, SSD_WIDTH), BF16),
            jax.ShapeDtypeStruct((m, SSD_CONV_CH), F32),
            jax.ShapeDtypeStruct((m, SSD_HEADS), F32),
            jax.ShapeDtypeStruct((SSD_HEADS, m), F32),
        ],
        compiler_params=_params(("parallel",)),
        name="even_proj",
    )(x, norm_w.reshape(1, d), w_main, w_dt, w_dtt)


def _sb_neg_tri(n):
    j = jnp.arange(n)[:, None]
    s = jnp.arange(n)[None, :]
    return -(j > s).astype(BF16)


def _softplus2(z2):
    return jnp.log2(1.0 + jnp.exp2(-jnp.abs(z2))) + jnp.maximum(z2, 0.0)


def _sb_tiles(qs, kbs, vbs, ntri, biases2, carries, accs, valid, row0):
    nh, nt = len(qs), len(kbs)
    t = kbs[0].shape[0]
    tail = lambda x, n: x[row0[n]:]
    put = lambda x, n, new: new if row0[n] == 0 else jnp.concatenate([x[:row0[n]], new], axis=0)
    z2 =[[_dot_nt(tail(qs[h], n), kbs[n]) + biases2[h] for h in range(nh)] for n in range(nt)]
    sp2 = [[_softplus2(z2[n][h]) for h in range(nh)] for n in range(nt)]
    sp2 = [[s if valid[n] is None else jnp.where(tail(valid[n], n), s, 0.0) for s in sp2[n]] for n in range(nt)]
    carry_at = []
    for n in range(nt):
        carry_at.append([tail(c, n) for c in carries])
        tot = [jnp.broadcast_to(jnp.sum(sp2[n][h], axis=1, keepdims=True), carry_at[n][h].shape) for h in range(nh)]
        carries = [put(carries[h], n, carry_at[n][h] - tot[h]) for h in range(nh)]
    rem2 = [[_dot(sp2[n][h].astype(BF16), ntri) + jnp.concatenate([carry_at[n][h]] * (t // LANES), axis=1)
             for h in range(nh)] for n in range(nt)]
    w = [[jnp.exp2((z2[n][h] - sp2[n][h]) + rem2[n][h]) for h in range(nh)] for n in range(nt)]
    w = [[x if valid[n] is None else jnp.where(tail(valid[n], n), x, 0.0) for x in w[n]] for n in range(nt)]
    for n in range(nt):
        pv = [_dot(w[n][h].astype(BF16), vbs[n]) for h in range(nh)]
        accs = [put(accs[h], n, tail(accs[h], n) + pv[h]) for h in range(nh)]
    return carries, accs


def _sb_prompt_kernel(bias_ref, q_ref, k_ref, v_ref, tri_ref, o_ref, c_scr, a_scr):
    p = pl.program_id(1)
    i = pl.program_id(2)
    tq = q_ref.shape[1]
    q = q_ref[0]
    lane = lax.broadcasted_iota(jnp.int32, (tq, LANES), 1)
    first = lane < SB_HEAD_DIM
    qs = (jnp.where(first, q, jnp.zeros_like(q)), jnp.where(first, jnp.zeros_like(q), q))
    biases = (bias_ref[2 * p] * LOG2E, bias_ref[2 * p + 1] * LOG2E)
    tri = tri_ref[...]
    c_scr[...] = jnp.zeros_like(c_scr)
    a_scr[...] = jnp.zeros_like(a_scr)

    tk = tri_ref.shape[0]

    def tiles(js, valid, row0):
        kbs = [k_ref[0, pl.ds(pl.multiple_of(j * tk, tk), tk), :] for j in js]
        vbs = [v_ref[0, pl.ds(pl.multiple_of(j * tk, tk), tk), :] for j in js]
        carry, acc = _sb_tiles(qs, kbs, vbs, tri, biases, [c_scr[0], c_scr[1]], [a_scr[0], a_scr[1]], valid, row0)
        for h in range(2):
            c_scr[h] = carry[h]
            a_scr[h] = acc[h]

    row = lax.broadcasted_iota(jnp.int32, (tq, tk), 0)
    col = lax.broadcasted_iota(jnp.int32, (tq, tk), 1)
    ratio = tq // tk
    for d in range(ratio - 1, 0, -2):
        tiles([ratio * i + d, ratio * i + d - 1], [col + d * tk < row, col + (d - 1) * tk < row], [d * tk, (d - 1) * tk])

    def body(n, _):
        j = ratio * i - 1 - 2 * n
        tiles([j, j - 1], [None, None], [0, 0])
        return 0

    lax.fori_loop(0, (ratio // 2) * i, body, 0)
    o_ref[0] = jnp.where(first, a_scr[0], a_scr[1]).astype(o_ref.dtype)


def _sb_prompt(q, k, v, sb_bias, bsz, t):
    npair, m, _ = q.shape
    tk = SB_TILE
    tq = SB_QUERY_TILES * tk
    assert t % tq == 0 and SB_QUERY_TILES % 2 == 0 and m == bsz * t, (t, tq, m, bsz)
    nq = t // tq
    return pl.pallas_call(
        _sb_prompt_kernel,
        grid_spec=pltpu.PrefetchScalarGridSpec(
            num_scalar_prefetch=1,
            grid=(bsz, npair, nq),
            in_specs=[
                pl.BlockSpec((1, tq, LANES), lambda b, p, i, s: (p, b * nq + i, 0)),
                pl.BlockSpec((1, t, LANES), lambda b, p, i, s: (p, b, 0)),
                pl.BlockSpec((1, t, LANES), lambda b, p, i, s: (p, b, 0)),
                pl.BlockSpec((tk, tk), lambda b, p, i, s: (0, 0)),
            ],
            out_specs=pl.BlockSpec((1, tq, LANES), lambda b, p, i, s: (p, b * nq + i, 0)),
            scratch_shapes=[pltpu.VMEM((2, tq, LANES), F32), pltpu.VMEM((2, tq, LANES), F32)],
        ),
        out_shape=jax.ShapeDtypeStruct((npair, m, LANES), BF16),
        compiler_params=_params(("parallel", "parallel", "arbitrary")),
        name="sb_prompt",
    )(sb_bias.astype(F32), q, k, v, _sb_neg_tri(tk))


def _sb_decode_kernel(pt_ref, q_ref, bias_ref, *refs, group):
    k_refs, v_refs = refs[0:group], refs[group:2 * group]
    tri_ref, o_ref, c_scr, a_scr = refs[2 * group:]
    s = pl.program_id(0)
    g = pl.program_id(1)
    width = q_ref.shape[2]

    @pl.when(g == 0)
    def _():
        c_scr[...] = jnp.zeros_like(c_scr)
        a_scr[...] = jnp.zeros_like(a_scr)

    head = lax.broadcasted_iota(jnp.int32, (SB_HEADS, width), 0)
    lane = lax.broadcasted_iota(jnp.int32, (SB_HEADS, width), 1)
    own = (lane // SB_HEAD_DIM) == head
    qm = jnp.where(own, jnp.broadcast_to(q_ref[0], (SB_HEADS, width)), 0.0).astype(BF16)
    bias2 = jnp.concatenate([bias_ref[...] * LOG2E] * group, axis=0)
    z2 = jnp.concatenate([_dot(qm, k_refs[j][0].astype(BF16)) for j in range(group)], axis=0) + bias2
    sp2 = _softplus2(z2)
    hi = sp2.astype(BF16)
    lo = (sp2 - hi.astype(F32)).astype(BF16)
    tot = jnp.sum(sp2, axis=1, keepdims=True)
    carries = [c_scr[...]]
    for j in range(group):
        carries.append(carries[j] - tot[j * SB_HEADS:(j + 1) * SB_HEADS, :])
    c_scr[...] = carries[group]
    rem2 = _dot(hi, tri_ref[...]) + _dot(lo, tri_ref[...]) + jnp.concatenate(carries[0:group], axis=0)
    w = jnp.exp2((z2 - sp2) + rem2)
    acc = a_scr[...]
    for j in range(group):
        wj = w[j * SB_HEADS:(j + 1) * SB_HEADS, :].astype(BF16)
        acc = acc + _dot_nt(wj, v_refs[j][0].astype(BF16))
    a_scr[...] = acc

    @pl.when(g == pl.num_programs(1) - 1)
    def _():
        o = jnp.sum(jnp.where(own, acc, 0.0), axis=0, keepdims=True)
        for p in range(width // LANES):
            o_ref[p, pl.ds(s, 1), :] = o[:, p * LANES:(p + 1) * LANES]


def _sb_decode(q, cache_kt, cache_vt, page_table, sb_bias):
    n_seq, _, width = q.shape
    page = cache_kt.shape[2]
    npages = page_table.shape[1]
    group = SB_DECODE_GROUP if npages % SB_DECODE_GROUP == 0 else 1
    tri = _sb_neg_tri(page)

    def kv_spec(jj):
        return pl.BlockSpec((1, width, page), lambda b, g, pt: (pt[b, npages - 1 - (g * group + jj)], 0, 0))

    return pl.pallas_call(
        functools.partial(_sb_decode_kernel, group=group),
        grid_spec=pltpu.PrefetchScalarGridSpec(
            num_scalar_prefetch=1,
            grid=(n_seq, npages // group),
            in_specs=[
                pl.BlockSpec((1, 1, width), lambda b, g, pt: (b, 0, 0)),
                pl.BlockSpec((SB_HEADS, 1), lambda b, g, pt: (0, 0)),
                *[kv_spec(jj) for jj in range(group)],
                *[kv_spec(jj) for jj in range(group)],
                pl.BlockSpec((page, page), lambda b, g, pt: (0, 0)),
            ],
            out_specs=pl.BlockSpec((width // LANES, n_seq, LANES), lambda b, g, pt: (0, 0, 0)),
            scratch_shapes=[pltpu.VMEM((SB_HEADS, 1), F32), pltpu.VMEM((SB_HEADS, width), F32)],
        ),
        out_shape=jax.ShapeDtypeStruct((width // LANES, n_seq, LANES), F32),
        compiler_params=_params(("arbitrary", "arbitrary")),
        name="sb_decode",
    )(page_table, q, sb_bias.astype(F32).reshape(SB_HEADS, 1), *([cache_kt] * group), *([cache_vt] * group), tri)


def _ssd_chunk(xbc_ref, z_ref, dt_ref, dtt_ref, cw_ref, cb_ref, dtb_ref, dtbt_ref, alog_ref, alogt_ref,
               dskip_ref, nw_ref, y_ref, buf, st):
    L = xbc_ref.shape[0]
    pad = SUBLANES
    buf[pad:pad + L, :] = xbc_ref[...]
    conv = cb_ref[...]
    for j in range(SSD_CONV):
        off = pad - (SSD_CONV - 1) + j
        conv = conv + cw_ref[j:j + 1, :] * buf[off:off + L, :]
    tail = buf[pad + L - (SSD_CONV - 1):pad + L, :]
    buf[pad - (SSD_CONV - 1):pad, :] = tail
    xa = _silu(conv)
    xs = xa[:, 0:SSD_WIDTH]
    gw = SSD_STATE
    bm = [xa[:, SSD_WIDTH + g * gw:SSD_WIDTH + (g + 1) * gw].astype(BF16) for g in range(SSD_GROUPS)]
    cm = [xa[:, SSD_WIDTH + (SSD_GROUPS + g) * gw:SSD_WIDTH + (SSD_GROUPS + g + 1) * gw].astype(BF16)
          for g in range(SSD_GROUPS)]

    dt = _softplus(dt_ref[...] + dtb_ref[...])
    dtt = _softplus(dtt_ref[...] + dtbt_ref[...])
    a2 = -jnp.exp(alog_ref[...]) * LOG2E
    at2 = -jnp.exp(alogt_ref[...]) * LOG2E
    row = lax.broadcasted_iota(jnp.int32, (L, L), 0)
    col = lax.broadcasted_iota(jnp.int32, (L, L), 1)
    lower = col <= row
    lower_b = lower.astype(BF16)
    upper_b = (row <= col).astype(BF16)
    cum = sum(_dot(lower_b, part) for part in _split3(dt * a2))
    cumt = sum(_dot(part, upper_b) for part in _split3(dtt * at2))
    last = cum[L - 1:L, :]
    wst = jnp.exp2(last - cum) * dt
    ecum = jnp.exp2(cum)
    elast = jnp.exp2(last)

    lane = lax.broadcasted_iota(jnp.int32, (L, LANES), 1)
    first = lane < SSD_HEAD_DIM
    first_row = first[0:1, :]
    heads_per_group = SSD_HEADS // SSD_GROUPS
    cb = [jnp.where(lower, _dot_nt(cm[g], bm[g]), 0.0) for g in range(SSD_GROUPS)]
    ys = []
    for p in range(SSD_WIDTH // LANES):
        g = (2 * p) // heads_per_group
        xp = xs[:, p * LANES:(p + 1) * LANES]
        xpb = xp.astype(BF16)
        yi = []
        for h in (2 * p, 2 * p + 1):
            seg = cum[:, h:h + 1] - cumt[h:h + 1, :]
            dec = jnp.exp2(jnp.minimum(seg, 0.0)) * dtt[h:h + 1, :]
            yi.append(_dot((cb[g] * dec).astype(BF16), xpb))
        y_intra = jnp.where(first, yi[0], yi[1])
        stp = st[:, p * LANES:(p + 1) * LANES]
        ec = jnp.where(first, ecum[:, 2 * p:2 * p + 1], ecum[:, 2 * p + 1:2 * p + 2])
        y_inter = _dot(cm[g], stp.astype(BF16)) * ec
        wp = jnp.where(first, wst[:, 2 * p:2 * p + 1], wst[:, 2 * p + 1:2 * p + 2])
        el = jnp.where(first_row, elast[:, 2 * p:2 * p + 1], elast[:, 2 * p + 1:2 * p + 2])
        st[:, p * LANES:(p + 1) * LANES] = el * stp + _dot_tn(bm[g], (xp * wp).astype(BF16))
        ys.append(y_intra + y_inter + dskip_ref[:, p * LANES:(p + 1) * LANES] * xp)
    y = jnp.concatenate(ys, axis=1)
    y_ref[...] = _rmsnorm(y * _silu(z_ref[...].astype(F32)), nw_ref[...]).astype(y_ref.dtype)
    return tail


def _ssd_prompt_kernel(xbc_ref, z_ref, dt_ref, dtt_ref, *refs):
    params, (y_ref, conv_ref, ssm_ref, buf, st) = refs[:-5], refs[-5:]
    c = pl.program_id(1)

    @pl.when(c == 0)
    def _():
        buf[0:SUBLANES, :] = jnp.zeros((SUBLANES, SSD_CONV_CH), F32)
        st[...] = jnp.zeros_like(st)

    for s in range(xbc_ref.shape[0] // SSD_CHUNK):
        rows = pl.ds(s * SSD_CHUNK, SSD_CHUNK)
        tail = _ssd_chunk(xbc_ref.at[rows], z_ref.at[rows], dt_ref.at[rows], dtt_ref.at[:, rows], *params,
                          y_ref.at[rows], buf, st)

    @pl.when(c == pl.num_programs(1) - 1)
    def _():
        conv_ref[0] = tail
        ssm_ref[0] = st[...].T


def _ssd_prompt(xbc, z, dt, dtt, conv_w, conv_b, dt_bias, a_log, d_skip, ssd_norm, bsz, t):
    m = xbc.shape[0]
    L = SSD_CHUNK * SSD_STEP_CHUNKS
    assert t % L == 0 and m == bsz * t, (t, L, m, bsz)
    nc = t // L
    rows = lambda b, c: (b * nc + c, 0)
    full = lambda b, c: (0, 0)
    vec = lambda v: v.astype(F32).reshape(1, -1)
    colv = lambda v: v.astype(F32).reshape(-1, 1)
    y, conv, ssm = pl.pallas_call(
        _ssd_prompt_kernel,
        grid=(bsz, nc),
        in_specs=[
            pl.BlockSpec((L, SSD_CONV_CH), rows),
            pl.BlockSpec((L, SSD_WIDTH), rows),
            pl.BlockSpec((L, SSD_HEADS), rows),
            pl.BlockSpec((SSD_HEADS, L), lambda b, c: (0, b * nc + c)),
            pl.BlockSpec((SSD_CONV, SSD_CONV_CH), full),
            pl.BlockSpec((1, SSD_CONV_CH), full),
            pl.BlockSpec((1, SSD_HEADS), full),
            pl.BlockSpec((SSD_HEADS, 1), full),
            pl.BlockSpec((1, SSD_HEADS), full),
            pl.BlockSpec((SSD_HEADS, 1), full),
            pl.BlockSpec((1, SSD_WIDTH), full),
            pl.BlockSpec((1, SSD_WIDTH), full),
        ],
        out_specs=[
            pl.BlockSpec((L, SSD_WIDTH), rows),
            pl.BlockSpec((1, SSD_CONV - 1, SSD_CONV_CH), lambda b, c: (b, 0, 0)),
            pl.BlockSpec((1, SSD_WIDTH, SSD_STATE), lambda b, c: (b, 0, 0)),
        ],
        out_shape=[
            jax.ShapeDtypeStruct((m, SSD_WIDTH), BF16),
            jax.ShapeDtypeStruct((bsz, SSD_CONV - 1, SSD_CONV_CH), F32),
            jax.ShapeDtypeStruct((bsz, SSD_WIDTH, SSD_STATE), F32),
        ],
        scratch_shapes=[pltpu.VMEM((SSD_CHUNK + SUBLANES, SSD_CONV_CH), F32),
                        pltpu.VMEM((SSD_STATE, SSD_WIDTH), F32)],
        compiler_params=_params(("parallel", "arbitrary")),
        name="ssd_prompt",
    )(xbc, z, dt, dtt, conv_w.astype(F32), vec(conv_b), vec(dt_bias), colv(dt_bias), vec(a_log), colv(a_log),
      vec(jnp.repeat(d_skip, SSD_HEAD_DIM)), vec(ssd_norm))
    return y, conv, ssm.reshape(bsz, SSD_HEADS, SSD_HEAD_DIM, SSD_STATE)


def _ssd_step_kernel(xbc_ref, z_ref, dt_ref, sconv_ref, sssm_ref, cw_ref, cb_ref, dtb_ref, alog_ref, dskip_ref,
                     nw_ref, y_ref, conv_ref, ssm_ref):
    lane_head = lax.broadcasted_iota(jnp.int32, (1, SSD_WIDTH), 1) // SSD_HEAD_DIM
    rows_per_group = SSD_WIDTH // SSD_GROUPS
    row = lax.broadcasted_iota(jnp.int32, (SSD_WIDTH, SSD_STATE), 0)
    a = -jnp.exp(alog_ref[...])
    for s in range(xbc_ref.shape[0]):
        xr = xbc_ref[s]
        cs = sconv_ref[s]
        conv = cb_ref[...] + cw_ref[SSD_CONV - 1:SSD_CONV, :] * xr
        for j in range(SSD_CONV - 1):
            conv = conv + cw_ref[j:j + 1, :] * cs[j:j + 1, :]
        conv_ref[s] = jnp.concatenate([cs[1:SSD_CONV - 1, :], xr], axis=0)
        xa = _silu(conv)
        xs = xa[:, 0:SSD_WIDTH]
        dt = _softplus(dt_ref[s] + dtb_ref[...])
        da = jnp.exp(dt * a)
        dt_w = jnp.zeros((1, SSD_WIDTH), F32)
        da_w = jnp.zeros((1, SSD_WIDTH), F32)
        for h in range(SSD_HEADS):
            dt_w = jnp.where(lane_head == h, dt[:, h:h + 1], dt_w)
            da_w = jnp.where(lane_head == h, da[:, h:h + 1], da_w)
        dtx_col = _row_to_col(dt_w * xs)
        da_col = _row_to_col(da_w)
        b_rows = jnp.zeros((SSD_WIDTH, SSD_STATE), F32)
        c_rows = jnp.zeros((SSD_WIDTH, SSD_STATE), F32)
        for g in range(SSD_GROUPS):
            sel = (row // rows_per_group) == g
            b0 = SSD_WIDTH + g * SSD_STATE
            c0 = SSD_WIDTH + (SSD_GROUPS + g) * SSD_STATE
            b_rows = jnp.where(sel, xa[:, b0:b0 + SSD_STATE], b_rows)
            c_rows = jnp.where(sel, xa[:, c0:c0 + SSD_STATE], c_rows)
        new = da_col * sssm_ref[s] + dtx_col * b_rows
        ssm_ref[s] = new
        y = _col_to_row(jnp.sum(new * c_rows, axis=1, keepdims=True)) + dskip_ref[...] * xs
        y_ref[s] = _rmsnorm(y * _silu(z_ref[s].astype(F32)), nw_ref[...])


def _ssd_step(xbc, z, dt, state_conv, state_ssm, conv_w, conv_b, dt_bias, a_log, d_skip, ssd_norm):
    n = xbc.shape[0]
    ns = STEP_SEQS if n % STEP_SEQS == 0 else 1
    vec = lambda v: v.astype(F32).reshape(1, -1)
    per = lambda i: (i, 0, 0)
    full = lambda i: (0, 0)
    y, conv, ssm = pl.pallas_call(
        _ssd_step_kernel,
        grid=(n // ns,),
        in_specs=[
            pl.BlockSpec((ns, 1, SSD_CONV_CH), per),
            pl.BlockSpec((ns, 1, SSD_WIDTH), per),
            pl.BlockSpec((ns, 1, SSD_HEADS), per),
            pl.BlockSpec((ns, SSD_CONV - 1, SSD_CONV_CH), per),
            pl.BlockSpec((ns, SSD_WIDTH, SSD_STATE), per),
            pl.BlockSpec((SSD_CONV, SSD_CONV_CH), full),
            pl.BlockSpec((1, SSD_CONV_CH), full),
            pl.BlockSpec((1, SSD_HEADS), full),
            pl.BlockSpec((1, SSD_HEADS), full),
            pl.BlockSpec((1, SSD_WIDTH), full),
            pl.BlockSpec((1, SSD_WIDTH), full),
        ],
        out_specs=[
            pl.BlockSpec((ns, 1, SSD_WIDTH), per),
            pl.BlockSpec((ns, SSD_CONV - 1, SSD_CONV_CH), per),
            pl.BlockSpec((ns, SSD_WIDTH, SSD_STATE), per),
        ],
        out_shape=[
            jax.ShapeDtypeStruct((n, 1, SSD_WIDTH), F32),
            jax.ShapeDtypeStruct((n, SSD_CONV - 1, SSD_CONV_CH), F32),
            jax.ShapeDtypeStruct((n, SSD_WIDTH, SSD_STATE), F32),
        ],
        compiler_params=_params(("parallel",)),
        name="ssd_step",
    )(xbc.reshape(n, 1, SSD_CONV_CH), z.reshape(n, 1, SSD_WIDTH), dt.reshape(n, 1, SSD_HEADS),
      state_conv.astype(F32), state_ssm.astype(F32).reshape(n, SSD_WIDTH, SSD_STATE), conv_w.astype(F32),
      vec(conv_b), vec(dt_bias), vec(a_log), vec(jnp.repeat(d_skip, SSD_HEAD_DIM)), vec(ssd_norm))
    return y.reshape(n, SSD_WIDTH), conv, ssm.reshape(n, SSD_HEADS, SSD_HEAD_DIM, SSD_STATE)


def _even_mix(osb_ref, y_ref):
    return jnp.concatenate([osb_ref[p] for p in range(osb_ref.shape[0])] + [y_ref[...]], axis=1)


def _odd_mix(o_ref, g_ref, nw_ref):
    o = jnp.concatenate([o_ref[h] for h in range(o_ref.shape[0])], axis=1).astype(F32)
    return (_rmsnorm(o, nw_ref[...]) * _silu(g_ref[...].astype(F32))).astype(BF16)


def _mix_mlp_kernel(*refs, mix_fn, n_mix, final_norm):
    mix_refs = refs[0:n_mix]
    x_ref, wout_ref, nw_ref, w1_ref, w2_ref, fw_ref, o_ref, hn_scr, acc_scr = refs[n_mix:]
    j = pl.program_id(1)

    @pl.when(j == 0)
    def _():
        h1 = x_ref[...] + _dot(mix_fn(*mix_refs), wout_ref[...])
        o_ref[...] = h1
        hn_scr[...] = _rmsnorm(h1, nw_ref[...]).astype(BF16)

    h = jnp.maximum(_dot(hn_scr[...], w1_ref[...]), 0.0)
    a = (h * h).astype(BF16)

    @pl.when(j == 0)
    def _():
        acc_scr[...] = _dot(a, w2_ref[...])

    @pl.when(j > 0)
    def _():
        acc_scr[...] += _dot(a, w2_ref[...])

    @pl.when(j == pl.num_programs(1) - 1)
    def _():
        out = o_ref[...] + acc_scr[...]
        o_ref[...] = _rmsnorm(out, fw_ref[...]) if final_norm else out


def _mix_mlp(mix_fn, mix_args, mix_specs, x, w_out, norm_w, w1_all, w2_all, layer, final_w, tm, tf):
    m, d = x.shape
    f = w1_all.shape[2]
    assert m % tm == 0 and f % tf == 0, (m, tm, f, tf)
    final_norm = final_w is not None
    fw = (final_w if final_norm else jnp.ones((d,), F32)).astype(F32).reshape(1, d)
    const = lambda i, j: (0, 0)
    return pl.pallas_call(
        functools.partial(_mix_mlp_kernel, mix_fn=mix_fn, n_mix=len(mix_args), final_norm=final_norm),
        grid=(m // tm, f // tf),
        in_specs=[
            *mix_specs,
            pl.BlockSpec((tm, d), lambda i, j: (i, 0)),
            pl.BlockSpec(w_out.shape, const),
            pl.BlockSpec((1, d), const),
            pl.BlockSpec((None, d, tf), lambda i, j: (layer, 0, j)),
            pl.BlockSpec((None, tf, d), lambda i, j: (layer, j, 0)),
            pl.BlockSpec((1, d), const),
        ],
        out_specs=pl.BlockSpec((tm, d), lambda i, j: (i, 0)),
        out_shape=jax.ShapeDtypeStruct((m, d), F32),
        scratch_shapes=[pltpu.VMEM((tm, d), BF16), pltpu.VMEM((tm, d), F32)],
        compiler_params=_params(("parallel", "arbitrary")),
        name="mix_mlp",
    )(*mix_args, x, w_out.astype(BF16), norm_w.astype(F32).reshape(1, d), w1_all, w2_all, fw)


def _even_mix_mlp(osb, y, x, w_out, norm_w, w1_all, w2_all, layer, final_w, tm, tf):
    specs = [pl.BlockSpec((osb.shape[0], tm, LANES), lambda i, j: (0, i, 0)),
             pl.BlockSpec((tm, SSD_WIDTH), lambda i, j: (i, 0))]
    return _mix_mlp(_even_mix, (osb, y), specs, x, w_out, norm_w, w1_all, w2_all, layer, final_w, tm, tf)


def _odd_mix_mlp(o, g, hg_norm, x, w_out, norm_w, w1_all, w2_all, layer, final_w, tm, tf):
    nh = o.shape[0]
    specs = [pl.BlockSpec((nh, tm, HG_VAL), lambda i, j: (0, i, 0)),
             pl.BlockSpec((tm, nh * HG_VAL), lambda i, j: (i, 0)),
             pl.BlockSpec((1, nh * HG_VAL), lambda i, j: (0, 0))]
    args = (o, g, hg_norm.astype(F32).reshape(1, -1))
    return _mix_mlp(_odd_mix, args, specs, x, w_out, norm_w, w1_all, w2_all, layer, final_w, tm, tf)


def _odd_proj_kernel(x_ref, nw_ref, w_ref, q_ref, f_ref, v_ref, g_ref):
    hn = _rmsnorm(x_ref[...], nw_ref[...]).astype(BF16)
    width = HG_HEADS * HG_KEY
    per_chunk = PROJ_COLS // HG_KEY
    for i, ref in enumerate((q_ref, f_ref, v_ref)):
        for c in range(width // PROJ_COLS):
            y = _dot(hn, w_ref[:, i * width + c * PROJ_COLS:i * width + (c + 1) * PROJ_COLS])
            for h in range(per_chunk):
                ref[c * per_chunk + h] = y[:, h * HG_KEY:(h + 1) * HG_KEY]
    for c in range(width // PROJ_COLS):
        g_ref[:, c * PROJ_COLS:(c + 1) * PROJ_COLS] = _dot(
            hn, w_ref[:, 3 * width + c * PROJ_COLS:3 * width + (c + 1) * PROJ_COLS]).astype(g_ref.dtype)


def _odd_proj(x, norm_w, w_in, tm):
    m, d = x.shape
    width = HG_HEADS * HG_KEY
    heads = lambda i: (0, i, 0)
    per_head = jax.ShapeDtypeStruct((HG_HEADS, m, HG_KEY), F32)
    return pl.pallas_call(
        _odd_proj_kernel,
        grid=(m // tm,),
        in_specs=[
            pl.BlockSpec((tm, d), lambda i: (i, 0)),
            pl.BlockSpec((1, d), lambda i: (0, 0)),
            pl.BlockSpec(w_in.shape, lambda i: (0, 0)),
        ],
        out_specs=[
            pl.BlockSpec((HG_HEADS, tm, HG_KEY), heads),
            pl.BlockSpec((HG_HEADS, tm, HG_KEY), heads),
            pl.BlockSpec((HG_HEADS, tm, HG_KEY), heads),
            pl.BlockSpec((tm, width), lambda i: (i, 0)),
        ],
        out_shape=[per_head, per_head, per_head, jax.ShapeDtypeStruct((m, width), BF16)],
        compiler_params=_params(("parallel",)),
        name="odd_proj",
    )(x, norm_w.astype(F32).reshape(1, d), w_in.astype(BF16))


def _hgrn_lower_bound(raw, layer):
    e = jnp.exp(raw - jnp.max(raw, axis=0, keepdims=True))
    p = e / jnp.sum(e, axis=0, keepdims=True)
    lb = jnp.zeros_like(p[0])
    for l in range(1, layer + 1):
        lb = lb + p[l]
    return lb


def _hgrn_gates(fpre, lb):
    sig, nsig = _sigmoid_pair(fpre)
    return lb + (1.0 - lb) * sig, (1.0 - lb) * nsig


def _hgrn_chunk(q_ref, f_ref, v_ref, lb_ref, o_ref, st, b_scr, k_scr, od_scr, layer):
    L = q_ref.shape[1]
    nblk = L // HG_DIAG
    row = lax.broadcasted_iota(jnp.int32, (L, L), 0)
    col = lax.broadcasted_iota(jnp.int32, (L, L), 1)
    lower_b = (col <= row).astype(BF16)
    pos = lax.broadcasted_iota(jnp.int32, (L, HG_KEY), 0)

    def heads(n, _):
        hs = [n * HG_HEAD_GROUP + e for e in range(HG_HEAD_GROUP)]
        every = range(HG_HEAD_GROUP)
        q = [q_ref[h] for h in hs]
        vb = [v_ref[h].astype(BF16) for h in hs]
        gates = [_hgrn_gates(f_ref[h], _hgrn_lower_bound(lb_ref[:, h], layer)) for h in hs]
        kin = [gk[1] for gk in gates]
        b = [sum(_dot(lower_b, part) for part in _split3(jnp.log(gk[0]) * LOG2E)) for gk in gates]
        for e in every:
            b_scr[e] = b[e]
            k_scr[e] = kin[e]
        stt = [st[h] for h in hs]
        o = [_dot_nt((q[e] * jnp.exp2(b[e])).astype(BF16), stt[e].astype(BF16)) for e in every]
        b_last = [x[L - 1:L, :] for x in b]
        for e in every:
            st[hs[e]] = stt[e] * jnp.exp2(b_last[e]) + _dot_tn(
                vb[e], (kin[e] * jnp.exp2(b_last[e] - b[e])).astype(BF16))

        att = [jnp.zeros((L, L), F32) for _ in every]
        g = L
        while g > HG_DIAG:
            half = g // 2
            late = (pos & (g - 1)) >= half
            pair = ((row ^ col) < g) & ((row & (g - 1)) >= half) & ((col & (g - 1)) < half)
            mid = [jnp.broadcast_to(x.reshape(L // g, g, HG_KEY)[:, half - 1:half, :],
                                    (L // g, g, HG_KEY)).reshape(L, HG_KEY) for x in b]
            r = [(jnp.where(late, q[e], kin[e]) * jnp.exp2(-jnp.abs(b[e] - mid[e]))).astype(BF16) for e in every]
            att = [jnp.where(pair, _dot_nt(r[e], r[e]), att[e]) for e in every]
            g = half
        o = [o[e] + _dot(att[e].astype(BF16), vb[e]) for e in every]

        at = lambda ref, i: ref[pl.ds(i, nblk, stride=HG_DIAG), :]
        for e in every:
            bs = [at(b_scr.at[e], i) for i in range(HG_DIAG)]
            ks = [at(k_scr.at[e], i) for i in range(HG_DIAG)]
            vs = [at(v_ref.at[hs[e]], i) for i in range(HG_DIAG)]
            for i in range(HG_DIAG):
                qi = at(q_ref.at[hs[e]], i)
                acc = jnp.zeros((nblk, HG_VAL), F32)
                for j in range(i + 1):
                    a = jnp.sum(qi * ks[j] * jnp.exp2(bs[i] - bs[j]), axis=1, keepdims=True)
                    acc = acc + a * vs[j]
                od_scr[e, pl.ds(i, nblk, stride=HG_DIAG), :] = acc
        for e in every:
            o_ref[hs[e]] = (o[e] + od_scr[e]).astype(o_ref.dtype)
        return 0

    groups = q_ref.shape[0] // HG_HEAD_GROUP
    if groups == 1:
        heads(0, 0)
    else:
        lax.fori_loop(0, groups, heads, 0)


def _hgrn_prompt_kernel(q_ref, f_ref, v_ref, lb_ref, o_ref, s_ref, st, b_scr, k_scr, od_scr, *, layer):
    c = pl.program_id(1)

    @pl.when(c == 0)
    def _():
        st[...] = jnp.zeros_like(st)

    for s in range(q_ref.shape[1] // HG_CHUNK):
        rows = pl.ds(s * HG_CHUNK, HG_CHUNK)
        _hgrn_chunk(q_ref.at[:, rows], f_ref.at[:, rows], v_ref.at[:, rows], lb_ref, o_ref.at[:, rows], st,
                    b_scr.at[s], k_scr.at[s], od_scr.at[s], layer)

    @pl.when(c == pl.num_programs(1) - 1)
    def _():
        for h in range(s_ref.shape[1]):
            s_ref[0, h] = st[h].T


def _hgrn_prompt(q, f, v, lb_raw, layer, bsz, t):
    nh, m, _ = q.shape
    L = HG_CHUNK * HG_STEP_CHUNKS
    assert t % L == 0 and m == bsz * t and nh % HG_HEAD_GROUP == 0, (t, L, m, bsz, nh)
    nc = t // L
    blk = lambda b, c: (0, b * nc + c, 0)
    depth = lb_raw.shape[0]
    return pl.pallas_call(
        functools.partial(_hgrn_prompt_kernel, layer=layer),
        grid=(bsz, nc),
        in_specs=[
            pl.BlockSpec((nh, L, HG_KEY), blk),
            pl.BlockSpec((nh, L, HG_KEY), blk),
            pl.BlockSpec((nh, L, HG_VAL), blk),
            pl.BlockSpec((depth, nh, 1, HG_KEY), lambda b, c: (0, 0, 0, 0)),
        ],
        out_specs=[
            pl.BlockSpec((nh, L, HG_VAL), blk),
            pl.BlockSpec((1, nh, HG_KEY, HG_VAL), lambda b, c: (b, 0, 0, 0)),
        ],
        out_shape=[
            jax.ShapeDtypeStruct((nh, m, HG_VAL), BF16),
            jax.ShapeDtypeStruct((bsz, nh, HG_KEY, HG_VAL), F32),
        ],
        scratch_shapes=[
            pltpu.VMEM((nh, HG_VAL, HG_KEY), F32),
            pltpu.VMEM((HG_STEP_CHUNKS, HG_HEAD_GROUP, HG_CHUNK, HG_KEY), F32),
            pltpu.VMEM((HG_STEP_CHUNKS, HG_HEAD_GROUP, HG_CHUNK, HG_KEY), F32),
            pltpu.VMEM((HG_STEP_CHUNKS, HG_HEAD_GROUP, HG_CHUNK, HG_VAL), F32),
        ],
        compiler_params=_params(("parallel", "arbitrary")),
        name="hgrn_prompt",
    )(q, f, v, lb_raw.astype(F32).reshape(depth, nh, 1, HG_KEY))


def _hgrn_step_kernel(q_ref, f_ref, v_ref, lb_ref, s_ref, o_ref, snew_ref, *, layer):
    ns = s_ref.shape[0]
    for h in range(q_ref.shape[0]):
        lb = _hgrn_lower_bound(lb_ref[:, h], layer)
        for s in range(ns):
            i = pl.program_id(0) * ns + s
            q = q_ref[h, pl.ds(i, 1), :]
            v = v_ref[h, pl.ds(i, 1), :]
            fg, kin = _hgrn_gates(f_ref[h, pl.ds(i, 1), :], lb)
            new = _row_to_col(fg) * s_ref[s, h] + _row_to_col(kin) * v
            snew_ref[s, h] = new
            o_ref[h, pl.ds(i, 1), :] = jnp.sum(_row_to_col(q) * new, axis=0, keepdims=True)


def _hgrn_step(q, f, v, lb_raw, state, layer):
    nh, n, _ = q.shape
    ns = STEP_SEQS if n % STEP_SEQS == 0 else 1
    depth = lb_raw.shape[0]
    whole = lambda i: (0, 0, 0)
    return pl.pallas_call(
        functools.partial(_hgrn_step_kernel, layer=layer),
        grid=(n // ns,),
        in_specs=[
            pl.BlockSpec((nh, n, HG_KEY), whole),
            pl.BlockSpec((nh, n, HG_KEY), whole),
            pl.BlockSpec((nh, n, HG_VAL), whole),
            pl.BlockSpec((depth, nh, 1, HG_KEY), lambda i: (0, 0, 0, 0)),
            pl.BlockSpec((ns, nh, HG_KEY, HG_VAL), lambda i: (i, 0, 0, 0)),
        ],
        out_specs=[
            pl.BlockSpec((nh, n, HG_VAL), whole),
            pl.BlockSpec((ns, nh, HG_KEY, HG_VAL), lambda i: (i, 0, 0, 0)),
        ],
        out_shape=[
            jax.ShapeDtypeStruct((nh, n, HG_VAL), F32),
            jax.ShapeDtypeStruct((n, nh, HG_KEY, HG_VAL), F32),
        ],
        compiler_params=_params(("arbitrary",)),
        name="hgrn_step",
    )(q, f, v, lb_raw.astype(F32).reshape(depth, nh, 1, HG_KEY), state.astype(F32))


def _row_tile(m, want):
    return want if m % want == 0 else m


def kernel(x_prompt, x_sample, cache_k, cache_v, page_table, state_conv, state_ssm, state_hgrn, norm_mix, norm_ffn,
           norm_final, w_in_even, sb_bias, conv_w, conv_b, dt_bias, a_log, d_skip, ssd_norm, w_out_even, w_in_odd,
           hg_lb_raw, hg_norm, w_out_odd, w_ff1, w_ff2):
    bsz, t, d = x_prompt.shape
    n_seq = x_sample.shape[0]
    depth = norm_mix.shape[0]
    mp = bsz * t
    hp = x_prompt.reshape(mp, d)
    hs = x_sample.reshape(n_seq, d)
    tmp = _row_tile(mp, 512)
    tmm = _row_tile(mp, 1024)
    tf = 1024
    outs = {k: [] for k in ("kp", "vp", "ks", "vs", "cp", "cs", "sp", "ss", "gp", "gs")}
    w1_all = w_ff1.astype(BF16)
    w2_all = w_ff2.astype(BF16)

    for layer in range(depth):
        li = layer // 2
        fw = norm_final if layer == depth - 1 else None
        mlp_w = (norm_ffn[layer], w1_all, w2_all, layer, fw)
        if layer % 2 == 0:
            ssd_w = (conv_w[li], conv_b[li], dt_bias[li], a_log[li], d_skip[li], ssd_norm[li])
            q, k, v, kf, vf, z, xbc, dt, dtt = _even_proj(hp, norm_mix[layer], w_in_even[li], tmp, seq_len=t)
            osb = _sb_prompt(q, k, v, sb_bias[li], bsz, t)
            y, cp, sp = _ssd_prompt(xbc, z, dt, dtt, *ssd_w, bsz, t)
            hp = _even_mix_mlp(osb, y, hp, w_out_even[li], *mlp_w, tmm, tf)
            rows_view = lambda a: jnp.transpose(a.reshape(bsz, SB_HEADS, SB_HEAD_DIM, t), (0, 3, 1, 2))
            outs["kp"].append(rows_view(kf))
            outs["vp"].append(rows_view(vf))
            outs["cp"].append(cp)
            outs["sp"].append(sp)
            q, k, v, kf, vf, z, xbc, dt, dtt = _even_proj(hs, norm_mix[layer], w_in_even[li], n_seq)
            qs = q.astype(F32).transpose(1, 0, 2).reshape(n_seq, 1, SB_WIDTH)
            n_phys, page = cache_k.shape[1], cache_k.shape[2]
            to_lanes = lambda c: jnp.transpose(c, (0, 2, 3, 1)).reshape(n_phys, SB_WIDTH, page)
            osb = _sb_decode(qs, to_lanes(cache_k[li]), to_lanes(cache_v[li]), page_table, sb_bias[li])
            y, cs, ss = _ssd_step(xbc, z, dt, state_conv[li], state_ssm[li], *ssd_w)
            hs = _even_mix_mlp(osb.astype(BF16), y.astype(BF16), hs, w_out_even[li], *mlp_w, n_seq, tf)
            outs["ks"].append(kf.reshape(n_seq, 1, SB_HEADS, SB_HEAD_DIM))
            outs["vs"].append(vf.reshape(n_seq, 1, SB_HEADS, SB_HEAD_DIM))
            outs["cs"].append(cs)
            outs["ss"].append(ss)
        else:
            q, f, v, g = _odd_proj(hp, norm_mix[layer], w_in_odd[li], tmp)
            o, gp = _hgrn_prompt(q, f, v, hg_lb_raw, layer, bsz, t)
            hp = _odd_mix_mlp(o, g, hg_norm[li], hp, w_out_odd[li], *mlp_w, tmm, tf)
            outs["gp"].append(gp)
            q, f, v, g = _odd_proj(hs, norm_mix[layer], w_in_odd[li], n_seq)
            o, gs = _hgrn_step(q, f, v, hg_lb_raw, state_hgrn[li], layer)
            hs = _odd_mix_mlp(o.astype(BF16), g, hg_norm[li], hs, w_out_odd[li], *mlp_w, n_seq, tf)
            outs["gs"].append(gs)

    y_prompt = hp.reshape(bsz, t, d)
    y_sample = hs.reshape(n_seq, 1, d)
    st = lambda key: jnp.stack(outs[key])
    return (y_prompt, y_sample, st("kp"), st("vp"), st("ks"), st("vs"), st("cp"), st("cs"), st("sp"), st("ss"),
            st("gp"), st("gs"))
```
